```python
import jax, jax.numpy as jnp
from jax import lax
import numpy as np

D_MODEL = 1024
BATCH = 8
SEQ = 2048
DEPTH = 4
DEC_BATCH = 128
DEC_SEQ = 4
PAST_LEN = 16384
PAGE_SIZE = 128

N_META = 16
N_PAIRS = DEPTH // 2
GLA_HEADS = 4
GLA_DK = 64
GLA_DV = 128
GLA_KEY = GLA_HEADS * GLA_DK
GLA_VAL = GLA_HEADS * GLA_DV
GLA_RANK = 16
GLA_GATE_NORM = 16.0
GLA_CHUNK = 16
LRU_WIDTH = 512
LRU_BLOCKS = 8
LRU_BDIM = LRU_WIDTH // LRU_BLOCKS
CONV_W = 4
LRU_C = 8.0
MIX_IN = 2 * GLA_KEY + 2 * GLA_VAL + GLA_RANK + 2 * LRU_WIDTH
MIX_OUT = GLA_VAL + LRU_WIDTH
RWKV_HEAD = 64
RWKV_HEADS = D_MODEL // RWKV_HEAD
DECAY_LORA = 64
AAA_LORA = 64
MV_LORA = 32
GATE_LORA = 128
RWKV_GN_EPS = 64e-5
PEER_HEADS = 8
PEER_NKEYS = 128
PEER_EXPERTS = PEER_NKEYS * PEER_NKEYS
PEER_DKEY = 256
PEER_HALF = PEER_DKEY // 2
PEER_TOPK = 16
PEER_BLOCK = 128
DN_ALPHA = float((2 * DEPTH) ** 0.25)
DN_BETA = float((8 * DEPTH) ** -0.25)
LN_EPS = 1e-5
F32 = jnp.float32

kernel_name = 'hybrid_gla_rglru_rwkv7_peer_step'


def layer_norm(x, g, b):
    xf = x.astype(F32)
    mu = xf.mean(-1, keepdims=True)
    var = jnp.mean(jnp.square(xf - mu), -1, keepdims=True)
    return ((xf - mu) * lax.rsqrt(var + LN_EPS) * g + b).astype(x.dtype)


def split_cols(z, sizes):
    out, o = [], 0
    for s in sizes:
        out.append(z[..., o:o + s])
        o += s
    return out


def gla_chunked(q, k, v, log_a, s0):
    B, T = q.shape[:2]
    n = -(-T // GLA_CHUNK)
    pad = n * GLA_CHUNK - T
    padf = lambda z: jnp.pad(z.astype(F32), ((0, 0), (0, pad), (0, 0), (0, 0)))
    blk = lambda z: z.reshape(B, n, GLA_CHUNK, z.shape[2], z.shape[3]).transpose(1, 0, 3, 2, 4)
    q, k, v, log_a = (blk(padf(z)) for z in (q, k, v, log_a))
    cum = jnp.cumsum(log_a, axis=3)
    last = cum[:, :, :, -1:, :]
    qe = q * jnp.exp(cum)
    ke = k * jnp.exp(-cum)
    kl = k * jnp.exp(last - cum)
    mask = jnp.tril(jnp.ones((GLA_CHUNK, GLA_CHUNK), bool))
    att = jnp.where(mask, jnp.einsum('nbhtd,nbhsd->nbhts', qe, ke), 0.0)
    o_intra = jnp.einsum('nbhts,nbhsv->nbhtv', att, v)
    upd = jnp.einsum('nbhsd,nbhsv->nbhdv', kl, v)
    decay = jnp.exp(last[:, :, :, 0, :])

    def step(S, inp):
        dec, u = inp
        return S * dec[..., None] + u, S

    s_fin, s_prev = lax.scan(step, s0.astype(F32), (decay, upd))
    o_inter = jnp.einsum('nbhtd,nbhdv->nbhtv', qe, s_prev)
    o = (o_intra + o_inter).transpose(1, 0, 3, 2, 4).reshape(B, n * GLA_CHUNK, GLA_HEADS, GLA_DV)[:, :T]
    return o, s_fin


def lru_scan(a, b, h0):
    b = b.at[:, 0].add(a[:, 0] * h0)

    def comb(l, r):
        al, bl = l
        ar, br = r
        return al * ar, ar * bl + br

    _, h = lax.associative_scan(comb, (a, b), axis=1)
    return h, h[:, -1]


def gla_lru_mixer(x, s_gla, h_lru, conv_buf, w_in, w_gate, b_gate, gla_norm, conv_w, conv_b,
                  lru_wa, lru_ba, lru_wx, lru_bx, lru_lambda, w_out):
    B, T, _ = x.shape
    proj = x @ w_in
    q, k, v, g_lr, r, xb, gb = split_cols(proj, [GLA_KEY, GLA_KEY, GLA_VAL, GLA_RANK, GLA_VAL, LRU_WIDTH, LRU_WIDTH])
    log_a = jax.nn.log_sigmoid((g_lr @ w_gate + b_gate).astype(F32)) / GLA_GATE_NORM
    hd = lambda z, d: z.reshape(B, T, GLA_HEADS, d)
    o, s_gla_new = gla_chunked(hd(q, GLA_DK) * GLA_DK ** -0.5, hd(k, GLA_DK), hd(v, GLA_DV), hd(log_a, GLA_DK), s_gla)
    o = o * lax.rsqrt(jnp.mean(jnp.square(o), -1, keepdims=True) + LN_EPS) * gla_norm
    o_gla = o.reshape(B, T, GLA_VAL) * jax.nn.silu(r.astype(F32))
    xcat = jnp.concatenate([conv_buf.astype(xb.dtype), xb], axis=1)
    xc = conv_b + sum(xcat[:, i:i + T] * conv_w[i] for i in range(CONV_W))
    new_buf = xcat[:, T:]
    xr = xc.reshape(B, T, LRU_BLOCKS, LRU_BDIM)
    gate_a = jax.nn.sigmoid((jnp.einsum('btnd,nde->btne', xr, lru_wa).reshape(B, T, LRU_WIDTH) + lru_ba).astype(F32))
    gate_x = jax.nn.sigmoid((jnp.einsum('btnd,nde->btne', xr, lru_wx).reshape(B, T, LRU_WIDTH) + lru_bx).astype(F32))
    log_at = -LRU_C * gate_a * jax.nn.softplus(-lru_lambda.astype(F32))
    a_t = jnp.exp(log_at)
    b_t = jnp.sqrt(-jnp.expm1(2.0 * log_at)) * (gate_x * xc.astype(F32))
    h, h_last = lru_scan(a_t, b_t, h_lru.astype(F32))
    y_lru = h * jax.nn.gelu(gb.astype(F32))
    out = jnp.concatenate([o_gla, y_lru], axis=-1).astype(x.dtype) @ w_out
    return out, s_gla_new.astype(x.dtype), h_last.astype(x.dtype), new_buf.astype(x.dtype)


def wkv7_scan(r, w, k, v, a, b, s0):
    tm = lambda z: jnp.moveaxis(z, 1, 0)

    def step(S, inp):
        r_t, w_t, k_t, v_t, a_t, b_t = inp
        sa = jnp.einsum('bhvk,bhk->bhv', S, a_t)
        S = S * w_t[:, :, None, :] + sa[..., None] * b_t[:, :, None, :] + v_t[..., None] * k_t[:, :, None, :]
        return S, jnp.einsum('bhvk,bhk->bhv', S, r_t)

    s_fin, y = lax.scan(step, s0.astype(F32), (tm(r), tm(w), tm(k), tm(v), tm(a), tm(b)))
    return jnp.moveaxis(y, 0, 1), s_fin


def rwkv7_mixer(x, s0, shift, v_first, vres, mu, w_r, w_k, w_v, w_o, w0, w1, w2, a0, a1, a2,
                g1, g2, k_k, k_a, r_k, lnx_g, lnx_b):
    B, T, D = x.shape
    x_prev = jnp.concatenate([shift[:, None, :].astype(x.dtype), x[:, :-1]], axis=1)
    xx = x_prev - x
    xr, xw, xk, xv, xa, xg = (x + xx * mu[i] for i in range(6))
    r = xr @ w_r
    w = -jax.nn.softplus(-(w0 + jnp.tanh(xw @ w1) @ w2).astype(F32)) - 0.5
    k = xk @ w_k
    v = xv @ w_v
    if vres is None:
        v_first = v
    else:
        v0, v1, v2 = vres
        v = v + (v_first - v) * jax.nn.sigmoid(v0 + (xv @ v1) @ v2)
    a = jax.nn.sigmoid(a0 + (xa @ a1) @ a2)
    g = jax.nn.sigmoid(xg @ g1) @ g2
    hd = lambda z: z.reshape(B, T, RWKV_HEADS, RWKV_HEAD).astype(F32)
    kk = hd(k * k_k)
    kk = kk / jnp.maximum(jnp.sqrt(jnp.sum(jnp.square(kk), -1, keepdims=True)), 1e-12)
    k = k * (1 + (a - 1) * k_a)
    rh, kh, vh, ah = hd(r), hd(k), hd(v), hd(a)
    decay = jnp.exp(-jnp.exp(hd(w)))
    y, s_new = wkv7_scan(rh, decay, kh, vh, -kk, kk * ah, s0)
    mu_y = y.mean(-1, keepdims=True)
    var_y = jnp.mean(jnp.square(y - mu_y), -1, keepdims=True)
    y = ((y - mu_y) * lax.rsqrt(var_y + RWKV_GN_EPS)).reshape(B, T, D) * lnx_g + lnx_b
    bonus = jnp.sum(rh * kh * r_k, -1, keepdims=True) * vh
    y = y + bonus.reshape(B, T, D)
    out = (y * g).astype(x.dtype) @ w_o
    return out, s_new.astype(x.dtype), x[:, -1], v_first


def peer_ffn(x, w_q, keys, u_tab, v_tab):
    B, T, D = x.shape
    n = B * T
    nb = -(-n // PEER_BLOCK)
    xf = jnp.pad(x.reshape(n, D), ((0, nb * PEER_BLOCK - n), (0, 0))).reshape(nb, PEER_BLOCK, D)
    kf = keys.astype(F32)

    def block(xb):
        q = (xb @ w_q).reshape(PEER_BLOCK, PEER_HEADS, PEER_DKEY).astype(F32)
        qm = q.mean(-1, keepdims=True)
        q = (q - qm) * lax.rsqrt(jnp.mean(jnp.square(q - qm), -1, keepdims=True) + LN_EPS)
        s1 = jnp.einsum('thd,hkd->thk', q[..., :PEER_HALF], kf[:, 0])
        s2 = jnp.einsum('thd,hkd->thk', q[..., PEER_HALF:], kf[:, 1])
        v1, i1 = lax.top_k(s1, PEER_TOPK)
        v2, i2 = lax.top_k(s2, PEER_TOPK)
        cand = (v1[..., :, None] + v2[..., None, :]).reshape(PEER_BLOCK, PEER_HEADS, PEER_TOPK * PEER_TOPK)
        cidx = (i1[..., :, None] * PEER_NKEYS + i2[..., None, :]).reshape(PEER_BLOCK, PEER_HEADS, PEER_TOPK * PEER_TOPK)
        sc, pos = lax.top_k(cand, PEER_TOPK)
        eidx = jnp.take_along_axis(cidx, pos, axis=-1)
        gate = jax.nn.softmax(sc, axis=-1)
        act = jax.nn.gelu(jnp.einsum('thkd,td->thk', u_tab[eidx], xb).astype(F32))
        coef = (gate * act).astype(xb.dtype)
        return jnp.einsum('thk,thkd->td', coef, v_tab[eidx])

    y = lax.map(block, xf)
    return y.reshape(nb * PEER_BLOCK, D)[:n].reshape(B, T, D)


def trunk(x, st_gla, st_h, st_conv, st_rwkv, st_shift, W):
    new_gla, new_h, new_conv, new_rwkv, new_shift = [], [], [], [], []
    v_first = None
    for layer in range(DEPTH):
        j = layer // 2
        if layer % 2 == 0:
            m, s1, s2, s3 = gla_lru_mixer(
                x, st_gla[j], st_h[j], st_conv[j], W['ev_w_in'][j], W['ev_gla_w_gate'][j], W['ev_gla_b_gate'][j],
                W['ev_gla_norm'][j], W['ev_conv_w'][j], W['ev_conv_b'][j], W['ev_lru_wa'][j], W['ev_lru_ba'][j],
                W['ev_lru_wx'][j], W['ev_lru_bx'][j], W['ev_lru_lambda'][j], W['ev_w_out'][j])
            new_gla.append(s1)
            new_h.append(s2)
            new_conv.append(s3)
        else:
            vres = None if j == 0 else (W['od_v0'][j - 1], W['od_v1'][j - 1], W['od_v2'][j - 1])
            m, s4, s5, v_first = rwkv7_mixer(
                x, st_rwkv[j], st_shift[j], v_first, vres, W['od_mu'][j], W['od_w_r'][j], W['od_w_k'][j],
                W['od_w_v'][j], W['od_w_o'][j], W['od_w0'][j], W['od_w1'][j], W['od_w2'][j], W['od_a0'][j],
                W['od_a1'][j], W['od_a2'][j], W['od_g1'][j], W['od_g2'][j], W['od_k_k'][j], W['od_k_a'][j],
                W['od_r_k'][j], W['od_lnx_g'][j], W['od_lnx_b'][j])
            new_rwkv.append(s4)
            new_shift.append(s5)
        x = layer_norm(DN_ALPHA * x + m, W['ln_g'][layer, 0], W['ln_b'][layer, 0])
        f = peer_ffn(x, W['peer_w_q'][layer], W['peer_keys'][layer], W['peer_u'][layer], W['peer_v'][layer])
        x = layer_norm(DN_ALPHA * x + f, W['ln_g'][layer, 1], W['ln_b'][layer, 1])
    return x, jnp.stack(new_gla), jnp.stack(new_h), jnp.stack(new_conv), jnp.stack(new_rwkv), jnp.stack(new_shift)


def setup_inputs(seed: int = 0) -> dict:
    key = jax.random.key(seed)
    ks = iter(jax.random.split(key, 64))
    nrm = lambda shape, scale: jax.random.normal(next(ks), shape, F32) * scale
    D = D_MODEL
    NP = N_PAIRS
    a_init = jax.random.uniform(next(ks), (NP, LRU_WIDTH), F32, minval=0.9, maxval=0.999)
    s_init = a_init ** (1.0 / LRU_C)
    lam = jnp.log(s_init) - jnp.log1p(-s_init)
    ramp = (jnp.arange(D, dtype=F32) / (D - 1)) ** 0.9
    return {
        'x_prompt': nrm((BATCH, SEQ, D), 1.0),
        'x_sample': nrm((DEC_BATCH, DEC_SEQ, D), 1.0),
        'state_gla': nrm((NP, DEC_BATCH, GLA_HEADS, GLA_DK, GLA_DV), 0.1),
        'state_lru_h': nrm((NP, DEC_BATCH, LRU_WIDTH), 0.5),
        'state_lru_conv': nrm((NP, DEC_BATCH, CONV_W - 1, LRU_WIDTH), 1.0),
        'state_rwkv': nrm((NP, DEC_BATCH, RWKV_HEADS, RWKV_HEAD, RWKV_HEAD), 0.1),
        'state_rwkv_shift': nrm((NP, DEC_BATCH, D), 1.0),
        'meta_tokens': nrm((N_META, D), 1.0),
        'ln_g': 1.0 + nrm((DEPTH, 2, D), 0.01),
        'ln_b': nrm((DEPTH, 2, D), 0.01),
        'ev_w_in': nrm((NP, D, MIX_IN), D ** -0.5),
        'ev_gla_w_gate': nrm((NP, GLA_RANK, GLA_KEY), GLA_RANK ** -0.5),
        'ev_gla_b_gate': nrm((NP, GLA_KEY), 0.1),
        'ev_gla_norm': 1.0 + nrm((NP, GLA_DV), 0.01),
        'ev_conv_w': nrm((NP, CONV_W, LRU_WIDTH), CONV_W ** -0.5),
        'ev_conv_b': nrm((NP, LRU_WIDTH), 0.01),
        'ev_lru_wa': nrm((NP, LRU_BLOCKS, LRU_BDIM, LRU_BDIM), LRU_BDIM ** -0.5),
        'ev_lru_ba': nrm((NP, LRU_WIDTH), 0.1),
        'ev_lru_wx': nrm((NP, LRU_BLOCKS, LRU_BDIM, LRU_BDIM), LRU_BDIM ** -0.5),
        'ev_lru_bx': nrm((NP, LRU_WIDTH), 0.1),
        'ev_lru_lambda': lam,
        'ev_w_out': nrm((NP, MIX_OUT, D), DN_BETA * MIX_OUT ** -0.5),
        'od_mu': jax.random.uniform(next(ks), (NP, 6, D), F32),
        'od_w_r': nrm((NP, D, D), D ** -0.5),
        'od_w_k': nrm((NP, D, D), D ** -0.5),
        'od_w_v': nrm((NP, D, D), DN_BETA * D ** -0.5),
        'od_w_o': nrm((NP, D, D), DN_BETA * D ** -0.5),
        'od_w0': -6.0 + 5.0 * ramp + nrm((NP, D), 0.1),
        'od_w1': nrm((NP, D, DECAY_LORA), D ** -0.5),
        'od_w2': nrm((NP, DECAY_LORA, D), 0.1 * DECAY_LORA ** -0.5),
        'od_a0': nrm((NP, D), 0.1),
        'od_a1': nrm((NP, D, AAA_LORA), D ** -0.5),
        'od_a2': nrm((NP, AAA_LORA, D), AAA_LORA ** -0.5),
        'od_v0': 1.0 + nrm((NP - 1, D), 0.1),
        'od_v1': nrm((NP - 1, D, MV_LORA), D ** -0.5),
        'od_v2': nrm((NP - 1, MV_LORA, D), MV_LORA ** -0.5),
        'od_g1': nrm((NP, D, GATE_LORA), D ** -0.5),
        'od_g2': nrm((NP, GATE_LORA, D), GATE_LORA ** -0.5),
        'od_k_k': 0.85 + nrm((NP, D), 0.02),
        'od_k_a': 1.0 + nrm((NP, D), 0.02),
        'od_r_k': nrm((NP, RWKV_HEADS, RWKV_HEAD), 0.1),
        'od_lnx_g': 1.0 + nrm((NP, D), 0.01),
        'od_lnx_b': nrm((NP, D), 0.01),
        'peer_w_q': nrm((DEPTH, D, PEER_HEADS * PEER_DKEY), D ** -0.5),
        'peer_keys': nrm((DEPTH, PEER_HEADS, 2, PEER_NKEYS, PEER_HALF), PEER_HALF ** -0.5),
        'peer_u': nrm((DEPTH, PEER_EXPERTS, D), D ** -0.5),
        'peer_v': nrm((DEPTH, PEER_EXPERTS, D), DN_BETA * PEER_HEADS ** -0.5),
    }


def reference(x_prompt, x_sample, state_gla, state_lru_h, state_lru_conv, state_rwkv, state_rwkv_shift,
              meta_tokens, ln_g, ln_b, ev_w_in, ev_gla_w_gate, ev_gla_b_gate, ev_gla_norm, ev_conv_w, ev_conv_b,
              ev_lru_wa, ev_lru_ba, ev_lru_wx, ev_lru_bx, ev_lru_lambda, ev_w_out, od_mu, od_w_r, od_w_k, od_w_v,
              od_w_o, od_w0, od_w1, od_w2, od_a0, od_a1, od_a2, od_v0, od_v1, od_v2, od_g1, od_g2, od_k_k, od_k_a,
              od_r_k, od_lnx_g, od_lnx_b, peer_w_q, peer_keys, peer_u, peer_v):
    W = dict(ln_g=ln_g, ln_b=ln_b, ev_w_in=ev_w_in, ev_gla_w_gate=ev_gla_w_gate, ev_gla_b_gate=ev_gla_b_gate,
             ev_gla_norm=ev_gla_norm, ev_conv_w=ev_conv_w, ev_conv_b=ev_conv_b, ev_lru_wa=ev_lru_wa,
             ev_lru_ba=ev_lru_ba, ev_lru_wx=ev_lru_wx, ev_lru_bx=ev_lru_bx, ev_lru_lambda=ev_lru_lambda,
             ev_w_out=ev_w_out, od_mu=od_mu, od_w_r=od_w_r, od_w_k=od_w_k, od_w_v=od_w_v, od_w_o=od_w_o,
             od_w0=od_w0, od_w1=od_w1, od_w2=od_w2, od_a0=od_a0, od_a1=od_a1, od_a2=od_a2, od_v0=od_v0,
             od_v1=od_v1, od_v2=od_v2, od_g1=od_g1, od_g2=od_g2, od_k_k=od_k_k, od_k_a=od_k_a, od_r_k=od_r_k,
             od_lnx_g=od_lnx_g, od_lnx_b=od_lnx_b, peer_w_q=peer_w_q, peer_keys=peer_keys, peer_u=peer_u,
             peer_v=peer_v)
    Bp = x_prompt.shape[0]
    dt = x_prompt.dtype
    xp = jnp.concatenate([jnp.broadcast_to(meta_tokens.astype(dt)[None], (Bp, N_META, D_MODEL)), x_prompt], axis=1)
    z_gla = jnp.zeros((N_PAIRS, Bp, GLA_HEADS, GLA_DK, GLA_DV), dt)
    z_h = jnp.zeros((N_PAIRS, Bp, LRU_WIDTH), dt)
    z_conv = jnp.zeros((N_PAIRS, Bp, CONV_W - 1, LRU_WIDTH), dt)
    z_rwkv = jnp.zeros((N_PAIRS, Bp, RWKV_HEADS, RWKV_HEAD, RWKV_HEAD), dt)
    z_shift = jnp.zeros((N_PAIRS, Bp, D_MODEL), dt)
    yp, p_gla, p_h, p_conv, p_rwkv, p_shift = trunk(xp, z_gla, z_h, z_conv, z_rwkv, z_shift, W)
    y_prompt = yp[:, N_META:]
    y_sample, s_gla, s_h, s_conv, s_rwkv, s_shift = trunk(
        x_sample, state_gla, state_lru_h, state_lru_conv, state_rwkv, state_rwkv_shift, W)
    return (y_prompt, y_sample, p_gla, p_h, p_conv, p_rwkv, p_shift, s_gla, s_h, s_conv, s_rwkv, s_shift)
```

```python
import functools

import jax
import jax.numpy as jnp
from jax import lax
from jax.experimental import pallas as pl
from jax.experimental.pallas import tpu as pltpu

F32 = jnp.float32
BF16 = jnp.bfloat16

D_MODEL = 1024
DEPTH = 4
N_META = 16
GLA_HEADS = 4
GLA_DK = 64
GLA_DV = 128
GLA_RANK = 16
GLA_GATE_NORM = 16.0
GLA_CHUNK = 16
LRU_WIDTH = 512
LRU_BLOCKS = 8
CONV_W = 4
LRU_C = 8.0
RWKV_HEAD = 64
RWKV_HEADS = D_MODEL // RWKV_HEAD
RWKV_GN_EPS = 64e-5
PEER_HEADS = 8
PEER_NKEYS = 128
PEER_DKEY = 256
PEER_HALF = PEER_DKEY // 2
PEER_TOPK = 16
DN_ALPHA = float((2 * DEPTH) ** 0.25)
LN_EPS = 1e-5

LANES = 128
SUBLANES = 8
VMEM_LIMIT = 56 * 1024 * 1024


def _dot(a, b):
    return jnp.dot(a.astype(BF16), b.astype(BF16), preferred_element_type=F32)


def _dot_nt(a, b):
    return lax.dot_general(a.astype(BF16), b.astype(BF16), (((1,), (1,)), ((), ())), preferred_element_type=F32)


def _dot_tn(a, b):
    return lax.dot_general(a.astype(BF16), b.astype(BF16), (((0,), (0,)), ((), ())), preferred_element_type=F32)


def _layer_norm_rows(z, g, b):
    mu = jnp.mean(z, axis=-1, keepdims=True)
    d = z - mu
    var = jnp.mean(d * d, axis=-1, keepdims=True)
    return d * lax.rsqrt(var + LN_EPS) * g + b


def _sigmoid(x):
    return 1.0 / (1.0 + jnp.exp(-x))


def _softplus(x):
    return jnp.maximum(x, 0.0) + jnp.log1p(jnp.exp(-jnp.abs(x)))


def _gelu_tanh(x):
    return 0.5 * x * (1.0 + jnp.tanh(0.7978845608028654 * (x + 0.044715 * (x * x * x))))


def _silu(x):
    return x * _sigmoid(x)


def _params(*sem):
    return pltpu.CompilerParams(dimension_semantics=sem, vmem_limit_bytes=VMEM_LIMIT)


def _const_spec(shape):
    nd = len(shape)
    return pl.BlockSpec(shape, lambda *_: (0,) * nd)


PEER_CAND_ROWS = 80


def _peer_cand_index():
    rows = [r1 * PEER_TOPK for r1 in range(16)]
    for j in range(1, 8):
        rows += [r1 * PEER_TOPK + j for r1 in range(8)]
    rows += list(range(8, 16))
    return jnp.broadcast_to(jnp.asarray(rows, F32)[:, None], (PEER_CAND_ROWS, LANES))


def _top16_ranked(s, key_iota):
    rank = jnp.full(s.shape, float(PEER_TOPK), F32)
    row16 = lax.broadcasted_iota(jnp.int32, (PEER_TOPK, s.shape[1]), 0)
    vals = jnp.zeros((PEER_TOPK, s.shape[1]), F32)
    for r in range(PEER_TOPK):
        m = jnp.max(s, axis=0, keepdims=True)
        first = jnp.min(jnp.where(s == m, key_iota, float(PEER_NKEYS)), axis=0, keepdims=True)
        hit = key_iota == first
        rank = jnp.where(hit, float(r), rank)
        s = jnp.where(hit, -jnp.inf, s)
        vals = jnp.where(row16 == r, m, vals)
    return vals, rank


def _peer_select(a, b, cidx):
    blocks = [a + b[0:1]]
    for j in range(1, 8):
        blocks.append(a[0:8] + b[j:j + 1])
    blocks.append(a[0:1] + b[8:16])
    c = jnp.concatenate(blocks, axis=0)
    e = jnp.exp(c - c[0:1])
    sel = jnp.zeros(c.shape, F32)
    cw = c
    for _ in range(PEER_TOPK):
        m = jnp.max(cw, axis=0, keepdims=True)
        first = jnp.min(jnp.where(cw == m, cidx, 1e9), axis=0, keepdims=True)
        hit = cidx == first
        sel = jnp.where(hit, 1.0, sel)
        cw = jnp.where(hit, -jnp.inf, cw)
    z = jnp.sum(sel * e, axis=0, keepdims=True)
    j_lo = sel[0:8]
    for j in range(1, 8):
        j_lo = j_lo + sel[8 + 8 * j:16 + 8 * j]
    extra = jnp.sum(sel[72:80], axis=0, keepdims=True)
    row8 = lax.broadcasted_iota(jnp.int32, j_lo.shape, 0)
    j_lo = j_lo + jnp.where(row8 == 0, extra, 0.0)
    return jnp.concatenate([j_lo, sel[8:16]], axis=0), z


def _peer_kernel(x_ref, wqt_ref, k1_ref, k2_ref, cidx_ref, u_ref, vt_ref, g_ref, b_ref, o_ref,
                 xb_ref, acc_ref, h1_ref, c1_ref, r2_ref, e2_ref, s_ref, hd_ref, coef_ref, *, tn, te, lc):
    j = pl.program_id(1)
    nj = pl.num_programs(1)

    @pl.when(j == 0)
    def _select():
        xb_ref[...] = x_ref[...].astype(BF16)
        acc_ref[...] = jnp.zeros_like(acc_ref)
        key_iota = lax.broadcasted_iota(jnp.int32, (PEER_NKEYS, LANES), 0).astype(F32)

        def head(h, carry):
            q = _dot_nt(wqt_ref[pl.ds(pl.multiple_of(h * PEER_DKEY, PEER_DKEY), PEER_DKEY), :], xb_ref[...])
            mu = jnp.mean(q, axis=0, keepdims=True)
            d = q - mu
            qn = d * lax.rsqrt(jnp.mean(d * d, axis=0, keepdims=True) + LN_EPS)
            s_ref[0] = _dot(k1_ref[h], qn[0:PEER_HALF])
            s_ref[1] = _dot(k2_ref[h], qn[PEER_HALF:PEER_DKEY])

            def chunk(ci, carry2):
                ls = pl.ds(pl.multiple_of(ci * LANES, LANES), LANES)
                s1 = s_ref[0, :, ls]
                s2 = s_ref[1, :, ls]
                a, r1 = _top16_ranked(s1, key_iota)
                b, r2 = _top16_ranked(s2, key_iota)
                jt, z = _peer_select(a, b, cidx_ref[...])
                h1 = jnp.zeros_like(r1)
                for r in range(PEER_TOPK):
                    h1 = jnp.where(r1 == float(r), jt[r:r + 1], h1)
                h1_ref[h, :, ls] = h1
                r2_ref[h, :, ls] = r2
                c1_ref[h, :, ls] = jnp.exp(s1 - a[0:1]) / z
                e2_ref[h, :, ls] = jnp.exp(s2 - b[0:1])
                return carry2

            lax.fori_loop(0, tn // LANES, chunk, 0)
            return carry

        lax.fori_loop(0, PEER_HEADS, head, 0)

    hd_ref[...] = _dot_nt(u_ref[...], xb_ref[...])
    for ii in range(te // PEER_NKEYS):
        i1 = j * (te // PEER_NKEYS) + ii
        rows = slice(ii * PEER_NKEYS, (ii + 1) * PEER_NKEYS)
        for c in range(tn // lc):
            ls = slice(c * lc, (c + 1) * lc)
            gate = jnp.zeros((PEER_NKEYS, lc), F32)
            for h in range(PEER_HEADS):
                hrow = h1_ref[h, pl.ds(i1, 1), ls]
                crow = c1_ref[h, pl.ds(i1, 1), ls]
                gate = gate + jnp.where(r2_ref[h, :, ls] < hrow, e2_ref[h, :, ls] * crow, 0.0)
            coef_ref[rows, ls] = (gate * _gelu_tanh(hd_ref[rows, ls])).astype(BF16)
    acc_ref[...] += jnp.dot(vt_ref[...], coef_ref[...], preferred_element_type=F32)

    @pl.when(j == nj - 1)
    def _finish():
        y = acc_ref[...].T
        o_ref[...] = _layer_norm_rows(DN_ALPHA * x_ref[...] + y, g_ref[...], b_ref[...])


def _peer_layer(x, wqt, k1, k2, u, vt, g, b, *, tn=1024, te=256, lc=256):
    n, d = x.shape
    ne = u.shape[0]
    kern = functools.partial(_peer_kernel, tn=tn, te=te, lc=lc)
    one = pl.Buffered(1)
    return pl.pallas_call(
        kern,
        out_shape=jax.ShapeDtypeStruct((n, d), F32),
        grid=(n // tn, ne // te),
        in_specs=[
            pl.BlockSpec((tn, d), lambda i, j: (i, 0), pipeline_mode=one),
            pl.BlockSpec(wqt.shape, lambda i, j: (0, 0), pipeline_mode=one),
            pl.BlockSpec(k1.shape, lambda i, j: (0, 0, 0), pipeline_mode=one),
            pl.BlockSpec(k2.shape, lambda i, j: (0, 0, 0), pipeline_mode=one),
            pl.BlockSpec((PEER_CAND_ROWS, LANES), lambda i, j: (0, 0), pipeline_mode=one),
            pl.BlockSpec((te, d), lambda i, j: (j, 0)),
            pl.BlockSpec((d, te), lambda i, j: (0, j)),
            pl.BlockSpec((1, d), lambda i, j: (0, 0), pipeline_mode=one),
            pl.BlockSpec((1, d), lambda i, j: (0, 0), pipeline_mode=one),
        ],
        out_specs=pl.BlockSpec((tn, d), lambda i, j: (i, 0), pipeline_mode=one),
        scratch_shapes=[
            pltpu.VMEM((tn, d), BF16),
            pltpu.VMEM((d, tn), F32),
            pltpu.VMEM((PEER_HEADS, PEER_NKEYS, tn), F32),
            pltpu.VMEM((PEER_HEADS, PEER_NKEYS, tn), F32),
            pltpu.VMEM((PEER_HEADS, PEER_NKEYS, tn), F32),
            pltpu.VMEM((PEER_HEADS, PEER_NKEYS, tn), F32),
            pltpu.VMEM((2, PEER_NKEYS, tn), F32),
            pltpu.VMEM((te, tn), F32),
            pltpu.VMEM((te, tn), BF16),
        ],
        compiler_params=_params("parallel", "arbitrary"),
        name="peer",
    )(x, wqt, k1, k2, _peer_cand_index(), u, vt, g, b)


GLA_PAD = GLA_HEADS * LANES


def _even_in_kernel(x_ref, wq_ref, wk_ref, wv_ref, wr_ref, wxb_ref, wgb_ref, wlr_ref, wg_ref, bg_ref,
                    qkl_ref, v_ref, rs_ref, xb_ref, gg_ref):
    xb16 = x_ref[...].astype(BF16)
    qkl_ref[:, 0:GLA_PAD] = _dot(xb16, wq_ref[...]) * (GLA_DK ** -0.5)
    qkl_ref[:, GLA_PAD:2 * GLA_PAD] = _dot(xb16, wk_ref[...])
    glr = _dot(xb16, wlr_ref[...])
    z = _dot(glr, wg_ref[...]) + bg_ref[...]
    qkl_ref[:, 2 * GLA_PAD:3 * GLA_PAD] = -_softplus(-z) * (1.0 / GLA_GATE_NORM)
    v_ref[...] = _dot(xb16, wv_ref[...])
    rs_ref[...] = _silu(_dot(xb16, wr_ref[...]))
    xb_ref[...] = _dot(xb16, wxb_ref[...])
    gg_ref[...] = _gelu_tanh(_dot(xb16, wgb_ref[...]))


def _even_in(x, consts, *, tm=512):
    n, d = x.shape
    row = lambda w: pl.BlockSpec((tm, w), lambda i: (i, 0))
    return pl.pallas_call(
        _even_in_kernel,
        out_shape=[jax.ShapeDtypeStruct((n, 3 * GLA_PAD), F32)] + [jax.ShapeDtypeStruct((n, LRU_WIDTH), F32)] * 4,
        grid=(n // tm,),
        in_specs=[row(d)] + [_const_spec(c.shape) for c in consts],
        out_specs=[row(3 * GLA_PAD)] + [row(LRU_WIDTH)] * 4,
        compiler_params=_params("parallel"),
        name="even_in",
    )(x, *consts)


def _gla_kernel(qkl_ref, v_ref, rs_ref, s0_ref, tri_ref, gn_ref, o_ref, sout_ref, st_ref, *, tb):
    t = pl.program_id(1)

    @pl.when(t == 0)
    def _init():
        st_ref[...] = s0_ref[0]

    tri = tri_ref[...]
    causal = tri > 0
    gn = gn_ref[...]

    def chunk(c, carry):
        rows = pl.ds(pl.multiple_of(c * GLA_CHUNK, GLA_CHUNK), GLA_CHUNK)
        q = qkl_ref[rows, 0:GLA_PAD]
        k = qkl_ref[rows, GLA_PAD:2 * GLA_PAD]
        la = qkl_ref[rows, 2 * GLA_PAD:3 * GLA_PAD]
        v = v_ref[rows, :]
        rs = rs_ref[rows, :]
        hi = la.astype(BF16)
        r1 = la - hi.astype(F32)
        mid = r1.astype(BF16)
        lo = (r1 - mid.astype(F32)).astype(BF16)
        cum = (jnp.dot(tri, hi, preferred_element_type=F32) + jnp.dot(tri, mid, preferred_element_type=F32)
               + jnp.dot(tri, lo, preferred_element_type=F32))
        last = cum[GLA_CHUNK - 1:GLA_CHUNK]
        qe = q * jnp.exp(cum)
        ke = k * jnp.exp(-cum)
        kl = k * jnp.exp(last - cum)
        dec = jnp.exp(last)
        for h in range(GLA_HEADS):
            sl = slice(h * LANES, (h + 1) * LANES)
            att = jnp.where(causal, _dot_nt(qe[:, sl], ke[:, sl]), 0.0)
            st = st_ref[h]
            o = _dot(att, v[:, sl]) + _dot_nt(qe[:, sl], st)
            st_ref[h] = st * dec[:, sl] + _dot_tn(v[:, sl], kl[:, sl])
            on = o * lax.rsqrt(jnp.mean(o * o, axis=-1, keepdims=True) + LN_EPS) * gn
            o_ref[rows, sl] = on * rs[:, sl]
        return carry

    lax.fori_loop(0, tb // GLA_CHUNK, chunk, 0)

    @pl.when(t == pl.num_programs(1) - 1)
    def _fin():
        sout_ref[0] = st_ref[...]


def _gla(qkl, v, rs, s0t, gn, *, nb, t, tb):
    nt = t // tb
    tri = jnp.tril(jnp.ones((GLA_CHUNK, GLA_CHUNK), BF16))
    row = lambda w: pl.BlockSpec((tb, w), lambda b, i: (b * nt + i, 0))
    st_spec = pl.BlockSpec((1, GLA_HEADS, GLA_DV, LANES), lambda b, i: (b, 0, 0, 0))
    return pl.pallas_call(
        functools.partial(_gla_kernel, tb=tb),
        out_shape=[jax.ShapeDtypeStruct((nb * t, GLA_HEADS * GLA_DV), F32), jax.ShapeDtypeStruct(s0t.shape, F32)],
        grid=(nb, nt),
        in_specs=[row(3 * GLA_PAD), row(GLA_HEADS * GLA_DV), row(GLA_HEADS * GLA_DV), st_spec,
                  _const_spec(tri.shape), _const_spec(gn.shape)],
        out_specs=[row(GLA_HEADS * GLA_DV), st_spec],
        scratch_shapes=[pltpu.VMEM((GLA_HEADS, GLA_DV, LANES), F32)],
        compiler_params=_params("parallel", "arbitrary"),
        name="gla",
    )(qkl, v, rs, s0t, tri, gn)


def _lru_kernel(xb_ref, gg_ref, buf0_ref, h0_ref, cw_ref, cb_ref, wa_ref, ba_ref, wx_ref, bx_ref, lam_ref,
                y_ref, hlast_ref, cbuf_ref, xs_ref, a_ref, b_ref, hs_ref, hcar_ref, *, tb, t_last):
    t = pl.program_id(1)
    halo = SUBLANES

    @pl.when(t == 0)
    def _init():
        xs_ref[0:halo] = buf0_ref[0]
        hcar_ref[...] = h0_ref[0]

    xs_ref[halo:halo + tb] = xb_ref[...]
    xc = cb_ref[...] + sum(xs_ref[pl.ds(halo - (CONV_W - 1) + i, tb), :] * cw_ref[i:i + 1, :] for i in range(CONV_W))
    ga = _sigmoid(_dot(xc, wa_ref[...]) + ba_ref[...])
    gx = _sigmoid(_dot(xc, wx_ref[...]) + bx_ref[...])
    log_at = ga * (-LRU_C * _softplus(-lam_ref[...]))
    a = jnp.exp(log_at)
    a_ref[...] = a
    b_ref[...] = jnp.sqrt(-jnp.tanh(log_at) * (a * a + 1.0)) * (gx * xc)

    def step(i, h):
        h = a_ref[pl.ds(i, 1), :] * h + b_ref[pl.ds(i, 1), :]
        hs_ref[pl.ds(i, 1), :] = h
        return h

    hcar_ref[...] = lax.fori_loop(0, tb, step, hcar_ref[...], unroll=8)
    y_ref[...] = hs_ref[...] * gg_ref[...]

    @pl.when(t == pl.num_programs(1) - 1)
    def _fin():
        hlast_ref[0] = hs_ref[t_last:t_last + 1, :]
        cbuf_ref[0] = xs_ref[halo + t_last - (CONV_W - 2):halo + t_last + 1, :]

    xs_ref[0:halo] = xs_ref[tb:tb + halo]


def _lru(xb, gg, buf0, h0, consts, *, nb, t, tb, t_last):
    nt = t // tb
    w = LRU_WIDTH
    row = pl.BlockSpec((tb, w), lambda b, i: (b * nt + i, 0))
    per_b = lambda r: pl.BlockSpec((1, r, w), lambda b, i: (b, 0, 0))
    return pl.pallas_call(
        functools.partial(_lru_kernel, tb=tb, t_last=t_last),
        out_shape=[jax.ShapeDtypeStruct((nb * t, w), F32), jax.ShapeDtypeStruct((nb, 1, w), F32),
                   jax.ShapeDtypeStruct((nb, CONV_W - 1, w), F32)],
        grid=(nb, nt),
        in_specs=[row, row, per_b(SUBLANES), per_b(1)] + [_const_spec(c.shape) for c in consts],
        out_specs=[row, per_b(1), per_b(CONV_W - 1)],
        scratch_shapes=[pltpu.VMEM((tb + 2 * SUBLANES, w), F32), pltpu.VMEM((tb, w), F32), pltpu.VMEM((tb, w), F32),
                        pltpu.VMEM((tb, w), F32), pltpu.VMEM((1, w), F32)],
        compiler_params=_params("parallel", "arbitrary"),
        name="lru",
    )(xb, gg, buf0, h0, *consts)


def _mix_out_kernel(a_ref, b_ref, x_ref, wa_ref, wb_ref, g_ref, beta_ref, o_ref):
    m = _dot(a_ref[...], wa_ref[...]) + _dot(b_ref[...], wb_ref[...])
    o_ref[...] = _layer_norm_rows(DN_ALPHA * x_ref[...] + m, g_ref[...], beta_ref[...])


def _mix_out(a, b, x, wa, wb, g, beta, *, tm=512):
    n, d = x.shape
    row = lambda w: pl.BlockSpec((tm, w), lambda i: (i, 0))
    consts = [wa, wb, g, beta]
    return pl.pallas_call(
        _mix_out_kernel,
        out_shape=jax.ShapeDtypeStruct((n, d), F32),
        grid=(n // tm,),
        in_specs=[row(a.shape[1]), row(b.shape[1]), row(d)] + [_const_spec(c.shape) for c in consts],
        out_specs=row(d),
        compiler_params=_params("parallel"),
        name="mix_out",
    )(a, b, x, *consts)


def _pad_heads(w, heads, width):
    lead = w.shape[:-1]
    w = w.reshape(lead + (heads, width))
    w = jnp.pad(w, [(0, 0)] * len(lead) + [(0, 0), (0, LANES - width)])
    return w.reshape(lead + (heads * LANES,))


def _even_weights(w_in, w_gate, b_gate, gla_norm, conv_w, conv_b, lru_wa, lru_ba, lru_wx, lru_bx, lru_lambda, w_out):
    gk = GLA_HEADS * GLA_DK
    gv = GLA_HEADS * GLA_DV
    o = 0
    cols = {}
    for name, width in (("q", gk), ("k", gk), ("v", gv), ("lr", GLA_RANK), ("r", gv), ("xb", LRU_WIDTH), ("gb", LRU_WIDTH)):
        cols[name] = w_in[:, o:o + width]
        o += width
    wlr = jnp.pad(cols["lr"], ((0, 0), (0, LANES - GLA_RANK)))
    wg = jnp.pad(_pad_heads(w_gate, GLA_HEADS, GLA_DK), ((0, LANES - GLA_RANK), (0, 0)))
    bg = _pad_heads(b_gate[None], GLA_HEADS, GLA_DK)
    in_consts = [_pad_heads(cols["q"], GLA_HEADS, GLA_DK).astype(BF16), _pad_heads(cols["k"], GLA_HEADS, GLA_DK).astype(BF16),
                 cols["v"].astype(BF16), cols["r"].astype(BF16), cols["xb"].astype(BF16), cols["gb"].astype(BF16),
                 wlr.astype(BF16), wg.astype(BF16), bg]
    eye = jnp.eye(LRU_BLOCKS, dtype=F32)
    bd = lambda w: (eye[:, None, :, None] * w[:, :, None, :]).reshape(LRU_WIDTH, LRU_WIDTH).astype(BF16)
    lru_consts = [conv_w, conv_b[None], bd(lru_wa), lru_ba[None], bd(lru_wx), lru_bx[None], lru_lambda[None]]
    return in_consts, gla_norm[None], lru_consts, w_out[:gv].astype(BF16), w_out[gv:].astype(BF16)


def _seq_rows(a, off, nb, t, t_pad):
    a = a[off:off + nb * t].reshape(nb, t, a.shape[1])
    return jnp.pad(a, ((0, 0), (0, t_pad - t), (0, 0))).reshape(nb * t_pad, a.shape[2])


def _unpad_rows(a, nb, t, t_pad):
    return a.reshape(nb, t_pad, a.shape[1])[:, :t].reshape(nb * t, a.shape[1])


def _even_layer(x, groups, weights, g, beta, *, tm):
    in_consts, gn, lru_consts, wo_a, wo_b = weights
    qkl, v, rs, xb, gg = _even_in(x, in_consts, tm=tm)
    ogs, yls, states = [], [], []
    for off, nb, t, tb, s_gla, h_lru, cbuf in groups:
        s0t = jnp.pad(jnp.swapaxes(s_gla, 2, 3), ((0, 0), (0, 0), (0, 0), (0, LANES - GLA_DK)))
        buf0 = jnp.pad(cbuf, ((0, 0), (SUBLANES - (CONV_W - 1), 0), (0, 0)))
        if tb is None:
            tg = GLA_CHUNK
            tl = SUBLANES
            og, st = _gla(_seq_rows(qkl, off, nb, t, tg), _seq_rows(v, off, nb, t, tg), _seq_rows(rs, off, nb, t, tg),
                          s0t, gn, nb=nb, t=tg, tb=tg)
            yl, hl, cb = _lru(_seq_rows(xb, off, nb, t, tl), _seq_rows(gg, off, nb, t, tl), buf0, h_lru[:, None],
                              lru_consts, nb=nb, t=tl, tb=tl, t_last=t - 1)
            og = _unpad_rows(og, nb, t, tg)
            yl = _unpad_rows(yl, nb, t, tl)
        else:
            og, st = _gla(qkl, v, rs, s0t, gn, nb=nb, t=t, tb=tb)
            yl, hl, cb = _lru(xb, gg, buf0, h_lru[:, None], lru_consts, nb=nb, t=t, tb=tb, t_last=tb - 1)
        ogs.append(og)
        yls.append(yl)
        states.append((jnp.swapaxes(st[..., :GLA_DK], 2, 3), hl[:, 0], cb))
    n_used = sum(o.shape[0] for o in ogs)
    tail = jnp.zeros((x.shape[0] - n_used, LRU_WIDTH), F32)
    og = jnp.concatenate(ogs + [tail], axis=0)
    yl = jnp.concatenate(yls + [tail], axis=0)
    return _mix_out(og, yl, x, wo_a, wo_b, g, beta, tm=tm), states


RWKV_STREAMS = 5


def _rwkv_in_kernel(*refs, has_vres):
    if has_vres:
        (x_ref, xp_ref, mu_ref, wr_ref, wk_ref, wv_ref, w1_ref, w2_ref, a1_ref, a2_ref, g1_ref, g2_ref, w0_ref, a0_ref,
         vf_ref, v1_ref, v2_ref, v0_ref, p_ref, g_ref) = refs
    else:
        (x_ref, xp_ref, mu_ref, wr_ref, wk_ref, wv_ref, w1_ref, w2_ref, a1_ref, a2_ref, g1_ref, g2_ref, w0_ref, a0_ref,
         p_ref, g_ref) = refs
    d = D_MODEL
    x = x_ref[...]
    xx = xp_ref[...] - x
    mix = lambda i: (x + xx * mu_ref[i:i + 1, :]).astype(BF16)
    xr, xw, xk, xv, xa, xg = (mix(i) for i in range(6))
    p_ref[:, 0:d] = _dot(xr, wr_ref[...])
    p_ref[:, d:2 * d] = _dot(xk, wk_ref[...])
    v = _dot(xv, wv_ref[...])
    if has_vres:
        v = v + (vf_ref[...] - v) * _sigmoid(v0_ref[...] + _dot(_dot(xv, v1_ref[...]), v2_ref[...]))
    p_ref[:, 2 * d:3 * d] = v
    p_ref[:, 3 * d:4 * d] = w0_ref[...] + _dot(jnp.tanh(_dot(xw, w1_ref[...])), w2_ref[...])
    p_ref[:, 4 * d:5 * d] = a0_ref[...] + _dot(_dot(xa, a1_ref[...]), a2_ref[...])
    g_ref[...] = _dot(_sigmoid(_dot(xg, g1_ref[...])), g2_ref[...])


def _rwkv_in(x, xprev, consts, vres, *, tm):
    n, d = x.shape
    row = lambda w: pl.BlockSpec((tm, w), lambda i: (i, 0))
    args = [x, xprev] + list(consts)
    specs = [row(d), row(d)] + [_const_spec(c.shape) for c in consts]
    if vres is not None:
        vf, vconsts = vres
        args += [vf] + list(vconsts)
        specs += [row(d)] + [_const_spec(c.shape) for c in vconsts]
    return pl.pallas_call(
        functools.partial(_rwkv_in_kernel, has_vres=vres is not None),
        out_shape=[jax.ShapeDtypeStruct((n, RWKV_STREAMS * d), F32), jax.ShapeDtypeStruct((n, d), F32)],
        grid=(n // tm,),
        in_specs=specs,
        out_specs=[row(RWKV_STREAMS * d), row(d)],
        compiler_params=_params("parallel"),
        name="rwkv_in",
    )(*args)


def _wkv_kernel(p_ref, s0_ref, kk_ref, ka_ref, rk_ref, lg_ref, lb_ref, y_ref, sout_ref,
                s_ref, dec_ref, a_ref, b_ref, km_ref, bon_ref, *, tt):
    t = pl.program_id(1)
    hk = RWKV_HEAD

    @pl.when(t == 0)
    def _init():
        s_ref[...] = s0_ref[...]

    r = p_ref[:, 0]
    k = p_ref[:, 1]
    v = p_ref[:, 2]
    dec_ref[...] = jnp.exp(-jnp.exp(-_softplus(-p_ref[:, 3]) - 0.5))
    ag = _sigmoid(p_ref[:, 4])
    kk = k * kk_ref[...][None]
    kk = kk / jnp.maximum(jnp.sqrt(jnp.sum(kk * kk, axis=1, keepdims=True)), 1e-12)
    km = k * (1.0 + (ag - 1.0) * ka_ref[...][None])
    km_ref[...] = km
    a_ref[...] = -kk
    b_ref[...] = kk * ag
    bon_ref[...] = jnp.sum(r * km * rk_ref[...][None], axis=1, keepdims=True) * v
    lg = lg_ref[...]
    lb = lb_ref[...]

    def step(i, carry):
        sa = jnp.zeros((hk, LANES), F32)
        for q in range(hk):
            sa = sa + s_ref[q] * a_ref[i, q:q + 1, :]
        vv = p_ref[i, 2]
        y = jnp.zeros((hk, LANES), F32)
        for q in range(hk):
            sn = s_ref[q] * dec_ref[i, q:q + 1, :] + sa * b_ref[i, q:q + 1, :] + vv * km_ref[i, q:q + 1, :]
            s_ref[q] = sn
            y = y + sn * p_ref[i, 0, q:q + 1, :]
        mu = jnp.mean(y, axis=0, keepdims=True)
        dlt = y - mu
        var = jnp.mean(dlt * dlt, axis=0, keepdims=True)
        y_ref[i] = dlt * lax.rsqrt(var + RWKV_GN_EPS) * lg + lb + bon_ref[i]
        return carry

    lax.fori_loop(0, tt, step, 0)

    @pl.when(t == pl.num_programs(1) - 1)
    def _fin():
        sout_ref[...] = s_ref[...]


def _wkv(p, s0, lane_consts, *, tt):
    t, _, hk, l = p.shape
    lane2 = pl.BlockSpec((hk, LANES), lambda g, i: (0, g))
    st = pl.BlockSpec((hk, hk, LANES), lambda g, i: (0, 0, g))
    seq = lambda: pltpu.VMEM((tt, hk, LANES), F32)
    return pl.pallas_call(
        functools.partial(_wkv_kernel, tt=tt),
        out_shape=[jax.ShapeDtypeStruct((t, hk, l), F32), jax.ShapeDtypeStruct((hk, hk, l), F32)],
        grid=(l // LANES, t // tt),
        in_specs=[pl.BlockSpec((tt, RWKV_STREAMS, hk, LANES), lambda g, i: (i, 0, 0, g)), st] + [lane2] * 5,
        out_specs=[pl.BlockSpec((tt, hk, LANES), lambda g, i: (i, 0, g)), st],
        scratch_shapes=[pltpu.VMEM((hk, hk, LANES), F32), seq(), seq(), seq(), seq(), seq()],
        compiler_params=_params("parallel", "arbitrary"),
        name="wkv",
    )(p, s0, *lane_consts)


def _gate_out_kernel(y_ref, g_ref, x_ref, w_ref, lg_ref, lb_ref, o_ref):
    m = _dot(y_ref[...] * g_ref[...], w_ref[...])
    o_ref[...] = _layer_norm_rows(DN_ALPHA * x_ref[...] + m, lg_ref[...], lb_ref[...])


def _gate_out(y, g, x, w, lg, lb, *, tm):
    n, d = x.shape
    row = pl.BlockSpec((tm, d), lambda i: (i, 0))
    consts = [w, lg, lb]
    return pl.pallas_call(
        _gate_out_kernel,
        out_shape=jax.ShapeDtypeStruct((n, d), F32),
        grid=(n // tm,),
        in_specs=[row, row, row] + [_const_spec(c.shape) for c in consts],
        out_specs=row,
        compiler_params=_params("parallel"),
        name="gate_out",
    )(y, g, x, *consts)


def _pad_cols(w):
    return jnp.pad(w, ((0, 0), (0, LANES - w.shape[1]))).astype(BF16)


def _pad_rows(w):
    return jnp.pad(w, ((0, LANES - w.shape[0]), (0, 0))).astype(BF16)


def _rwkv_weights(mu, w_r, w_k, w_v, w_o, w0, w1, w2, a0, a1, a2, g1, g2, k_k, k_a, r_k, lnx_g, lnx_b, vres):
    consts = [jnp.pad(mu, ((0, SUBLANES - mu.shape[0]), (0, 0))), w_r.astype(BF16), w_k.astype(BF16), w_v.astype(BF16),
              _pad_cols(w1), _pad_rows(w2), _pad_cols(a1), _pad_rows(a2), _pad_cols(g1), _pad_rows(g2), w0[None], a0[None]]
    vconsts = None if vres is None else [_pad_cols(vres[1]), _pad_rows(vres[2]), vres[0][None]]
    per_head = [k_k.reshape(RWKV_HEADS, RWKV_HEAD).T, k_a.reshape(RWKV_HEADS, RWKV_HEAD).T, r_k.T,
                lnx_g.reshape(RWKV_HEADS, RWKV_HEAD).T, lnx_b.reshape(RWKV_HEADS, RWKV_HEAD).T]
    return consts, vconsts, per_head, w_o.astype(BF16)


def _rwkv_layer(x, groups, weights, v_first, g, beta, *, tm):
    consts, vconsts, per_head, w_o = weights
    n, d = x.shape
    prevs = []
    for off, nb, t, tt, s, shift in groups:
        xg = x[off:off + nb * t].reshape(nb, t, d)
        prevs.append(jnp.concatenate([shift[:, None], xg[:, :-1]], axis=1).reshape(nb * t, d))
    n_used = sum(p.shape[0] for p in prevs)
    xprev = jnp.concatenate(prevs + [jnp.zeros((n - n_used, d), F32)], axis=0)
    vres = None if vconsts is None else (v_first, vconsts)
    p, gate = _rwkv_in(x, xprev, consts, vres, tm=tm)
    ys, states = [], []
    for off, nb, t, tt, s, shift in groups:
        lanes = nb * RWKV_HEADS
        pg = p[off:off + nb * t].reshape(nb, t, RWKV_STREAMS, RWKV_HEADS, RWKV_HEAD)
        pg = jnp.transpose(pg, (1, 2, 4, 0, 3)).reshape(t, RWKV_STREAMS, RWKV_HEAD, lanes)
        s0 = jnp.transpose(s, (3, 2, 0, 1)).reshape(RWKV_HEAD, RWKV_HEAD, lanes)
        lane_consts = [jnp.tile(c, (1, nb)) for c in per_head]
        y, s_out = _wkv(pg, s0, lane_consts, tt=tt)
        ys.append(jnp.transpose(y.reshape(t, RWKV_HEAD, nb, RWKV_HEADS), (2, 0, 3, 1)).reshape(nb * t, d))
        s_new = jnp.transpose(s_out.reshape(RWKV_HEAD, RWKV_HEAD, nb, RWKV_HEADS), (2, 3, 1, 0))
        states.append((s_new, x[off:off + nb * t].reshape(nb, t, d)[:, -1]))
    y = jnp.concatenate(ys + [jnp.zeros((n - n_used, d), F32)], axis=0)
    return _gate_out(y, gate, x, w_o, g, beta, tm=tm), states, p[:, 2 * d:3 * d]


TOKEN_TILE = 512
PEER_TOKEN_TILE = 1024
PROMPT_TIME_BLOCK = 688
WKV_TIME_BLOCK = 16


def kernel(x_prompt, x_sample, state_gla, state_lru_h, state_lru_conv, state_rwkv, state_rwkv_shift, meta_tokens, ln_g, ln_b, ev_w_in, ev_gla_w_gate, ev_gla_b_gate, ev_gla_norm, ev_conv_w, ev_conv_b, ev_lru_wa, ev_lru_ba, ev_lru_wx, ev_lru_bx, ev_lru_lambda, ev_w_out, od_mu, od_w_r, od_w_k, od_w_v, od_w_o, od_w0, od_w1, od_w2, od_a0, od_a1, od_a2, od_v0, od_v1, od_v2, od_g1, od_g2, od_k_k, od_k_a, od_r_k, od_lnx_g, od_lnx_b, peer_w_q, peer_keys, peer_u, peer_v):
    bp, sp, d = x_prompt.shape
    bs, ss, _ = x_sample.shape
    tp = sp + N_META
    n_p, n_s = bp * tp, bs * ss
    n = -(-(n_p + n_s) // PEER_TOKEN_TILE) * PEER_TOKEN_TILE
    assert tp % PROMPT_TIME_BLOCK == 0 and tp % WKV_TIME_BLOCK == 0 and n % TOKEN_TILE == 0

    xp = jnp.concatenate([jnp.broadcast_to(meta_tokens[None], (bp, N_META, d)), x_prompt], axis=1)
    x = jnp.concatenate([xp.reshape(n_p, d), x_sample.reshape(n_s, d), jnp.zeros((n - n_p - n_s, d), F32)], axis=0)

    n_pairs = DEPTH // 2
    zeros = lambda *s: jnp.zeros(s, F32)
    outs = {k: [] for k in ("p_gla", "p_h", "p_conv", "p_rwkv", "p_shift", "s_gla", "s_h", "s_conv", "s_rwkv", "s_shift")}
    v_first = None
    for layer in range(DEPTH):
        j = layer // 2
        g, beta = ln_g[layer, 0][None], ln_b[layer, 0][None]
        if layer % 2 == 0:
            weights = _even_weights(ev_w_in[j], ev_gla_w_gate[j], ev_gla_b_gate[j], ev_gla_norm[j], ev_conv_w[j], ev_conv_b[j],
                                    ev_lru_wa[j], ev_lru_ba[j], ev_lru_wx[j], ev_lru_bx[j], ev_lru_lambda[j], ev_w_out[j])
            groups = [(0, bp, tp, PROMPT_TIME_BLOCK, zeros(bp, GLA_HEADS, GLA_DK, GLA_DV), zeros(bp, LRU_WIDTH),
                       zeros(bp, CONV_W - 1, LRU_WIDTH)),
                      (n_p, bs, ss, None, state_gla[j], state_lru_h[j], state_lru_conv[j])]
            x, states = _even_layer(x, groups, weights, g, beta, tm=TOKEN_TILE)
            for pre, st in zip("ps", states):
                outs[pre + "_gla"].append(st[0])
                outs[pre + "_h"].append(st[1])
                outs[pre + "_conv"].append(st[2])
        else:
            vres = None if j == 0 else (od_v0[j - 1], od_v1[j - 1], od_v2[j - 1])
            weights = _rwkv_weights(od_mu[j], od_w_r[j], od_w_k[j], od_w_v[j], od_w_o[j], od_w0[j], od_w1[j], od_w2[j],
                                    od_a0[j], od_a1[j], od_a2[j], od_g1[j], od_g2[j], od_k_k[j], od_k_a[j], od_r_k[j],
                                    od_lnx_g[j], od_lnx_b[j], vres)
            groups = [(0, bp, tp, WKV_TIME_BLOCK, zeros(bp, RWKV_HEADS, RWKV_HEAD, RWKV_HEAD), zeros(bp, d)),
                      (n_p, bs, ss, ss, state_rwkv[j], state_rwkv_shift[j])]
            x, states, v = _rwkv_layer(x, groups, weights, v_first, g, beta, tm=TOKEN_TILE)
            if v_first is None:
                v_first = v
            for pre, st in zip("ps", states):
                outs[pre + "_rwkv"].append(st[0])
                outs[pre + "_shift"].append(st[1])
        x = _peer_layer(x, peer_w_q[layer].T.astype(BF16), peer_keys[layer, :, 0].astype(BF16),
                        peer_keys[layer, :, 1].astype(BF16), peer_u[layer].astype(BF16), peer_v[layer].T.astype(BF16),
                        ln_g[layer, 1][None], ln_b[layer, 1][None], tn=PEER_TOKEN_TILE)

    y_prompt = x[:n_p].reshape(bp, tp, d)[:, N_META:]
    y_sample = x[n_p:n_p + n_s].reshape(bs, ss, d)
    st = {k: jnp.stack(v) for k, v in outs.items()}
    return (y_prompt, y_sample, st["p_gla"], st["p_h"], st["p_conv"], st["p_rwkv"], st["p_shift"],
            st["s_gla"], st["s_h"], st["s_conv"], st["s_rwkv"], st["s_shift"])
```

```python
import functools

import jax
import jax.numpy as jnp
from jax import lax
from jax.experimental import pallas as pl
from jax.experimental.pallas import tpu as pltpu

F32 = jnp.float32
BF16 = jnp.bfloat16

D_MODEL = 1024
DEPTH = 4
N_META = 16
GLA_HEADS = 4
GLA_DK = 64
GLA_DV = 128
GLA_RANK = 16
GLA_GATE_NORM = 16.0
GLA_CHUNK = 16
LRU_WIDTH = 512
LRU_BLOCKS = 8
CONV_W = 4
LRU_C = 8.0
RWKV_HEAD = 64
RWKV_HEADS = D_MODEL // RWKV_HEAD
RWKV_GN_EPS = 64e-5
PEER_HEADS = 8
PEER_NKEYS = 128
PEER_DKEY = 256
PEER_HALF = PEER_DKEY // 2
PEER_TOPK = 16
DN_ALPHA = float((2 * DEPTH) ** 0.25)
LN_EPS = 1e-5

LANES = 128
SUBLANES = 8
VMEM_LIMIT = 56 * 1024 * 1024


def _dot(a, b):
    return jnp.dot(a.astype(BF16), b.astype(BF16), preferred_element_type=F32)


def _dot_nt(a, b):
    return lax.dot_general(a.astype(BF16), b.astype(BF16), (((1,), (1,)), ((), ())), preferred_element_type=F32)


def _dot_tn(a, b):
    return lax.dot_general(a.astype(BF16), b.astype(BF16), (((0,), (0,)), ((), ())), preferred_element_type=F32)


def _layer_norm_rows(z, g, b):
    mu = jnp.mean(z, axis=-1, keepdims=True)
    d = z - mu
    var = jnp.mean(d * d, axis=-1, keepdims=True)
    return d * lax.rsqrt(var + LN_EPS) * g + b


def _sigmoid(x):
    return 1.0 / (1.0 + jnp.exp(-x))


def _softplus(x):
    return jnp.maximum(x, 0.0) + jnp.log1p(jnp.exp(-jnp.abs(x)))


def _gelu_tanh(x):
    c0 = 0.7978845608028654
    c1 = c0 * 0.044715
    one = jnp.asarray(1.0, x.dtype)
    return (0.5 * x) * (one + jnp.tanh(x * (c0 + c1 * (x * x))))


def _silu(x):
    return x * _sigmoid(x)


def _params(*sem):
    return pltpu.CompilerParams(dimension_semantics=sem, vmem_limit_bytes=VMEM_LIMIT)


def _const_spec(shape):
    nd = len(shape)
    return pl.BlockSpec(shape, lambda *_: (0,) * nd)


PEER_CAND_ROWS = 80
PEER_PACK = 2 * SUBLANES
PEER_SELECT_LANES = 2 * LANES


def _peer_cand_index():
    rows = [r1 * PEER_TOPK for r1 in range(16)]
    for j in range(1, 8):
        rows += [r1 * PEER_TOPK + j for r1 in range(8)]
    rows += list(range(8, 16))
    return jnp.broadcast_to(jnp.asarray(rows, F32)[:, None], (PEER_CAND_ROWS, PEER_SELECT_LANES))


PEER_CODE_STEP = 2.0 ** 116
PEER_CODED_BELOW = -(2.0 ** 119)


def _rank_code(r):
    return -PEER_CODE_STEP * (PEER_TOPK + r)


def _decode_rank(s):
    return jnp.where(s < PEER_CODED_BELOW, s * (-1.0 / PEER_CODE_STEP) - PEER_TOPK, float(PEER_TOPK))


def _count_coded(s):
    return jnp.sum(jnp.where(s < PEER_CODED_BELOW, 1.0, 0.0), axis=0, keepdims=True)


def _top16(s, key_iota):
    row16 = lax.broadcasted_iota(jnp.int32, (PEER_TOPK, s.shape[1]), 0)
    vals = jnp.zeros((PEER_TOPK, s.shape[1]), F32)
    for r in range(PEER_TOPK):
        m = jnp.max(s, axis=0, keepdims=True)
        hit = s == m
        if key_iota is not None:
            first = jnp.min(jnp.where(hit, key_iota, float(PEER_NKEYS)), axis=0, keepdims=True)
            hit = key_iota == first
        s = jnp.where(hit, _rank_code(r), s)
        vals = jnp.where(row16 == r, m, vals)
    return vals, s


def _peer_candidates(a, b):
    blocks = [a + b[0:1]]
    for j in range(1, 8):
        blocks.append(a[0:8] + b[j:j + 1])
    blocks.append(a[0:1] + b[8:16])
    return jnp.concatenate(blocks, axis=0)


def _select16(c, cidx):
    for _ in range(PEER_TOPK):
        m = jnp.max(c, axis=0, keepdims=True)
        hit = c == m
        if cidx is not None:
            first = jnp.min(jnp.where(hit, cidx, 1e9), axis=0, keepdims=True)
            hit = cidx == first
        c = jnp.where(hit, _rank_code(0), c)
    return c


def _peer_gate_tiles(s1, s2, a, b, s1c, s2c, c, cc):
    sel = jnp.where(cc < PEER_CODED_BELOW, 1.0, 0.0)
    z = jnp.sum(sel * jnp.exp(c - c[0:1]), axis=0, keepdims=True)
    j_lo = sel[0:8]
    for j in range(1, 8):
        j_lo = j_lo + sel[8 + 8 * j:16 + 8 * j]
    extra = jnp.sum(sel[72:80], axis=0, keepdims=True)
    row8 = lax.broadcasted_iota(jnp.int32, j_lo.shape, 0)
    j_lo = j_lo + jnp.where(row8 == 0, extra, 0.0)
    jt = jnp.concatenate([j_lo, sel[8:16]], axis=0)
    r1 = _decode_rank(s1c)
    h1 = jnp.zeros_like(r1)
    for r in range(PEER_TOPK):
        h1 = jnp.where(r1 == float(r), jt[r:r + 1], h1)
    return h1, jnp.exp(s1 - a[0:1]) / z, _decode_rank(s2c), jnp.exp(s2 - b[0:1])


def _peer_kernel(x_ref, wqt_ref, k1_ref, k2_ref, cidx_ref, u_ref, vt_ref, g_ref, b_ref, o_ref,
                 xt_ref, acc_ref, h1_ref, c1_ref, r2_ref, e2_ref, s_ref, hd0_ref, hd1_ref, coef0_ref, coef1_ref, *, tn, te):
    j = pl.program_id(1)
    nj = pl.num_programs(1)

    @pl.when(j == 0)
    def _select():
        xt_ref[...] = x_ref[...].T.astype(BF16)
        acc_ref[...] = jnp.zeros_like(acc_ref)
        hd1_ref[...] = jnp.zeros_like(hd1_ref)
        coef0_ref[...] = jnp.zeros_like(coef0_ref)
        key_iota = lax.broadcasted_iota(jnp.int32, (PEER_NKEYS, PEER_SELECT_LANES), 0).astype(F32)

        def head(h, carry):
            q = _dot(wqt_ref[pl.ds(pl.multiple_of(h * PEER_DKEY, PEER_DKEY), PEER_DKEY), :], xt_ref[...])
            mu = jnp.mean(q, axis=0, keepdims=True)
            d = q - mu
            qn = d * lax.rsqrt(jnp.mean(d * d, axis=0, keepdims=True) + LN_EPS)
            s_ref[0] = _dot(k1_ref[h], qn[0:PEER_HALF])
            s_ref[1] = _dot(k2_ref[h], qn[PEER_HALF:PEER_DKEY])

            def chunk(ci, carry2):
                ls = pl.ds(pl.multiple_of(ci * PEER_SELECT_LANES, PEER_SELECT_LANES), PEER_SELECT_LANES)
                s1 = s_ref[0, :, ls]
                s2 = s_ref[1, :, ls]

                def put(tiles):
                    h1_ref[h, :, ls] = tiles[0]
                    c1_ref[h, :, ls] = tiles[1]
                    r2_ref[h, :, ls] = tiles[2].astype(BF16)
                    e2_ref[h, :, ls] = tiles[3].astype(BF16)

                a, s1c = _top16(s1, None)
                b, s2c = _top16(s2, None)
                c = _peer_candidates(a, b)
                cc = _select16(c, None)
                put(_peer_gate_tiles(s1, s2, a, b, s1c, s2c, c, cc))
                miscount = (jnp.abs(_count_coded(s1c) - PEER_TOPK) + jnp.abs(_count_coded(s2c) - PEER_TOPK)
                            + jnp.abs(_count_coded(cc) - PEER_TOPK))

                @pl.when(jnp.max(miscount) > 0.0)
                def _with_ties():
                    a, s1c = _top16(s1, key_iota)
                    b, s2c = _top16(s2, key_iota)
                    c = _peer_candidates(a, b)
                    cc = _select16(c, cidx_ref[...])
                    put(_peer_gate_tiles(s1, s2, a, b, s1c, s2c, c, cc))

                return carry2

            lax.fori_loop(0, tn // PEER_SELECT_LANES, chunk, 0)
            return carry

        lax.fori_loop(0, PEER_HEADS, head, 0)

    groups = PEER_NKEYS // PEER_PACK
    last_key = PEER_NKEYS - 1
    zero_pack = jnp.zeros((PEER_PACK, LANES), F32)

    def gate_rows(tile):
        rows = []
        for ii in range(te // PEER_NKEYS):
            i1 = jnp.clip(tile * (te // PEER_NKEYS) + ii, 0, last_key)
            rows.append(([h1_ref[h, pl.ds(i1, 1), :] for h in range(PEER_HEADS)],
                         [c1_ref[h, pl.ds(i1, 1), :] for h in range(PEER_HEADS)]))
        return rows

    def gate_stage(rows, hd_ref, coef_ref, lane_groups):
        for ii, (bound_rows, scale_rows) in enumerate(rows):
            for c in lane_groups:
                ls = slice(c * LANES, (c + 1) * LANES)
                bounds = [(bound_rows[h][:, ls] + zero_pack).astype(BF16) for h in range(PEER_HEADS)]
                scales = [(scale_rows[h][:, ls] + zero_pack).astype(BF16) for h in range(PEER_HEADS)]
                for gi in range(groups):
                    krows = slice(gi * PEER_PACK, (gi + 1) * PEER_PACK)
                    gate = jnp.zeros((PEER_PACK, LANES), BF16)
                    for h in range(PEER_HEADS):
                        gate = gate + jnp.where(r2_ref[h, krows, ls] < bounds[h], e2_ref[h, krows, ls] * scales[h],
                                                jnp.zeros((), BF16))
                    erows = slice(ii * PEER_NKEYS + gi * PEER_PACK, ii * PEER_NKEYS + (gi + 1) * PEER_PACK)
                    coef_ref[erows, ls] = gate * _gelu_tanh(hd_ref[erows, ls].astype(BF16))

    all_lanes = range(tn // LANES)
    xt = xt_ref[...]
    hd0_ref[...] = _dot(u_ref[0:te, :], xt)
    gate_stage(gate_rows(2 * j - 1), hd1_ref, coef1_ref, all_lanes)
    acc_ref[...] += jnp.dot(vt_ref[:, 0:te], coef0_ref[...], preferred_element_type=F32)
    hd1_ref[...] = _dot(u_ref[te:2 * te, :], xt)
    gate_stage(gate_rows(2 * j), hd0_ref, coef0_ref, all_lanes)
    acc_ref[...] += jnp.dot(vt_ref[:, te:2 * te], coef1_ref[...], preferred_element_type=F32)

    @pl.when(j == nj - 1)
    def _finish():
        y = acc_ref[...].T
        o_ref[...] = _layer_norm_rows(DN_ALPHA * x_ref[...] + y, g_ref[...], b_ref[...])


def _peer_layer(x, wqt, k1, k2, u, vt, g, b, *, tn=1024, te=256):
    n, d = x.shape
    ne = u.shape[0]
    assert ne % (2 * te) == 0 and n % tn == 0
    pairs = ne // (2 * te)
    steps = pairs + 1
    kern = functools.partial(_peer_kernel, tn=tn, te=te)
    one = pl.Buffered(1)
    return pl.pallas_call(
        kern,
        out_shape=jax.ShapeDtypeStruct((n, d), F32),
        grid=(n // tn, steps),
        in_specs=[
            pl.BlockSpec((tn, d), lambda i, j: (i, 0), pipeline_mode=one),
            pl.BlockSpec(wqt.shape, lambda i, j: (0, 0), pipeline_mode=one),
            pl.BlockSpec(k1.shape, lambda i, j: (0, 0, 0), pipeline_mode=one),
            pl.BlockSpec(k2.shape, lambda i, j: (0, 0, 0), pipeline_mode=one),
            pl.BlockSpec((PEER_CAND_ROWS, PEER_SELECT_LANES), lambda i, j: (0, 0), pipeline_mode=one),
            pl.BlockSpec((2 * te, d), lambda i, j: (jnp.minimum(j, pairs - 1), 0)),
            pl.BlockSpec((d, 2 * te), lambda i, j: (0, jnp.maximum(j - 1, 0))),
            pl.BlockSpec((1, d), lambda i, j: (0, 0), pipeline_mode=one),
            pl.BlockSpec((1, d), lambda i, j: (0, 0), pipeline_mode=one),
        ],
        out_specs=pl.BlockSpec((tn, d), lambda i, j: (i, 0), pipeline_mode=one),
        scratch_shapes=[
            pltpu.VMEM((d, tn), BF16),
            pltpu.VMEM((d, tn), F32),
            pltpu.VMEM((PEER_HEADS, PEER_NKEYS, tn), F32),
            pltpu.VMEM((PEER_HEADS, PEER_NKEYS, tn), F32),
            pltpu.VMEM((PEER_HEADS, PEER_NKEYS, tn), BF16),
            pltpu.VMEM((PEER_HEADS, PEER_NKEYS, tn), BF16),
            pltpu.VMEM((2, PEER_NKEYS, tn), F32),
            pltpu.VMEM((te, tn), F32),
            pltpu.VMEM((te, tn), F32),
            pltpu.VMEM((te, tn), BF16),
            pltpu.VMEM((te, tn), BF16),
        ],
        compiler_params=_params("parallel", "arbitrary"),
        name="peer",
    )(x, wqt, k1, k2, _peer_cand_index(), u, vt, g, b)


GLA_PAD = GLA_HEADS * LANES


def _even_in_kernel(x_ref, wq_ref, wk_ref, wv_ref, wr_ref, wxb_ref, wgb_ref, wlr_ref, wg_ref, bg_ref,
                    qkl_ref, v_ref, rs_ref, xb_ref, gg_ref):
    xb16 = x_ref[...].astype(BF16)
    qkl_ref[:, 0:GLA_PAD] = _dot(xb16, wq_ref[...]) * (GLA_DK ** -0.5)
    qkl_ref[:, GLA_PAD:2 * GLA_PAD] = _dot(xb16, wk_ref[...])
    glr = _dot(xb16, wlr_ref[...])
    z = _dot(glr, wg_ref[...]) + bg_ref[...]
    qkl_ref[:, 2 * GLA_PAD:3 * GLA_PAD] = -_softplus(-z) * (1.0 / GLA_GATE_NORM)
    v_ref[...] = _dot(xb16, wv_ref[...])
    rs_ref[...] = _silu(_dot(xb16, wr_ref[...]))
    xb_ref[...] = _dot(xb16, wxb_ref[...])
    gg_ref[...] = _gelu_tanh(_dot(xb16, wgb_ref[...]))


def _even_in(x, consts, *, tm=512):
    n, d = x.shape
    row = lambda w: pl.BlockSpec((tm, w), lambda i: (i, 0))
    return pl.pallas_call(
        _even_in_kernel,
        out_shape=[jax.ShapeDtypeStruct((n, 3 * GLA_PAD), F32)] + [jax.ShapeDtypeStruct((n, LRU_WIDTH), F32)] * 4,
        grid=(n // tm,),
        in_specs=[row(d)] + [_const_spec(c.shape) for c in consts],
        out_specs=[row(3 * GLA_PAD)] + [row(LRU_WIDTH)] * 4,
        compiler_params=_params("parallel"),
        name="even_in",
    )(x, *consts)


def _gla_kernel(qkl_ref, v_ref, rs_ref, s0_ref, tri_ref, gn_ref, o_ref, sout_ref, st_ref, *, tb):
    t = pl.program_id(1)

    @pl.when(t == 0)
    def _init():
        st_ref[...] = s0_ref[0]

    tri = tri_ref[...]
    causal = tri > 0
    gn = gn_ref[...]

    def chunk(c, carry):
        rows = pl.ds(pl.multiple_of(c * GLA_CHUNK, GLA_CHUNK), GLA_CHUNK)
        q = qkl_ref[rows, 0:GLA_PAD]
        k = qkl_ref[rows, GLA_PAD:2 * GLA_PAD]
        la = qkl_ref[rows, 2 * GLA_PAD:3 * GLA_PAD]
        v = v_ref[rows, :]
        rs = rs_ref[rows, :]
        hi = la.astype(BF16)
        r1 = la - hi.astype(F32)
        mid = r1.astype(BF16)
        lo = (r1 - mid.astype(F32)).astype(BF16)
        cum = (jnp.dot(tri, hi, preferred_element_type=F32) + jnp.dot(tri, mid, preferred_element_type=F32)
               + jnp.dot(tri, lo, preferred_element_type=F32))
        last = cum[GLA_CHUNK - 1:GLA_CHUNK]
        qe = q * jnp.exp(cum)
        ke = k * jnp.exp(-cum)
        kl = k * jnp.exp(last - cum)
        dec = jnp.exp(last)
        for h in range(GLA_HEADS):
            sl = slice(h * LANES, (h + 1) * LANES)
            att = jnp.where(causal, _dot_nt(qe[:, sl], ke[:, sl]), 0.0)
            st = st_ref[h]
            o = _dot(att, v[:, sl]) + _dot_nt(qe[:, sl], st)
            st_ref[h] = st * dec[:, sl] + _dot_tn(v[:, sl], kl[:, sl])
            on = o * lax.rsqrt(jnp.mean(o * o, axis=-1, keepdims=True) + LN_EPS) * gn
            o_ref[rows, sl] = on * rs[:, sl]
        return carry

    lax.fori_loop(0, tb // GLA_CHUNK, chunk, 0)

    @pl.when(t == pl.num_programs(1) - 1)
    def _fin():
        sout_ref[0] = st_ref[...]


def _gla(qkl, v, rs, s0t, gn, *, nb, t, tb):
    nt = t // tb
    tri = jnp.tril(jnp.ones((GLA_CHUNK, GLA_CHUNK), BF16))
    row = lambda w: pl.BlockSpec((tb, w), lambda b, i: (b * nt + i, 0))
    st_spec = pl.BlockSpec((1, GLA_HEADS, GLA_DV, LANES), lambda b, i: (b, 0, 0, 0))
    return pl.pallas_call(
        functools.partial(_gla_kernel, tb=tb),
        out_shape=[jax.ShapeDtypeStruct((nb * t, GLA_HEADS * GLA_DV), F32), jax.ShapeDtypeStruct(s0t.shape, F32)],
        grid=(nb, nt),
        in_specs=[row(3 * GLA_PAD), row(GLA_HEADS * GLA_DV), row(GLA_HEADS * GLA_DV), st_spec,
                  _const_spec(tri.shape), _const_spec(gn.shape)],
        out_specs=[row(GLA_HEADS * GLA_DV), st_spec],
        scratch_shapes=[pltpu.VMEM((GLA_HEADS, GLA_DV, LANES), F32)],
        compiler_params=_params("parallel", "arbitrary"),
        name="gla",
    )(qkl, v, rs, s0t, tri, gn)


def _lru_kernel(xb_ref, gg_ref, buf0_ref, h0_ref, cw_ref, cb_ref, wa_ref, ba_ref, wx_ref, bx_ref, lam_ref,
                y_ref, hlast_ref, cbuf_ref, xs_ref, a_ref, b_ref, hs_ref, hcar_ref, *, tb, t_last):
    t = pl.program_id(1)
    halo = SUBLANES

    @pl.when(t == 0)
    def _init():
        xs_ref[0:halo] = buf0_ref[0]
        hcar_ref[...] = h0_ref[0]

    xs_ref[halo:halo + tb] = xb_ref[...]
    xc = cb_ref[...] + sum(xs_ref[pl.ds(halo - (CONV_W - 1) + i, tb), :] * cw_ref[i:i + 1, :] for i in range(CONV_W))
    ga = _sigmoid(_dot(xc, wa_ref[...]) + ba_ref[...])
    gx = _sigmoid(_dot(xc, wx_ref[...]) + bx_ref[...])
    log_at = ga * (-LRU_C * _softplus(-lam_ref[...]))
    a = jnp.exp(log_at)
    a_ref[...] = a
    b_ref[...] = jnp.sqrt(-jnp.tanh(log_at) * (a * a + 1.0)) * (gx * xc)

    def step(i, h):
        h = a_ref[pl.ds(i, 1), :] * h + b_ref[pl.ds(i, 1), :]
        hs_ref[pl.ds(i, 1), :] = h
        return h

    hcar_ref[...] = lax.fori_loop(0, tb, step, hcar_ref[...], unroll=8)
    y_ref[...] = hs_ref[...] * gg_ref[...]

    @pl.when(t == pl.num_programs(1) - 1)
    def _fin():
        hlast_ref[0] = hs_ref[t_last:t_last + 1, :]
        cbuf_ref[0] = xs_ref[halo + t_last - (CONV_W - 2):halo + t_last + 1, :]

    xs_ref[0:halo] = xs_ref[tb:tb + halo]


def _lru(xb, gg, buf0, h0, consts, *, nb, t, tb, t_last):
    nt = t // tb
    w = LRU_WIDTH
    row = pl.BlockSpec((tb, w), lambda b, i: (b * nt + i, 0))
    per_b = lambda r: pl.BlockSpec((1, r, w), lambda b, i: (b, 0, 0))
    return pl.pallas_call(
        functools.partial(_lru_kernel, tb=tb, t_last=t_last),
        out_shape=[jax.ShapeDtypeStruct((nb * t, w), F32), jax.ShapeDtypeStruct((nb, 1, w), F32),
                   jax.ShapeDtypeStruct((nb, CONV_W - 1, w), F32)],
        grid=(nb, nt),
        in_specs=[row, row, per_b(SUBLANES), per_b(1)] + [_const_spec(c.shape) for c in consts],
        out_specs=[row, per_b(1), per_b(CONV_W - 1)],
        scratch_shapes=[pltpu.VMEM((tb + 2 * SUBLANES, w), F32), pltpu.VMEM((tb, w), F32), pltpu.VMEM((tb, w), F32),
                        pltpu.VMEM((tb, w), F32), pltpu.VMEM((1, w), F32)],
        compiler_params=_params("parallel", "arbitrary"),
        name="lru",
    )(xb, gg, buf0, h0, *consts)


def _mix_out_kernel(a_ref, b_ref, x_ref, wa_ref, wb_ref, g_ref, beta_ref, o_ref):
    m = _dot(a_ref[...], wa_ref[...]) + _dot(b_ref[...], wb_ref[...])
    o_ref[...] = _layer_norm_rows(DN_ALPHA * x_ref[...] + m, g_ref[...], beta_ref[...])


def _mix_out(a, b, x, wa, wb, g, beta, *, tm=512):
    n, d = x.shape
    row = lambda w: pl.BlockSpec((tm, w), lambda i: (i, 0))
    consts = [wa, wb, g, beta]
    return pl.pallas_call(
        _mix_out_kernel,
        out_shape=jax.ShapeDtypeStruct((n, d), F32),
        grid=(n // tm,),
        in_specs=[row(a.shape[1]), row(b.shape[1]), row(d)] + [_const_spec(c.shape) for c in consts],
        out_specs=row(d),
        compiler_params=_params("parallel"),
        name="mix_out",
    )(a, b, x, *consts)


def _pad_heads(w, heads, width):
    lead = w.shape[:-1]
    w = w.reshape(lead + (heads, width))
    w = jnp.pad(w, [(0, 0)] * len(lead) + [(0, 0), (0, LANES - width)])
    return w.reshape(lead + (heads * LANES,))


def _even_weights(w_in, w_gate, b_gate, gla_norm, conv_w, conv_b, lru_wa, lru_ba, lru_wx, lru_bx, lru_lambda, w_out):
    gk = GLA_HEADS * GLA_DK
    gv = GLA_HEADS * GLA_DV
    o = 0
    cols = {}
    for name, width in (("q", gk), ("k", gk), ("v", gv), ("lr", GLA_RANK), ("r", gv), ("xb", LRU_WIDTH), ("gb", LRU_WIDTH)):
        cols[name] = w_in[:, o:o + width]
        o += width
    wlr = jnp.pad(cols["lr"], ((0, 0), (0, LANES - GLA_RANK)))
    wg = jnp.pad(_pad_heads(w_gate, GLA_HEADS, GLA_DK), ((0, LANES - GLA_RANK), (0, 0)))
    bg = _pad_heads(b_gate[None], GLA_HEADS, GLA_DK)
    in_consts = [_pad_heads(cols["q"], GLA_HEADS, GLA_DK).astype(BF16), _pad_heads(cols["k"], GLA_HEADS, GLA_DK).astype(BF16),
                 cols["v"].astype(BF16), cols["r"].astype(BF16), cols["xb"].astype(BF16), cols["gb"].astype(BF16),
                 wlr.astype(BF16), wg.astype(BF16), bg]
    eye = jnp.eye(LRU_BLOCKS, dtype=F32)
    bd = lambda w: (eye[:, None, :, None] * w[:, :, None, :]).reshape(LRU_WIDTH, LRU_WIDTH).astype(BF16)
    lru_consts = [conv_w, conv_b[None], bd(lru_wa), lru_ba[None], bd(lru_wx), lru_bx[None], lru_lambda[None]]
    return in_consts, gla_norm[None], lru_consts, w_out[:gv].astype(BF16), w_out[gv:].astype(BF16)


def _seq_rows(a, off, nb, t, t_pad):
    a = a[off:off + nb * t].reshape(nb, t, a.shape[1])
    return jnp.pad(a, ((0, 0), (0, t_pad - t), (0, 0))).reshape(nb * t_pad, a.shape[2])


def _unpad_rows(a, nb, t, t_pad):
    return a.reshape(nb, t_pad, a.shape[1])[:, :t].reshape(nb * t, a.shape[1])


def _even_layer(x, groups, weights, g, beta, *, tm):
    in_consts, gn, lru_consts, wo_a, wo_b = weights
    qkl, v, rs, xb, gg = _even_in(x, in_consts, tm=tm)
    ogs, yls, states = [], [], []
    for off, nb, t, tb, s_gla, h_lru, cbuf in groups:
        s0t = jnp.pad(jnp.swapaxes(s_gla, 2, 3), ((0, 0), (0, 0), (0, 0), (0, LANES - GLA_DK)))
        buf0 = jnp.pad(cbuf, ((0, 0), (SUBLANES - (CONV_W - 1), 0), (0, 0)))
        if tb is None:
            tg = GLA_CHUNK
            tl = SUBLANES
            og, st = _gla(_seq_rows(qkl, off, nb, t, tg), _seq_rows(v, off, nb, t, tg), _seq_rows(rs, off, nb, t, tg),
                          s0t, gn, nb=nb, t=tg, tb=tg)
            yl, hl, cb = _lru(_seq_rows(xb, off, nb, t, tl), _seq_rows(gg, off, nb, t, tl), buf0, h_lru[:, None],
                              lru_consts, nb=nb, t=tl, tb=tl, t_last=t - 1)
            og = _unpad_rows(og, nb, t, tg)
            yl = _unpad_rows(yl, nb, t, tl)
        else:
            og, st = _gla(qkl, v, rs, s0t, gn, nb=nb, t=t, tb=tb)
            yl, hl, cb = _lru(xb, gg, buf0, h_lru[:, None], lru_consts, nb=nb, t=t, tb=tb, t_last=tb - 1)
        ogs.append(og)
        yls.append(yl)
        states.append((jnp.swapaxes(st[..., :GLA_DK], 2, 3), hl[:, 0], cb))
    n_used = sum(o.shape[0] for o in ogs)
    tail = jnp.zeros((x.shape[0] - n_used, LRU_WIDTH), F32)
    og = jnp.concatenate(ogs + [tail], axis=0)
    yl = jnp.concatenate(yls + [tail], axis=0)
    return _mix_out(og, yl, x, wo_a, wo_b, g, beta, tm=tm), states


RWKV_STREAMS = 5


def _rwkv_in_kernel(*refs, has_vres):
    if has_vres:
        (x_ref, xp_ref, mu_ref, wr_ref, wk_ref, wv_ref, w1_ref, w2_ref, a1_ref, a2_ref, g1_ref, g2_ref, w0_ref, a0_ref,
         vf_ref, v1_ref, v2_ref, v0_ref, p_ref, g_ref) = refs
    else:
        (x_ref, xp_ref, mu_ref, wr_ref, wk_ref, wv_ref, w1_ref, w2_ref, a1_ref, a2_ref, g1_ref, g2_ref, w0_ref, a0_ref,
         p_ref, g_ref) = refs
    d = D_MODEL
    x = x_ref[...]
    xx = xp_ref[...] - x
    mix = lambda i: (x + xx * mu_ref[i:i + 1, :]).astype(BF16)
    xr, xw, xk, xv, xa, xg = (mix(i) for i in range(6))
    p_ref[:, 0:d] = _dot(xr, wr_ref[...])
    p_ref[:, d:2 * d] = _dot(xk, wk_ref[...])
    v = _dot(xv, wv_ref[...])
    if has_vres:
        v = v + (vf_ref[...] - v) * _sigmoid(v0_ref[...] + _dot(_dot(xv, v1_ref[...]), v2_ref[...]))
    p_ref[:, 2 * d:3 * d] = v
    p_ref[:, 3 * d:4 * d] = w0_ref[...] + _dot(jnp.tanh(_dot(xw, w1_ref[...])), w2_ref[...])
    p_ref[:, 4 * d:5 * d] = a0_ref[...] + _dot(_dot(xa, a1_ref[...]), a2_ref[...])
    g_ref[...] = _dot(_sigmoid(_dot(xg, g1_ref[...])), g2_ref[...])


def _rwkv_in(x, xprev, consts, vres, *, tm):
    n, d = x.shape
    row = lambda w: pl.BlockSpec((tm, w), lambda i: (i, 0))
    args = [x, xprev] + list(consts)
    specs = [row(d), row(d)] + [_const_spec(c.shape) for c in consts]
    if vres is not None:
        vf, vconsts = vres
        args += [vf] + list(vconsts)
        specs += [row(d)] + [_const_spec(c.shape) for c in vconsts]
    return pl.pallas_call(
        functools.partial(_rwkv_in_kernel, has_vres=vres is not None),
        out_shape=[jax.ShapeDtypeStruct((n, RWKV_STREAMS * d), F32), jax.ShapeDtypeStruct((n, d), F32)],
        grid=(n // tm,),
        in_specs=specs,
        out_specs=[row(RWKV_STREAMS * d), row(d)],
        compiler_params=_params("parallel"),
        name="rwkv_in",
    )(*args)


def _wkv_kernel(p_ref, s0_ref, kk_ref, ka_ref, rk_ref, lg_ref, lb_ref, y_ref, sout_ref,
                s_ref, dec_ref, a_ref, b_ref, km_ref, bon_ref, *, tt):
    t = pl.program_id(1)
    hk = RWKV_HEAD

    @pl.when(t == 0)
    def _init():
        s_ref[...] = s0_ref[...]

    r = p_ref[:, 0]
    k = p_ref[:, 1]
    v = p_ref[:, 2]
    dec_ref[...] = jnp.exp(-jnp.exp(-_softplus(-p_ref[:, 3]) - 0.5))
    ag = _sigmoid(p_ref[:, 4])
    kk = k * kk_ref[...][None]
    kk = kk / jnp.maximum(jnp.sqrt(jnp.sum(kk * kk, axis=1, keepdims=True)), 1e-12)
    km = k * (1.0 + (ag - 1.0) * ka_ref[...][None])
    km_ref[...] = km
    a_ref[...] = -kk
    b_ref[...] = kk * ag
    bon_ref[...] = jnp.sum(r * km * rk_ref[...][None], axis=1, keepdims=True) * v
    lg = lg_ref[...]
    lb = lb_ref[...]

    def step(i, carry):
        sa = jnp.zeros((hk, LANES), F32)
        for q in range(hk):
            sa = sa + s_ref[q] * a_ref[i, q:q + 1, :]
        vv = p_ref[i, 2]
        y = jnp.zeros((hk, LANES), F32)
        for q in range(hk):
            sn = s_ref[q] * dec_ref[i, q:q + 1, :] + sa * b_ref[i, q:q + 1, :] + vv * km_ref[i, q:q + 1, :]
            s_ref[q] = sn
            y = y + sn * p_ref[i, 0, q:q + 1, :]
        mu = jnp.mean(y, axis=0, keepdims=True)
        dlt = y - mu
        var = jnp.mean(dlt * dlt, axis=0, keepdims=True)
        y_ref[i] = dlt * lax.rsqrt(var + RWKV_GN_EPS) * lg + lb + bon_ref[i]
        return carry

    lax.fori_loop(0, tt, step, 0)

    @pl.when(t == pl.num_programs(1) - 1)
    def _fin():
        sout_ref[...] = s_ref[...]


def _wkv(p, s0, lane_consts, *, tt):
    t, _, hk, l = p.shape
    lane2 = pl.BlockSpec((hk, LANES), lambda g, i: (0, g))
    st = pl.BlockSpec((hk, hk, LANES), lambda g, i: (0, 0, g))
    seq = lambda: pltpu.VMEM((tt, hk, LANES), F32)
    return pl.pallas_call(
        functools.partial(_wkv_kernel, tt=tt),
        out_shape=[jax.ShapeDtypeStruct((t, hk, l), F32), jax.ShapeDtypeStruct((hk, hk, l), F32)],
        grid=(l // LANES, t // tt),
        in_specs=[pl.BlockSpec((tt, RWKV_STREAMS, hk, LANES), lambda g, i: (i, 0, 0, g)), st] + [lane2] * 5,
        out_specs=[pl.BlockSpec((tt, hk, LANES), lambda g, i: (i, 0, g)), st],
        scratch_shapes=[pltpu.VMEM((hk, hk, LANES), F32), seq(), seq(), seq(), seq(), seq()],
        compiler_params=_params("parallel", "arbitrary"),
        name="wkv",
    )(p, s0, *lane_consts)


def _gate_out_kernel(y_ref, g_ref, x_ref, w_ref, lg_ref, lb_ref, o_ref):
    m = _dot(y_ref[...] * g_ref[...], w_ref[...])
    o_ref[...] = _layer_norm_rows(DN_ALPHA * x_ref[...] + m, lg_ref[...], lb_ref[...])


def _gate_out(y, g, x, w, lg, lb, *, tm):
    n, d = x.shape
    row = pl.BlockSpec((tm, d), lambda i: (i, 0))
    consts = [w, lg, lb]
    return pl.pallas_call(
        _gate_out_kernel,
        out_shape=jax.ShapeDtypeStruct((n, d), F32),
        grid=(n // tm,),
        in_specs=[row, row, row] + [_const_spec(c.shape) for c in consts],
        out_specs=row,
        compiler_params=_params("parallel"),
        name="gate_out",
    )(y, g, x, *consts)


def _pad_cols(w):
    return jnp.pad(w, ((0, 0), (0, LANES - w.shape[1]))).astype(BF16)


def _pad_rows(w):
    return jnp.pad(w, ((0, LANES - w.shape[0]), (0, 0))).astype(BF16)


def _rwkv_weights(mu, w_r, w_k, w_v, w_o, w0, w1, w2, a0, a1, a2, g1, g2, k_k, k_a, r_k, lnx_g, lnx_b, vres):
    consts = [jnp.pad(mu, ((0, SUBLANES - mu.shape[0]), (0, 0))), w_r.astype(BF16), w_k.astype(BF16), w_v.astype(BF16),
              _pad_cols(w1), _pad_rows(w2), _pad_cols(a1), _pad_rows(a2), _pad_cols(g1), _pad_rows(g2), w0[None], a0[None]]
    vconsts = None if vres is None else [_pad_cols(vres[1]), _pad_rows(vres[2]), vres[0][None]]
    per_head = [k_k.reshape(RWKV_HEADS, RWKV_HEAD).T, k_a.reshape(RWKV_HEADS, RWKV_HEAD).T, r_k.T,
                lnx_g.reshape(RWKV_HEADS, RWKV_HEAD).T, lnx_b.reshape(RWKV_HEADS, RWKV_HEAD).T]
    return consts, vconsts, per_head, w_o.astype(BF16)


def _rwkv_layer(x, groups, weights, v_first, g, beta, *, tm):
    consts, vconsts, per_head, w_o = weights
    n, d = x.shape
    prevs = []
    for off, nb, t, tt, s, shift in groups:
        xg = x[off:off + nb * t].reshape(nb, t, d)
        prevs.append(jnp.concatenate([shift[:, None], xg[:, :-1]], axis=1).reshape(nb * t, d))
    n_used = sum(p.shape[0] for p in prevs)
    xprev = jnp.concatenate(prevs + [jnp.zeros((n - n_used, d), F32)], axis=0)
    vres = None if vconsts is None else (v_first, vconsts)
    p, gate = _rwkv_in(x, xprev, consts, vres, tm=tm)
    ys, states = [], []
    for off, nb, t, tt, s, shift in groups:
        lanes = nb * RWKV_HEADS
        pg = p[off:off + nb * t].reshape(nb, t, RWKV_STREAMS, RWKV_HEADS, RWKV_HEAD)
        pg = jnp.transpose(pg, (1, 2, 4, 0, 3)).reshape(t, RWKV_STREAMS, RWKV_HEAD, lanes)
        s0 = jnp.transpose(s, (3, 2, 0, 1)).reshape(RWKV_HEAD, RWKV_HEAD, lanes)
        lane_consts = [jnp.tile(c, (1, nb)) for c in per_head]
        y, s_out = _wkv(pg, s0, lane_consts, tt=tt)
        ys.append(jnp.transpose(y.reshape(t, RWKV_HEAD, nb, RWKV_HEADS), (2, 0, 3, 1)).reshape(nb * t, d))
        s_new = jnp.transpose(s_out.reshape(RWKV_HEAD, RWKV_HEAD, nb, RWKV_HEADS), (2, 3, 1, 0))
        states.append((s_new, x[off:off + nb * t].reshape(nb, t, d)[:, -1]))
    y = jnp.concatenate(ys + [jnp.zeros((n - n_used, d), F32)], axis=0)
    return _gate_out(y, gate, x, w_o, g, beta, tm=tm), states, p[:, 2 * d:3 * d]


TOKEN_TILE = 512
PEER_TOKEN_TILE = 1024
PROMPT_TIME_BLOCK = 688
WKV_TIME_BLOCK = 16


def kernel(x_prompt, x_sample, state_gla, state_lru_h, state_lru_conv, state_rwkv, state_rwkv_shift, meta_tokens, ln_g, ln_b, ev_w_in, ev_gla_w_gate, ev_gla_b_gate, ev_gla_norm, ev_conv_w, ev_conv_b, ev_lru_wa, ev_lru_ba, ev_lru_wx, ev_lru_bx, ev_lru_lambda, ev_w_out, od_mu, od_w_r, od_w_k, od_w_v, od_w_o, od_w0, od_w1, od_w2, od_a0, od_a1, od_a2, od_v0, od_v1, od_v2, od_g1, od_g2, od_k_k, od_k_a, od_r_k, od_lnx_g, od_lnx_b, peer_w_q, peer_keys, peer_u, peer_v):
    bp, sp, d = x_prompt.shape
    bs, ss, _ = x_sample.shape
    tp = sp + N_META
    n_p, n_s = bp * tp, bs * ss
    n = -(-(n_p + n_s) // PEER_TOKEN_TILE) * PEER_TOKEN_TILE
    assert tp % PROMPT_TIME_BLOCK == 0 and tp % WKV_TIME_BLOCK == 0 and n % TOKEN_TILE == 0

    xp = jnp.concatenate([jnp.broadcast_to(meta_tokens[None], (bp, N_META, d)), x_prompt], axis=1)
    x = jnp.concatenate([xp.reshape(n_p, d), x_sample.reshape(n_s, d), jnp.zeros((n - n_p - n_s, d), F32)], axis=0)

    n_pairs = DEPTH // 2
    zeros = lambda *s: jnp.zeros(s, F32)
    outs = {k: [] for k in ("p_gla", "p_h", "p_conv", "p_rwkv", "p_shift", "s_gla", "s_h", "s_conv", "s_rwkv", "s_shift")}
    v_first = None
    for layer in range(DEPTH):
        j = layer // 2
        g, beta = ln_g[layer, 0][None], ln_b[layer, 0][None]
        if layer % 2 == 0:
            weights = _even_weights(ev_w_in[j], ev_gla_w_gate[j], ev_gla_b_gate[j], ev_gla_norm[j], ev_conv_w[j], ev_conv_b[j],
                                    ev_lru_wa[j], ev_lru_ba[j], ev_lru_wx[j], ev_lru_bx[j], ev_lru_lambda[j], ev_w_out[j])
            groups = [(0, bp, tp, PROMPT_TIME_BLOCK, zeros(bp, GLA_HEADS, GLA_DK, GLA_DV), zeros(bp, LRU_WIDTH),
                       zeros(bp, CONV_W - 1, LRU_WIDTH)),
                      (n_p, bs, ss, None, state_gla[j], state_lru_h[j], state_lru_conv[j])]
            x, states = _even_layer(x, groups, weights, g, beta, tm=TOKEN_TILE)
            for pre, st in zip("ps", states):
                outs[pre + "_gla"].append(st[0])
                outs[pre + "_h"].append(st[1])
                outs[pre + "_conv"].append(st[2])
        else:
            vres = None if j == 0 else (od_v0[j - 1], od_v1[j - 1], od_v2[j - 1])
            weights = _rwkv_weights(od_mu[j], od_w_r[j], od_w_k[j], od_w_v[j], od_w_o[j], od_w0[j], od_w1[j], od_w2[j],
                                    od_a0[j], od_a1[j], od_a2[j], od_g1[j], od_g2[j], od_k_k[j], od_k_a[j], od_r_k[j],
                                    od_lnx_g[j], od_lnx_b[j], vres)
            groups = [(0, bp, tp, WKV_TIME_BLOCK, zeros(bp, RWKV_HEADS, RWKV_HEAD, RWKV_HEAD), zeros(bp, d)),
                      (n_p, bs, ss, ss, state_rwkv[j], state_rwkv_shift[j])]
            x, states, v = _rwkv_layer(x, groups, weights, v_first, g, beta, tm=TOKEN_TILE)
            if v_first is None:
                v_first = v
            for pre, st in zip("ps", states):
                outs[pre + "_rwkv"].append(st[0])
                outs[pre + "_shift"].append(st[1])
        x = _peer_layer(x, peer_w_q[layer].T.astype(BF16), peer_keys[layer, :, 0].astype(BF16),
                        peer_keys[layer, :, 1].astype(BF16), peer_u[layer].astype(BF16), peer_v[layer].T.astype(BF16),
                        ln_g[layer, 1][None], ln_b[layer, 1][None], tn=PEER_TOKEN_TILE)

    y_prompt = x[:n_p].reshape(bp, tp, d)[:, N_META:]
    y_sample = x[n_p:n_p + n_s].reshape(bs, ss, d)
    st = {k: jnp.stack(v) for k, v in outs.items()}
    return (y_prompt, y_sample, st["p_gla"], st["p_h"], st["p_conv"], st["p_rwkv"], st["p_shift"],
            st["s_gla"], st["s_h"], st["s_conv"], st["s_rwkv"], st["s_shift"])
```

```python
import functools

import jax
import jax.numpy as jnp
from jax import lax
from jax.experimental import pallas as pl
from jax.experimental.pallas import tpu as pltpu

F32 = jnp.float32
BF16 = jnp.bfloat16

D_MODEL = 1024
DEPTH = 4
N_META = 16
GLA_HEADS = 4
GLA_DK = 64
GLA_DV = 128
GLA_RANK = 16
GLA_GATE_NORM = 16.0
GLA_CHUNK = 16
LRU_WIDTH = 512
LRU_BLOCKS = 8
CONV_W = 4
LRU_C = 8.0
RWKV_HEAD = 64
RWKV_HEADS = D_MODEL // RWKV_HEAD
RWKV_GN_EPS = 64e-5
PEER_HEADS = 8
PEER_NKEYS = 128
PEER_DKEY = 256
PEER_HALF = PEER_DKEY // 2
PEER_TOPK = 16
DN_ALPHA = float((2 * DEPTH) ** 0.25)
LN_EPS = 1e-5

LANES = 128
SUBLANES = 8
VMEM_LIMIT = 56 * 1024 * 1024


def _dot(a, b):
    return jnp.dot(a.astype(BF16), b.astype(BF16), preferred_element_type=F32)


def _dot_nt(a, b):
    return lax.dot_general(a.astype(BF16), b.astype(BF16), (((1,), (1,)), ((), ())), preferred_element_type=F32)


def _dot_tn(a, b):
    return lax.dot_general(a.astype(BF16), b.astype(BF16), (((0,), (0,)), ((), ())), preferred_element_type=F32)


def _layer_norm_rows(z, g, b):
    mu = jnp.mean(z, axis=-1, keepdims=True)
    d = z - mu
    var = jnp.mean(d * d, axis=-1, keepdims=True)
    return d * lax.rsqrt(var + LN_EPS) * g + b


def _sigmoid(x):
    return 1.0 / (1.0 + jnp.exp(-x))


def _softplus(x):
    return jnp.maximum(x, 0.0) + jnp.log1p(jnp.exp(-jnp.abs(x)))


def _gelu_tanh(x):
    c0 = 0.7978845608028654
    c1 = c0 * 0.044715
    one = jnp.asarray(1.0, x.dtype)
    return (0.5 * x) * (one + jnp.tanh(x * (c0 + c1 * (x * x))))


def _silu(x):
    return x * _sigmoid(x)


def _params(*sem):
    return pltpu.CompilerParams(dimension_semantics=sem, vmem_limit_bytes=VMEM_LIMIT)


def _const_spec(shape):
    nd = len(shape)
    return pl.BlockSpec(shape, lambda *_: (0,) * nd)


PEER_CAND_ROWS = 80
PEER_PACK = 2 * SUBLANES
PEER_SELECT_LANES = 2 * LANES


def _peer_cand_index():
    rows = [r1 * PEER_TOPK for r1 in range(16)]
    for j in range(1, 8):
        rows += [r1 * PEER_TOPK + j for r1 in range(8)]
    rows += list(range(8, 16))
    return jnp.broadcast_to(jnp.asarray(rows, F32)[:, None], (PEER_CAND_ROWS, PEER_SELECT_LANES))


PEER_CODE_STEP = 2.0 ** 116
PEER_CODED_BELOW = -(2.0 ** 119)


def _rank_code(r):
    return -PEER_CODE_STEP * (PEER_TOPK + r)


def _decode_rank(s):
    return jnp.where(s < PEER_CODED_BELOW, s * (-1.0 / PEER_CODE_STEP) - PEER_TOPK, float(PEER_TOPK))


def _count_coded(s):
    return jnp.sum(jnp.where(s < PEER_CODED_BELOW, 1.0, 0.0), axis=0, keepdims=True)


def _top16(s, key_iota):
    row16 = lax.broadcasted_iota(jnp.int32, (PEER_TOPK, s.shape[1]), 0)
    vals = jnp.zeros((PEER_TOPK, s.shape[1]), F32)
    for r in range(PEER_TOPK):
        m = jnp.max(s, axis=0, keepdims=True)
        hit = s == m
        if key_iota is not None:
            first = jnp.min(jnp.where(hit, key_iota, float(PEER_NKEYS)), axis=0, keepdims=True)
            hit = key_iota == first
        s = jnp.where(hit, _rank_code(r), s)
        vals = jnp.where(row16 == r, m, vals)
    return vals, s


def _peer_candidates(a, b):
    blocks = [a + b[0:1]]
    for j in range(1, 8):
        blocks.append(a[0:8] + b[j:j + 1])
    blocks.append(a[0:1] + b[8:16])
    return jnp.concatenate(blocks, axis=0)


def _select16(c, cidx):
    for _ in range(PEER_TOPK):
        m = jnp.max(c, axis=0, keepdims=True)
        hit = c == m
        if cidx is not None:
            first = jnp.min(jnp.where(hit, cidx, 1e9), axis=0, keepdims=True)
            hit = cidx == first
        c = jnp.where(hit, _rank_code(0), c)
    return c


def _peer_gate_tiles(s1, s2, a, b, s1c, s2c, c, cc):
    sel = jnp.where(cc < PEER_CODED_BELOW, 1.0, 0.0)
    z = jnp.sum(sel * jnp.exp(c - c[0:1]), axis=0, keepdims=True)
    j_lo = sel[0:8]
    for j in range(1, 8):
        j_lo = j_lo + sel[8 + 8 * j:16 + 8 * j]
    extra = jnp.sum(sel[72:80], axis=0, keepdims=True)
    row8 = lax.broadcasted_iota(jnp.int32, j_lo.shape, 0)
    j_lo = j_lo + jnp.where(row8 == 0, extra, 0.0)
    jt = jnp.concatenate([j_lo, sel[8:16]], axis=0)
    r1 = _decode_rank(s1c)
    h1 = jnp.zeros_like(r1)
    for r in range(PEER_TOPK):
        h1 = jnp.where(r1 == float(r), jt[r:r + 1], h1)
    return h1, jnp.exp(s1 - a[0:1]) / z, _decode_rank(s2c), jnp.exp(s2 - b[0:1])


def _peer_kernel(x_ref, wqt_ref, k1_ref, k2_ref, cidx_ref, u_ref, vt_ref, g_ref, b_ref, o_ref,
                 xt_ref, acc_ref, h1_ref, c1_ref, r2_ref, e2_ref, s_ref, hd0_ref, hd1_ref, coef_ref, *, tn, te):
    j = pl.program_id(1)
    nj = pl.num_programs(1)

    @pl.when(j == 0)
    def _select():
        xt_ref[...] = x_ref[...].T.astype(BF16)
        acc_ref[...] = jnp.zeros_like(acc_ref)
        hd1_ref[...] = jnp.zeros_like(hd1_ref)
        coef_ref[0:te] = jnp.zeros((te, tn), BF16)
        key_iota = lax.broadcasted_iota(jnp.int32, (PEER_NKEYS, PEER_SELECT_LANES), 0).astype(F32)

        def head(h, carry):
            q = _dot(wqt_ref[pl.ds(pl.multiple_of(h * PEER_DKEY, PEER_DKEY), PEER_DKEY), :], xt_ref[...])
            mu = jnp.mean(q, axis=0, keepdims=True)
            d = q - mu
            qn = d * lax.rsqrt(jnp.mean(d * d, axis=0, keepdims=True) + LN_EPS)
            s_ref[0] = _dot(k1_ref[h], qn[0:PEER_HALF])
            s_ref[1] = _dot(k2_ref[h], qn[PEER_HALF:PEER_DKEY])

            def chunk(ci, carry2):
                ls = pl.ds(pl.multiple_of(ci * PEER_SELECT_LANES, PEER_SELECT_LANES), PEER_SELECT_LANES)
                s1 = s_ref[0, :, ls]
                s2 = s_ref[1, :, ls]

                def put(tiles):
                    h1_ref[h, :, ls] = tiles[0]
                    c1_ref[h, :, ls] = tiles[1]
                    r2_ref[h, :, ls] = tiles[2].astype(BF16)
                    e2_ref[h, :, ls] = tiles[3].astype(BF16)

                a, s1c = _top16(s1, None)
                b, s2c = _top16(s2, None)
                c = _peer_candidates(a, b)
                cc = _select16(c, None)
                put(_peer_gate_tiles(s1, s2, a, b, s1c, s2c, c, cc))
                miscount = (jnp.abs(_count_coded(s1c) - PEER_TOPK) + jnp.abs(_count_coded(s2c) - PEER_TOPK)
                            + jnp.abs(_count_coded(cc) - PEER_TOPK))

                @pl.when(jnp.max(miscount) > 0.0)
                def _with_ties():
                    a, s1c = _top16(s1, key_iota)
                    b, s2c = _top16(s2, key_iota)
                    c = _peer_candidates(a, b)
                    cc = _select16(c, cidx_ref[...])
                    put(_peer_gate_tiles(s1, s2, a, b, s1c, s2c, c, cc))

                return carry2

            lax.fori_loop(0, tn // PEER_SELECT_LANES, chunk, 0)
            return carry

        lax.fori_loop(0, PEER_HEADS, head, 0)

    groups = PEER_NKEYS // PEER_PACK
    last_key = PEER_NKEYS - 1
    zero_pack = jnp.zeros((PEER_PACK, LANES), F32)

    def gate_rows(tile):
        rows = []
        for ii in range(te // PEER_NKEYS):
            i1 = jnp.clip(tile * (te // PEER_NKEYS) + ii, 0, last_key)
            rows.append(([h1_ref[h, pl.ds(i1, 1), :] for h in range(PEER_HEADS)],
                         [c1_ref[h, pl.ds(i1, 1), :] for h in range(PEER_HEADS)]))
        return rows

    def gate_stage(rows, hd_ref, coef_rows):
        nk = len(rows)
        for c in range(tn // LANES):
            ls = slice(c * LANES, (c + 1) * LANES)
            gates = [[jnp.zeros((PEER_PACK, LANES), BF16) for _ in range(groups)] for _ in range(nk)]
            for h in range(PEER_HEADS):
                bounds = [(rows[ii][0][h][:, ls] + zero_pack).astype(BF16) for ii in range(nk)]
                scales = [(rows[ii][1][h][:, ls] + zero_pack).astype(BF16) for ii in range(nk)]
                for gi in range(groups):
                    krows = slice(gi * PEER_PACK, (gi + 1) * PEER_PACK)
                    rank = r2_ref[h, krows, ls]
                    fac = e2_ref[h, krows, ls]
                    for ii in range(nk):
                        gates[ii][gi] = gates[ii][gi] + jnp.where(rank < bounds[ii], fac * scales[ii], jnp.zeros((), BF16))
            for ii in range(nk):
                for gi in range(groups):
                    erows = slice(ii * PEER_NKEYS + gi * PEER_PACK, ii * PEER_NKEYS + (gi + 1) * PEER_PACK)
                    crows = slice(coef_rows + erows.start, coef_rows + erows.stop)
                    coef_ref[crows, ls] = gates[ii][gi] * _gelu_tanh(hd_ref[erows, ls].astype(BF16))

    xt = xt_ref[...]
    hd0_ref[...] = _dot(u_ref[0:te, :], xt)
    gate_stage(gate_rows(2 * j - 1), hd1_ref, te)
    acc_ref[...] += jnp.dot(vt_ref[0], coef_ref[...], preferred_element_type=F32)
    hd1_ref[...] = _dot(u_ref[te:2 * te, :], xt)
    gate_stage(gate_rows(2 * j), hd0_ref, 0)

    @pl.when(j == nj - 1)
    def _finish():
        y = acc_ref[...].T
        o_ref[...] = _layer_norm_rows(DN_ALPHA * x_ref[...] + y, g_ref[...], b_ref[...])


def _peer_layer(x, wqt, k1, k2, u, v, g, b, *, tn=1024, te=256):
    n, d = x.shape
    ne = u.shape[0]
    assert ne % (2 * te) == 0 and n % tn == 0
    pairs = ne // (2 * te)
    steps = pairs + 1
    vt = jnp.swapaxes(v.reshape(pairs, 2 * te, d), 1, 2)
    kern = functools.partial(_peer_kernel, tn=tn, te=te)
    one = pl.Buffered(1)
    return pl.pallas_call(
        kern,
        out_shape=jax.ShapeDtypeStruct((n, d), F32),
        grid=(n // tn, steps),
        in_specs=[
            pl.BlockSpec((tn, d), lambda i, j: (i, 0), pipeline_mode=one),
            pl.BlockSpec(wqt.shape, lambda i, j: (0, 0), pipeline_mode=one),
            pl.BlockSpec(k1.shape, lambda i, j: (0, 0, 0), pipeline_mode=one),
            pl.BlockSpec(k2.shape, lambda i, j: (0, 0, 0), pipeline_mode=one),
            pl.BlockSpec((PEER_CAND_ROWS, PEER_SELECT_LANES), lambda i, j: (0, 0), pipeline_mode=one),
            pl.BlockSpec((2 * te, d), lambda i, j: (jnp.minimum(j, pairs - 1), 0)),
            pl.BlockSpec((1, d, 2 * te), lambda i, j: (jnp.maximum(j - 1, 0), 0, 0)),
            pl.BlockSpec((1, d), lambda i, j: (0, 0), pipeline_mode=one),
            pl.BlockSpec((1, d), lambda i, j: (0, 0), pipeline_mode=one),
        ],
        out_specs=pl.BlockSpec((tn, d), lambda i, j: (i, 0), pipeline_mode=one),
        scratch_shapes=[
            pltpu.VMEM((d, tn), BF16),
            pltpu.VMEM((d, tn), F32),
            pltpu.VMEM((PEER_HEADS, PEER_NKEYS, tn), F32),
            pltpu.VMEM((PEER_HEADS, PEER_NKEYS, tn), F32),
            pltpu.VMEM((PEER_HEADS, PEER_NKEYS, tn), BF16),
            pltpu.VMEM((PEER_HEADS, PEER_NKEYS, tn), BF16),
            pltpu.VMEM((2, PEER_NKEYS, tn), F32),
            pltpu.VMEM((te, tn), F32),
            pltpu.VMEM((te, tn), F32),
            pltpu.VMEM((2 * te, tn), BF16),
        ],
        compiler_params=_params("parallel", "arbitrary"),
        name="peer",
    )(x, wqt, k1, k2, _peer_cand_index(), u, vt, g, b)


GLA_PAD = GLA_HEADS * LANES


def _even_in_kernel(x_ref, wq_ref, wk_ref, wv_ref, wr_ref, wxb_ref, wgb_ref, wlr_ref, wg_ref, bg_ref,
                    qkl_ref, v_ref, rs_ref, xb_ref, gg_ref):
    xb16 = x_ref[...].astype(BF16)
    qkl_ref[:, 0:GLA_PAD] = _dot(xb16, wq_ref[...]) * (GLA_DK ** -0.5)
    qkl_ref[:, GLA_PAD:2 * GLA_PAD] = _dot(xb16, wk_ref[...])
    glr = _dot(xb16, wlr_ref[...])
    z = _dot(glr, wg_ref[...]) + bg_ref[...]
    qkl_ref[:, 2 * GLA_PAD:3 * GLA_PAD] = -_softplus(-z) * (1.0 / GLA_GATE_NORM)
    v_ref[...] = _dot(xb16, wv_ref[...])
    rs_ref[...] = _silu(_dot(xb16, wr_ref[...]))
    xb_ref[...] = _dot(xb16, wxb_ref[...])
    gg_ref[...] = _gelu_tanh(_dot(xb16, wgb_ref[...]))


def _even_in(x, consts, *, tm=512):
    n, d = x.shape
    row = lambda w: pl.BlockSpec((tm, w), lambda i: (i, 0))
    return pl.pallas_call(
        _even_in_kernel,
        out_shape=[jax.ShapeDtypeStruct((n, 3 * GLA_PAD), F32)] + [jax.ShapeDtypeStruct((n, LRU_WIDTH), F32)] * 4,
        grid=(n // tm,),
        in_specs=[row(d)] + [_const_spec(c.shape) for c in consts],
        out_specs=[row(3 * GLA_PAD)] + [row(LRU_WIDTH)] * 4,
        compiler_params=_params("parallel"),
        name="even_in",
    )(x, *consts)


GLA_SEQS_PER_STEP = 4
GLA_TIME_BLOCK = 3 * GLA_CHUNK


def _gla_kernel(*refs, tb, ns):
    qkl_refs, v_refs, rs_refs, s0_refs = (refs[i * ns:(i + 1) * ns] for i in range(4))
    tri_ref, gn_ref = refs[4 * ns:4 * ns + 2]
    o_refs = refs[4 * ns + 2:5 * ns + 2]
    sout_refs = refs[5 * ns + 2:6 * ns + 2]
    st_ref = refs[6 * ns + 2]
    t = pl.program_id(1)

    @pl.when(t == 0)
    def _init():
        for p in range(ns):
            st_ref[p] = s0_refs[p][0]

    tri = tri_ref[...]
    causal = tri > 0
    gn = gn_ref[...]

    def chunk(c, carry):
        rows = pl.ds(pl.multiple_of(c * GLA_CHUNK, GLA_CHUNK), GLA_CHUNK)
        seqs = range(ns)
        pairs = [(p, h) for p in seqs for h in range(GLA_HEADS)]
        sl = lambda h: slice(h * LANES, (h + 1) * LANES)
        las = [qkl_refs[p][rows, 2 * GLA_PAD:3 * GLA_PAD] for p in seqs]
        his = [la.astype(BF16) for la in las]
        r1s = [la - hi.astype(F32) for la, hi in zip(las, his)]
        mids = [r1.astype(BF16) for r1 in r1s]
        los = [(r1 - mid.astype(F32)).astype(BF16) for r1, mid in zip(r1s, mids)]
        parts = [[jnp.dot(tri, part, preferred_element_type=F32) for part in (hi, mid, lo)]
                 for hi, mid, lo in zip(his, mids, los)]
        cums = [a + b + c3 for a, b, c3 in parts]
        lasts = [cum[GLA_CHUNK - 1:GLA_CHUNK] for cum in cums]
        qes = [qkl_refs[p][rows, 0:GLA_PAD] * jnp.exp(cums[p]) for p in seqs]
        ks = [qkl_refs[p][rows, GLA_PAD:2 * GLA_PAD] for p in seqs]
        kes = [ks[p] * jnp.exp(-cums[p]) for p in seqs]
        kls = [ks[p] * jnp.exp(lasts[p] - cums[p]) for p in seqs]
        decs = [jnp.exp(last) for last in lasts]
        vs = [v_refs[p][rows, :] for p in seqs]
        sts = [st_ref[p, h] for p, h in pairs]
        atts = [_dot_nt(qes[p][:, sl(h)], kes[p][:, sl(h)]) for p, h in pairs]
        inters = [_dot_nt(qes[p][:, sl(h)], st) for (p, h), st in zip(pairs, sts)]
        upds = [_dot_tn(vs[p][:, sl(h)], kls[p][:, sl(h)]) for p, h in pairs]
        intras = [_dot(jnp.where(causal, att, 0.0), vs[p][:, sl(h)]) for (p, h), att in zip(pairs, atts)]
        for i, (p, h) in enumerate(pairs):
            st_ref[p, h] = sts[i] * decs[p][:, sl(h)] + upds[i]
            o = intras[i] + inters[i]
            on = o * lax.rsqrt(jnp.mean(o * o, axis=-1, keepdims=True) + LN_EPS) * gn
            o_refs[p][rows, sl(h)] = on * rs_refs[p][rows, sl(h)]
        return carry

    lax.fori_loop(0, tb // GLA_CHUNK, chunk, 0)

    @pl.when(t == pl.num_programs(1) - 1)
    def _fin():
        for p in range(ns):
            sout_refs[p][0] = st_ref[p]


def _gla(qkl, v, rs, s0t, gn, *, nb, t, tb):
    nt = t // tb
    ns = GLA_SEQS_PER_STEP
    assert nb % ns == 0
    per = nb // ns
    w = GLA_HEADS * GLA_DV
    tri = jnp.tril(jnp.ones((GLA_CHUNK, GLA_CHUNK), BF16))
    row_in = lambda width, p: pl.BlockSpec((tb, width), lambda g, i: ((p * per + g) * nt + i, 0))
    st_in = lambda p: pl.BlockSpec((1, GLA_HEADS, GLA_DV, LANES), lambda g, i: (p * per + g, 0, 0, 0))
    row_out = pl.BlockSpec((tb, w), lambda g, i: (g * nt + i, 0))
    st_out = pl.BlockSpec((1, GLA_HEADS, GLA_DV, LANES), lambda g, i: (g, 0, 0, 0))
    slots = range(ns)
    outs = pl.pallas_call(
        functools.partial(_gla_kernel, tb=tb, ns=ns),
        out_shape=[jax.ShapeDtypeStruct((per * t, w), F32)] * ns
        + [jax.ShapeDtypeStruct((per,) + s0t.shape[1:], F32)] * ns,
        grid=(per, nt),
        in_specs=[row_in(3 * GLA_PAD, p) for p in slots] + [row_in(w, p) for p in slots] + [row_in(w, p) for p in slots]
        + [st_in(p) for p in slots] + [_const_spec(tri.shape), _const_spec(gn.shape)],
        out_specs=[row_out] * ns + [st_out] * ns,
        scratch_shapes=[pltpu.VMEM((ns, GLA_HEADS, GLA_DV, LANES), F32)],
        compiler_params=_params("parallel", "arbitrary"),
        name="gla",
    )(*([qkl] * ns + [v] * ns + [rs] * ns + [s0t] * ns + [tri, gn]))
    return jnp.concatenate(outs[:ns], axis=0), jnp.concatenate(outs[ns:], axis=0)


def _lru_kernel(xb_ref, gg_ref, buf0_ref, h0_ref, cw_ref, cb_ref, wa_ref, ba_ref, wx_ref, bx_ref, lam_ref,
                y_ref, hlast_ref, cbuf_ref, xs_ref, a_ref, b_ref, hs_ref, hcar_ref, *, tb, t_last):
    t = pl.program_id(1)
    halo = SUBLANES

    @pl.when(t == 0)
    def _init():
        xs_ref[0:halo] = buf0_ref[0]
        hcar_ref[...] = h0_ref[0]

    xs_ref[halo:halo + tb] = xb_ref[...]
    xc = cb_ref[...] + sum(xs_ref[pl.ds(halo - (CONV_W - 1) + i, tb), :] * cw_ref[i:i + 1, :] for i in range(CONV_W))
    ga = _sigmoid(_dot(xc, wa_ref[...]) + ba_ref[...])
    gx = _sigmoid(_dot(xc, wx_ref[...]) + bx_ref[...])
    log_at = ga * (-LRU_C * _softplus(-lam_ref[...]))
    a = jnp.exp(log_at)
    a_ref[...] = a
    b_ref[...] = jnp.sqrt(-jnp.tanh(log_at) * (a * a + 1.0)) * (gx * xc)

    def step(i, h):
        h = a_ref[pl.ds(i, 1), :] * h + b_ref[pl.ds(i, 1), :]
        hs_ref[pl.ds(i, 1), :] = h
        return h

    hcar_ref[...] = lax.fori_loop(0, tb, step, hcar_ref[...], unroll=8)
    y_ref[...] = hs_ref[...] * gg_ref[...]

    @pl.when(t == pl.num_programs(1) - 1)
    def _fin():
        hlast_ref[0] = hs_ref[t_last:t_last + 1, :]
        cbuf_ref[0] = xs_ref[halo + t_last - (CONV_W - 2):halo + t_last + 1, :]

    xs_ref[0:halo] = xs_ref[tb:tb + halo]


def _lru(xb, gg, buf0, h0, consts, *, nb, t, tb, t_last):
    nt = t // tb
    w = LRU_WIDTH
    row = pl.BlockSpec((tb, w), lambda b, i: (b * nt + i, 0))
    per_b = lambda r: pl.BlockSpec((1, r, w), lambda b, i: (b, 0, 0))
    return pl.pallas_call(
        functools.partial(_lru_kernel, tb=tb, t_last=t_last),
        out_shape=[jax.ShapeDtypeStruct((nb * t, w), F32), jax.ShapeDtypeStruct((nb, 1, w), F32),
                   jax.ShapeDtypeStruct((nb, CONV_W - 1, w), F32)],
        grid=(nb, nt),
        in_specs=[row, row, per_b(SUBLANES), per_b(1)] + [_const_spec(c.shape) for c in consts],
        out_specs=[row, per_b(1), per_b(CONV_W - 1)],
        scratch_shapes=[pltpu.VMEM((tb + 2 * SUBLANES, w), F32), pltpu.VMEM((tb, w), F32), pltpu.VMEM((tb, w), F32),
                        pltpu.VMEM((tb, w), F32), pltpu.VMEM((1, w), F32)],
        compiler_params=_params("parallel", "arbitrary"),
        name="lru",
    )(xb, gg, buf0, h0, *consts)


def _mix_out_kernel(a_ref, b_ref, x_ref, wa_ref, wb_ref, g_ref, beta_ref, o_ref):
    m = _dot(a_ref[...], wa_ref[...]) + _dot(b_ref[...], wb_ref[...])
    o_ref[...] = _layer_norm_rows(DN_ALPHA * x_ref[...] + m, g_ref[...], beta_ref[...])


def _mix_out(a, b, x, wa, wb, g, beta, *, tm=512):
    n, d = x.shape
    row = lambda w: pl.BlockSpec((tm, w), lambda i: (i, 0))
    consts = [wa, wb, g, beta]
    return pl.pallas_call(
        _mix_out_kernel,
        out_shape=jax.ShapeDtypeStruct((n, d), F32),
        grid=(n // tm,),
        in_specs=[row(a.shape[1]), row(b.shape[1]), row(d)] + [_const_spec(c.shape) for c in consts],
        out_specs=row(d),
        compiler_params=_params("parallel"),
        name="mix_out",
    )(a, b, x, *consts)


def _pad_heads(w, heads, width):
    lead = w.shape[:-1]
    w = w.reshape(lead + (heads, width))
    w = jnp.pad(w, [(0, 0)] * len(lead) + [(0, 0), (0, LANES - width)])
    return w.reshape(lead + (heads * LANES,))


def _even_weights(w_in, w_gate, b_gate, gla_norm, conv_w, conv_b, lru_wa, lru_ba, lru_wx, lru_bx, lru_lambda, w_out):
    gk = GLA_HEADS * GLA_DK
    gv = GLA_HEADS * GLA_DV
    o = 0
    cols = {}
    for name, width in (("q", gk), ("k", gk), ("v", gv), ("lr", GLA_RANK), ("r", gv), ("xb", LRU_WIDTH), ("gb", LRU_WIDTH)):
        cols[name] = w_in[:, o:o + width]
        o += width
    wlr = jnp.pad(cols["lr"], ((0, 0), (0, LANES - GLA_RANK)))
    wg = jnp.pad(_pad_heads(w_gate, GLA_HEADS, GLA_DK), ((0, LANES - GLA_RANK), (0, 0)))
    bg = _pad_heads(b_gate[None], GLA_HEADS, GLA_DK)
    in_consts = [_pad_heads(cols["q"], GLA_HEADS, GLA_DK).astype(BF16), _pad_heads(cols["k"], GLA_HEADS, GLA_DK).astype(BF16),
                 cols["v"].astype(BF16), cols["r"].astype(BF16), cols["xb"].astype(BF16), cols["gb"].astype(BF16),
                 wlr.astype(BF16), wg.astype(BF16), bg]
    eye = jnp.eye(LRU_BLOCKS, dtype=F32)
    bd = lambda w: (eye[:, None, :, None] * w[:, :, None, :]).reshape(LRU_WIDTH, LRU_WIDTH).astype(BF16)
    lru_consts = [conv_w, conv_b[None], bd(lru_wa), lru_ba[None], bd(lru_wx), lru_bx[None], lru_lambda[None]]
    return in_consts, gla_norm[None], lru_consts, w_out[:gv].astype(BF16), w_out[gv:].astype(BF16)


def _seq_rows(a, off, nb, t, t_pad):
    a = a[off:off + nb * t].reshape(nb, t, a.shape[1])
    return jnp.pad(a, ((0, 0), (0, t_pad - t), (0, 0))).reshape(nb * t_pad, a.shape[2])


def _unpad_rows(a, nb, t, t_pad):
    return a.reshape(nb, t_pad, a.shape[1])[:, :t].reshape(nb * t, a.shape[1])


def _even_layer(x, groups, weights, g, beta, *, tm):
    in_consts, gn, lru_consts, wo_a, wo_b = weights
    qkl, v, rs, xb, gg = _even_in(x, in_consts, tm=tm)
    ogs, yls, states = [], [], []
    for off, nb, t, tb, s_gla, h_lru, cbuf in groups:
        s0t = jnp.pad(jnp.swapaxes(s_gla, 2, 3), ((0, 0), (0, 0), (0, 0), (0, LANES - GLA_DK)))
        buf0 = jnp.pad(cbuf, ((0, 0), (SUBLANES - (CONV_W - 1), 0), (0, 0)))
        if tb is None:
            tg = GLA_CHUNK
            tl = SUBLANES
            og, st = _gla(_seq_rows(qkl, off, nb, t, tg), _seq_rows(v, off, nb, t, tg), _seq_rows(rs, off, nb, t, tg),
                          s0t, gn, nb=nb, t=tg, tb=tg)
            yl, hl, cb = _lru(_seq_rows(xb, off, nb, t, tl), _seq_rows(gg, off, nb, t, tl), buf0, h_lru[:, None],
                              lru_consts, nb=nb, t=tl, tb=tl, t_last=t - 1)
            og = _unpad_rows(og, nb, t, tg)
            yl = _unpad_rows(yl, nb, t, tl)
        else:
            og, st = _gla(qkl, v, rs, s0t, gn, nb=nb, t=t, tb=GLA_TIME_BLOCK)
            yl, hl, cb = _lru(xb, gg, buf0, h_lru[:, None], lru_consts, nb=nb, t=t, tb=tb, t_last=tb - 1)
        ogs.append(og)
        yls.append(yl)
        states.append((jnp.swapaxes(st[..., :GLA_DK], 2, 3), hl[:, 0], cb))
    n_used = sum(o.shape[0] for o in ogs)
    tail = jnp.zeros((x.shape[0] - n_used, LRU_WIDTH), F32)
    og = jnp.concatenate(ogs + [tail], axis=0)
    yl = jnp.concatenate(yls + [tail], axis=0)
    return _mix_out(og, yl, x, wo_a, wo_b, g, beta, tm=tm), states


RWKV_STREAMS = 5


def _rwkv_in_kernel(*refs, has_vres):
    if has_vres:
        (x_ref, xp_ref, mu_ref, wr_ref, wk_ref, wv_ref, w1_ref, w2_ref, a1_ref, a2_ref, g1_ref, g2_ref, w0_ref, a0_ref,
         vf_ref, v1_ref, v2_ref, v0_ref, p_ref, g_ref) = refs
    else:
        (x_ref, xp_ref, mu_ref, wr_ref, wk_ref, wv_ref, w1_ref, w2_ref, a1_ref, a2_ref, g1_ref, g2_ref, w0_ref, a0_ref,
         p_ref, g_ref) = refs
    d = D_MODEL
    x = x_ref[...]
    xx = xp_ref[...] - x
    mix = lambda i: (x + xx * mu_ref[i:i + 1, :]).astype(BF16)
    xr, xw, xk, xv, xa, xg = (mix(i) for i in range(6))
    p_ref[:, 0:d] = _dot(xr, wr_ref[...])
    p_ref[:, d:2 * d] = _dot(xk, wk_ref[...])
    v = _dot(xv, wv_ref[...])
    if has_vres:
        v = v + (vf_ref[...] - v) * _sigmoid(v0_ref[...] + _dot(_dot(xv, v1_ref[...]), v2_ref[...]))
    p_ref[:, 2 * d:3 * d] = v
    p_ref[:, 3 * d:4 * d] = w0_ref[...] + _dot(jnp.tanh(_dot(xw, w1_ref[...])), w2_ref[...])
    p_ref[:, 4 * d:5 * d] = a0_ref[...] + _dot(_dot(xa, a1_ref[...]), a2_ref[...])
    g_ref[...] = _dot(_sigmoid(_dot(xg, g1_ref[...])), g2_ref[...])


def _rwkv_in(x, xprev, consts, vres, *, tm):
    n, d = x.shape
    row = lambda w: pl.BlockSpec((tm, w), lambda i: (i, 0))
    args = [x, xprev] + list(consts)
    specs = [row(d), row(d)] + [_const_spec(c.shape) for c in consts]
    if vres is not None:
        vf, vconsts = vres
        args += [vf] + list(vconsts)
        specs += [row(d)] + [_const_spec(c.shape) for c in vconsts]
    return pl.pallas_call(
        functools.partial(_rwkv_in_kernel, has_vres=vres is not None),
        out_shape=[jax.ShapeDtypeStruct((n, RWKV_STREAMS * d), F32), jax.ShapeDtypeStruct((n, d), F32)],
        grid=(n // tm,),
        in_specs=specs,
        out_specs=[row(RWKV_STREAMS * d), row(d)],
        compiler_params=_params("parallel"),
        name="rwkv_in",
    )(*args)


def _wkv_kernel(p_ref, s0_ref, kk_ref, ka_ref, rk_ref, lg_ref, lb_ref, y_ref, sout_ref,
                s_ref, dec_ref, a_ref, b_ref, km_ref, bon_ref, *, tt):
    t = pl.program_id(1)
    hk = RWKV_HEAD

    @pl.when(t == 0)
    def _init():
        s_ref[...] = s0_ref[...]

    r = p_ref[:, 0]
    k = p_ref[:, 1]
    v = p_ref[:, 2]
    dec_ref[...] = jnp.exp(-jnp.exp(-_softplus(-p_ref[:, 3]) - 0.5))
    ag = _sigmoid(p_ref[:, 4])
    kk = k * kk_ref[...][None]
    kk = kk / jnp.maximum(jnp.sqrt(jnp.sum(kk * kk, axis=1, keepdims=True)), 1e-12)
    km = k * (1.0 + (ag - 1.0) * ka_ref[...][None])
    km_ref[...] = km
    a_ref[...] = -kk
    b_ref[...] = kk * ag
    bon_ref[...] = jnp.sum(r * km * rk_ref[...][None], axis=1, keepdims=True) * v
    lg = lg_ref[...]
    lb = lb_ref[...]

    def step(i, carry):
        sa = jnp.zeros((hk, LANES), F32)
        for q in range(hk):
            sa = sa + s_ref[q] * a_ref[i, q:q + 1, :]
        vv = p_ref[i, 2]
        y = jnp.zeros((hk, LANES), F32)
        for q in range(hk):
            sn = s_ref[q] * dec_ref[i, q:q + 1, :] + sa * b_ref[i, q:q + 1, :] + vv * km_ref[i, q:q + 1, :]
            s_ref[q] = sn
            y = y + sn * p_ref[i, 0, q:q + 1, :]
        mu = jnp.mean(y, axis=0, keepdims=True)
        dlt = y - mu
        var = jnp.mean(dlt * dlt, axis=0, keepdims=True)
        y_ref[i] = dlt * lax.rsqrt(var + RWKV_GN_EPS) * lg + lb + bon_ref[i]
        return carry

    lax.fori_loop(0, tt, step, 0)

    @pl.when(t == pl.num_programs(1) - 1)
    def _fin():
        sout_ref[...] = s_ref[...]


def _wkv(p, s0, lane_consts, *, tt):
    t, _, hk, l = p.shape
    lane2 = pl.BlockSpec((hk, LANES), lambda g, i: (0, g))
    st = pl.BlockSpec((hk, hk, LANES), lambda g, i: (0, 0, g))
    seq = lambda: pltpu.VMEM((tt, hk, LANES), F32)
    return pl.pallas_call(
        functools.partial(_wkv_kernel, tt=tt),
        out_shape=[jax.ShapeDtypeStruct((t, hk, l), F32), jax.ShapeDtypeStruct((hk, hk, l), F32)],
        grid=(l // LANES, t // tt),
        in_specs=[pl.BlockSpec((tt, RWKV_STREAMS, hk, LANES), lambda g, i: (i, 0, 0, g)), st] + [lane2] * 5,
        out_specs=[pl.BlockSpec((tt, hk, LANES), lambda g, i: (i, 0, g)), st],
        scratch_shapes=[pltpu.VMEM((hk, hk, LANES), F32), seq(), seq(), seq(), seq(), seq()],
        compiler_params=_params("parallel", "arbitrary"),
        name="wkv",
    )(p, s0, *lane_consts)


def _gate_out_kernel(y_ref, g_ref, x_ref, w_ref, lg_ref, lb_ref, o_ref):
    m = _dot(y_ref[...] * g_ref[...], w_ref[...])
    o_ref[...] = _layer_norm_rows(DN_ALPHA * x_ref[...] + m, lg_ref[...], lb_ref[...])


def _gate_out(y, g, x, w, lg, lb, *, tm):
    n, d = x.shape
    row = pl.BlockSpec((tm, d), lambda i: (i, 0))
    consts = [w, lg, lb]
    return pl.pallas_call(
        _gate_out_kernel,
        out_shape=jax.ShapeDtypeStruct((n, d), F32),
        grid=(n // tm,),
        in_specs=[row, row, row] + [_const_spec(c.shape) for c in consts],
        out_specs=row,
        compiler_params=_params("parallel"),
        name="gate_out",
    )(y, g, x, *consts)


def _pad_cols(w):
    return jnp.pad(w, ((0, 0), (0, LANES - w.shape[1]))).astype(BF16)


def _pad_rows(w):
    return jnp.pad(w, ((0, LANES - w.shape[0]), (0, 0))).astype(BF16)


def _rwkv_weights(mu, w_r, w_k, w_v, w_o, w0, w1, w2, a0, a1, a2, g1, g2, k_k, k_a, r_k, lnx_g, lnx_b, vres):
    consts = [jnp.pad(mu, ((0, SUBLANES - mu.shape[0]), (0, 0))), w_r.astype(BF16), w_k.astype(BF16), w_v.astype(BF16),
              _pad_cols(w1), _pad_rows(w2), _pad_cols(a1), _pad_rows(a2), _pad_cols(g1), _pad_rows(g2), w0[None], a0[None]]
    vconsts = None if vres is None else [_pad_cols(vres[1]), _pad_rows(vres[2]), vres[0][None]]
    per_head = [k_k.reshape(RWKV_HEADS, RWKV_HEAD).T, k_a.reshape(RWKV_HEADS, RWKV_HEAD).T, r_k.T,
                lnx_g.reshape(RWKV_HEADS, RWKV_HEAD).T, lnx_b.reshape(RWKV_HEADS, RWKV_HEAD).T]
    return consts, vconsts, per_head, w_o.astype(BF16)


def _rwkv_layer(x, groups, weights, v_first, g, beta, *, tm):
    consts, vconsts, per_head, w_o = weights
    n, d = x.shape
    prevs = []
    for off, nb, t, tt, s, shift in groups:
        xg = x[off:off + nb * t].reshape(nb, t, d)
        prevs.append(jnp.concatenate([shift[:, None], xg[:, :-1]], axis=1).reshape(nb * t, d))
    n_used = sum(p.shape[0] for p in prevs)
    xprev = jnp.concatenate(prevs + [jnp.zeros((n - n_used, d), F32)], axis=0)
    vres = None if vconsts is None else (v_first, vconsts)
    p, gate = _rwkv_in(x, xprev, consts, vres, tm=tm)
    ys, states = [], []
    for off, nb, t, tt, s, shift in groups:
        lanes = nb * RWKV_HEADS
        pg = p[off:off + nb * t].reshape(nb, t, RWKV_STREAMS, RWKV_HEADS, RWKV_HEAD)
        pg = jnp.transpose(pg, (1, 2, 4, 0, 3)).reshape(t, RWKV_STREAMS, RWKV_HEAD, lanes)
        s0 = jnp.transpose(s, (3, 2, 0, 1)).reshape(RWKV_HEAD, RWKV_HEAD, lanes)
        lane_consts = [jnp.tile(c, (1, nb)) for c in per_head]
        y, s_out = _wkv(pg, s0, lane_consts, tt=tt)
        ys.append(jnp.transpose(y.reshape(t, RWKV_HEAD, nb, RWKV_HEADS), (2, 0, 3, 1)).reshape(nb * t, d))
        s_new = jnp.transpose(s_out.reshape(RWKV_HEAD, RWKV_HEAD, nb, RWKV_HEADS), (2, 3, 1, 0))
        states.append((s_new, x[off:off + nb * t].reshape(nb, t, d)[:, -1]))
    y = jnp.concatenate(ys + [jnp.zeros((n - n_used, d), F32)], axis=0)
    return _gate_out(y, gate, x, w_o, g, beta, tm=tm), states, p[:, 2 * d:3 * d]


TOKEN_TILE = 512
PEER_TOKEN_TILE = 1024
PROMPT_TIME_BLOCK = 688
WKV_TIME_BLOCK = 16


def kernel(x_prompt, x_sample, state_gla, state_lru_h, state_lru_conv, state_rwkv, state_rwkv_shift, meta_tokens, ln_g, ln_b, ev_w_in, ev_gla_w_gate, ev_gla_b_gate, ev_gla_norm, ev_conv_w, ev_conv_b, ev_lru_wa, ev_lru_ba, ev_lru_wx, ev_lru_bx, ev_lru_lambda, ev_w_out, od_mu, od_w_r, od_w_k, od_w_v, od_w_o, od_w0, od_w1, od_w2, od_a0, od_a1, od_a2, od_v0, od_v1, od_v2, od_g1, od_g2, od_k_k, od_k_a, od_r_k, od_lnx_g, od_lnx_b, peer_w_q, peer_keys, peer_u, peer_v):
    bp, sp, d = x_prompt.shape
    bs, ss, _ = x_sample.shape
    tp = sp + N_META
    n_p, n_s = bp * tp, bs * ss
    n = -(-(n_p + n_s) // PEER_TOKEN_TILE) * PEER_TOKEN_TILE
    assert tp % PROMPT_TIME_BLOCK == 0 and tp % GLA_TIME_BLOCK == 0 and tp % WKV_TIME_BLOCK == 0 and n % TOKEN_TILE == 0

    xp = jnp.concatenate([jnp.broadcast_to(meta_tokens[None], (bp, N_META, d)), x_prompt], axis=1)
    x = jnp.concatenate([xp.reshape(n_p, d), x_sample.reshape(n_s, d), jnp.zeros((n - n_p - n_s, d), F32)], axis=0)

    n_pairs = DEPTH // 2
    zeros = lambda *s: jnp.zeros(s, F32)
    outs = {k: [] for k in ("p_gla", "p_h", "p_conv", "p_rwkv", "p_shift", "s_gla", "s_h", "s_conv", "s_rwkv", "s_shift")}
    v_first = None
    for layer in range(DEPTH):
        j = layer // 2
        g, beta = ln_g[layer, 0][None], ln_b[layer, 0][None]
        if layer % 2 == 0:
            weights = _even_weights(ev_w_in[j], ev_gla_w_gate[j], ev_gla_b_gate[j], ev_gla_norm[j], ev_conv_w[j], ev_conv_b[j],
                                    ev_lru_wa[j], ev_lru_ba[j], ev_lru_wx[j], ev_lru_bx[j], ev_lru_lambda[j], ev_w_out[j])
            groups = [(0, bp, tp, PROMPT_TIME_BLOCK, zeros(bp, GLA_HEADS, GLA_DK, GLA_DV), zeros(bp, LRU_WIDTH),
                       zeros(bp, CONV_W - 1, LRU_WIDTH)),
                      (n_p, bs, ss, None, state_gla[j], state_lru_h[j], state_lru_conv[j])]
            x, states = _even_layer(x, groups, weights, g, beta, tm=TOKEN_TILE)
            for pre, st in zip("ps", states):
                outs[pre + "_gla"].append(st[0])
                outs[pre + "_h"].append(st[1])
                outs[pre + "_conv"].append(st[2])
        else:
            vres = None if j == 0 else (od_v0[j - 1], od_v1[j - 1], od_v2[j - 1])
            weights = _rwkv_weights(od_mu[j], od_w_r[j], od_w_k[j], od_w_v[j], od_w_o[j], od_w0[j], od_w1[j], od_w2[j],
                                    od_a0[j], od_a1[j], od_a2[j], od_g1[j], od_g2[j], od_k_k[j], od_k_a[j], od_r_k[j],
                                    od_lnx_g[j], od_lnx_b[j], vres)
            groups = [(0, bp, tp, WKV_TIME_BLOCK, zeros(bp, RWKV_HEADS, RWKV_HEAD, RWKV_HEAD), zeros(bp, d)),
                      (n_p, bs, ss, ss, state_rwkv[j], state_rwkv_shift[j])]
            x, states, v = _rwkv_layer(x, groups, weights, v_first, g, beta, tm=TOKEN_TILE)
            if v_first is None:
                v_first = v
            for pre, st in zip("ps", states):
                outs[pre + "_rwkv"].append(st[0])
                outs[pre + "_shift"].append(st[1])
        x = _peer_layer(x, peer_w_q[layer].T.astype(BF16), peer_keys[layer, :, 0].astype(BF16),
                        peer_keys[layer, :, 1].astype(BF16), peer_u[layer].astype(BF16), peer_v[layer].astype(BF16),
                        ln_g[layer, 1][None], ln_b[layer, 1][None], tn=PEER_TOKEN_TILE)

    y_prompt = x[:n_p].reshape(bp, tp, d)[:, N_META:]
    y_sample = x[n_p:n_p + n_s].reshape(bs, ss, d)
    st = {k: jnp.stack(v) for k, v in outs.items()}
    return (y_prompt, y_sample, st["p_gla"], st["p_h"], st["p_conv"], st["p_rwkv"], st["p_shift"],
            st["s_gla"], st["s_h"], st["s_conv"], st["s_rwkv"], st["s_shift"])
```

```python
import functools

import jax
import jax.numpy as jnp
from jax import lax
from jax.experimental import pallas as pl
from jax.experimental.pallas import tpu as pltpu

F32 = jnp.float32
BF16 = jnp.bfloat16

D_MODEL = 1024
DEPTH = 4
N_META = 16
GLA_HEADS = 4
GLA_DK = 64
GLA_DV = 128
GLA_RANK = 16
GLA_GATE_NORM = 16.0
GLA_CHUNK = 16
LRU_WIDTH = 512
LRU_BLOCKS = 8
CONV_W = 4
LRU_C = 8.0
RWKV_HEAD = 64
RWKV_HEADS = D_MODEL // RWKV_HEAD
RWKV_GN_EPS = 64e-5
PEER_HEADS = 8
PEER_NKEYS = 128
PEER_DKEY = 256
PEER_HALF = PEER_DKEY // 2
PEER_TOPK = 16
DN_ALPHA = float((2 * DEPTH) ** 0.25)
LN_EPS = 1e-5

LANES = 128
SUBLANES = 8
VMEM_LIMIT = 56 * 1024 * 1024


def _dot(a, b):
    return jnp.dot(a.astype(BF16), b.astype(BF16), preferred_element_type=F32)


def _dot_nt(a, b):
    return lax.dot_general(a.astype(BF16), b.astype(BF16), (((1,), (1,)), ((), ())), preferred_element_type=F32)


def _dot_tn(a, b):
    return lax.dot_general(a.astype(BF16), b.astype(BF16), (((0,), (0,)), ((), ())), preferred_element_type=F32)


def _layer_norm_rows(z, g, b):
    mu = jnp.mean(z, axis=-1, keepdims=True)
    d = z - mu
    var = jnp.mean(d * d, axis=-1, keepdims=True)
    return d * lax.rsqrt(var + LN_EPS) * g + b


def _sigmoid(x):
    return 1.0 / (1.0 + jnp.exp(-x))


def _softplus(x):
    return jnp.maximum(x, 0.0) + jnp.log1p(jnp.exp(-jnp.abs(x)))


def _gelu_tanh(x):
    c0 = 0.7978845608028654
    c1 = c0 * 0.044715
    one = jnp.asarray(1.0, x.dtype)
    return (0.5 * x) * (one + jnp.tanh(x * (c0 + c1 * (x * x))))


def _silu(x):
    return x * _sigmoid(x)


def _params(*sem):
    return pltpu.CompilerParams(dimension_semantics=sem, vmem_limit_bytes=VMEM_LIMIT)


def _const_spec(shape):
    nd = len(shape)
    return pl.BlockSpec(shape, lambda *_: (0,) * nd)


PEER_CAND_ROWS = 80
PEER_PACK = 2 * SUBLANES
PEER_SELECT_LANES = 2 * LANES


def _peer_cand_index():
    rows = [r1 * PEER_TOPK for r1 in range(16)]
    for j in range(1, 8):
        rows += [r1 * PEER_TOPK + j for r1 in range(8)]
    rows += list(range(8, 16))
    return jnp.broadcast_to(jnp.asarray(rows, F32)[:, None], (PEER_CAND_ROWS, PEER_SELECT_LANES))


PEER_CODE_STEP = 2.0 ** 116
PEER_CODED_BELOW = -(2.0 ** 119)


def _rank_code(r):
    return -PEER_CODE_STEP * (PEER_TOPK + r)


def _decode_rank(s):
    return jnp.where(s < PEER_CODED_BELOW, s * (-1.0 / PEER_CODE_STEP) - PEER_TOPK, float(PEER_TOPK))


def _count_coded(s):
    return jnp.sum(jnp.where(s < PEER_CODED_BELOW, 1.0, 0.0), axis=0, keepdims=True)


def _top16(s, key_iota):
    row16 = lax.broadcasted_iota(jnp.int32, (PEER_TOPK, s.shape[1]), 0)
    vals = jnp.zeros((PEER_TOPK, s.shape[1]), F32)
    for r in range(PEER_TOPK):
        m = jnp.max(s, axis=0, keepdims=True)
        hit = s == m
        if key_iota is not None:
            first = jnp.min(jnp.where(hit, key_iota, float(PEER_NKEYS)), axis=0, keepdims=True)
            hit = key_iota == first
        s = jnp.where(hit, _rank_code(r), s)
        vals = jnp.where(row16 == r, m, vals)
    return vals, s


def _peer_candidates(a, b):
    blocks = [a + b[0:1]]
    for j in range(1, 8):
        blocks.append(a[0:8] + b[j:j + 1])
    blocks.append(a[0:1] + b[8:16])
    return jnp.concatenate(blocks, axis=0)


def _select16(c, cidx):
    for _ in range(PEER_TOPK):
        m = jnp.max(c, axis=0, keepdims=True)
        hit = c == m
        if cidx is not None:
            first = jnp.min(jnp.where(hit, cidx, 1e9), axis=0, keepdims=True)
            hit = cidx == first
        c = jnp.where(hit, _rank_code(0), c)
    return c


def _peer_gate_tiles(s1, s2, a, b, s1c, s2c, c, cc):
    sel = jnp.where(cc < PEER_CODED_BELOW, 1.0, 0.0)
    z = jnp.sum(sel * jnp.exp(c - c[0:1]), axis=0, keepdims=True)
    j_lo = sel[0:8]
    for j in range(1, 8):
        j_lo = j_lo + sel[8 + 8 * j:16 + 8 * j]
    extra = jnp.sum(sel[72:80], axis=0, keepdims=True)
    row8 = lax.broadcasted_iota(jnp.int32, j_lo.shape, 0)
    j_lo = j_lo + jnp.where(row8 == 0, extra, 0.0)
    jt = jnp.concatenate([j_lo, sel[8:16]], axis=0)
    r1 = _decode_rank(s1c)
    h1 = jnp.zeros_like(r1)
    for r in range(PEER_TOPK):
        h1 = jnp.where(r1 == float(r), jt[r:r + 1], h1)
    return h1, jnp.exp(s1 - a[0:1]) / z, _decode_rank(s2c), jnp.exp(s2 - b[0:1])


def _peer_kernel(x_ref, wqt_ref, k1_ref, k2_ref, cidx_ref, u_ref, vt_ref, g_ref, b_ref, o_ref,
                 xt_ref, acc_ref, h1_ref, c1_ref, r2_ref, e2_ref, s_ref, hd0_ref, hd1_ref, coef_ref, *, tn, te):
    j = pl.program_id(1)
    nj = pl.num_programs(1)

    @pl.when(j == 0)
    def _select():
        xt_ref[...] = x_ref[...].T.astype(BF16)
        acc_ref[...] = jnp.zeros_like(acc_ref)
        hd1_ref[...] = jnp.zeros_like(hd1_ref)
        coef_ref[0:te] = jnp.zeros((te, tn), BF16)
        key_iota = lax.broadcasted_iota(jnp.int32, (PEER_NKEYS, PEER_SELECT_LANES), 0).astype(F32)

        def head(h, carry):
            q = _dot(wqt_ref[pl.ds(pl.multiple_of(h * PEER_DKEY, PEER_DKEY), PEER_DKEY), :], xt_ref[...])
            mu = jnp.mean(q, axis=0, keepdims=True)
            d = q - mu
            qn = d * lax.rsqrt(jnp.mean(d * d, axis=0, keepdims=True) + LN_EPS)
            s_ref[0] = _dot(k1_ref[h], qn[0:PEER_HALF])
            s_ref[1] = _dot(k2_ref[h], qn[PEER_HALF:PEER_DKEY])

            def chunk(ci, carry2):
                ls = pl.ds(pl.multiple_of(ci * PEER_SELECT_LANES, PEER_SELECT_LANES), PEER_SELECT_LANES)
                s1 = s_ref[0, :, ls]
                s2 = s_ref[1, :, ls]

                def put(tiles):
                    h1_ref[h, :, ls] = tiles[0]
                    c1_ref[h, :, ls] = tiles[1]
                    r2_ref[h, :, ls] = tiles[2].astype(BF16)
                    e2_ref[h, :, ls] = tiles[3].astype(BF16)

                a, s1c = _top16(s1, None)
                b, s2c = _top16(s2, None)
                c = _peer_candidates(a, b)
                cc = _select16(c, None)
                put(_peer_gate_tiles(s1, s2, a, b, s1c, s2c, c, cc))
                miscount = (jnp.abs(_count_coded(s1c) - PEER_TOPK) + jnp.abs(_count_coded(s2c) - PEER_TOPK)
                            + jnp.abs(_count_coded(cc) - PEER_TOPK))

                @pl.when(jnp.max(miscount) > 0.0)
                def _with_ties():
                    a, s1c = _top16(s1, key_iota)
                    b, s2c = _top16(s2, key_iota)
                    c = _peer_candidates(a, b)
                    cc = _select16(c, cidx_ref[...])
                    put(_peer_gate_tiles(s1, s2, a, b, s1c, s2c, c, cc))

                return carry2

            lax.fori_loop(0, tn // PEER_SELECT_LANES, chunk, 0)
            return carry

        lax.fori_loop(0, PEER_HEADS, head, 0)

    groups = PEER_NKEYS // PEER_PACK
    last_key = PEER_NKEYS - 1
    zero_pack = jnp.zeros((PEER_PACK, LANES), F32)

    def gate_rows(tile):
        rows = []
        for ii in range(te // PEER_NKEYS):
            i1 = jnp.clip(tile * (te // PEER_NKEYS) + ii, 0, last_key)
            rows.append(([h1_ref[h, pl.ds(i1, 1), :] for h in range(PEER_HEADS)],
                         [c1_ref[h, pl.ds(i1, 1), :] for h in range(PEER_HEADS)]))
        return rows

    def gate_stage(rows, hd_ref, coef_rows):
        nk = len(rows)
        for c in range(tn // LANES):
            ls = slice(c * LANES, (c + 1) * LANES)
            gates = [[jnp.zeros((PEER_PACK, LANES), BF16) for _ in range(groups)] for _ in range(nk)]
            for h in range(PEER_HEADS):
                bounds = [(rows[ii][0][h][:, ls] + zero_pack).astype(BF16) for ii in range(nk)]
                scales = [(rows[ii][1][h][:, ls] + zero_pack).astype(BF16) for ii in range(nk)]
                for gi in range(groups):
                    krows = slice(gi * PEER_PACK, (gi + 1) * PEER_PACK)
                    rank = r2_ref[h, krows, ls]
                    fac = e2_ref[h, krows, ls]
                    for ii in range(nk):
                        gates[ii][gi] = gates[ii][gi] + jnp.where(rank < bounds[ii], fac * scales[ii], jnp.zeros((), BF16))
            for ii in range(nk):
                for gi in range(groups):
                    erows = slice(ii * PEER_NKEYS + gi * PEER_PACK, ii * PEER_NKEYS + (gi + 1) * PEER_PACK)
                    crows = slice(coef_rows + erows.start, coef_rows + erows.stop)
                    coef_ref[crows, ls] = gates[ii][gi] * _gelu_tanh(hd_ref[erows, ls].astype(BF16))

    xt = xt_ref[...]
    hd0_ref[...] = _dot(u_ref[0:te, :], xt)
    gate_stage(gate_rows(2 * j - 1), hd1_ref, te)
    acc_ref[...] += jnp.dot(vt_ref[0], coef_ref[...], preferred_element_type=F32)
    hd1_ref[...] = _dot(u_ref[te:2 * te, :], xt)
    gate_stage(gate_rows(2 * j), hd0_ref, 0)

    @pl.when(j == nj - 1)
    def _finish():
        y = acc_ref[...].T
        o_ref[...] = _layer_norm_rows(DN_ALPHA * x_ref[...] + y, g_ref[...], b_ref[...])


def _peer_layer(x, wqt, k1, k2, u, v, g, b, *, tn=1024, te=256):
    n, d = x.shape
    ne = u.shape[0]
    assert ne % (2 * te) == 0 and n % tn == 0
    pairs = ne // (2 * te)
    steps = pairs + 1
    vt = jnp.swapaxes(v.reshape(pairs, 2 * te, d), 1, 2)
    kern = functools.partial(_peer_kernel, tn=tn, te=te)
    one = pl.Buffered(1)
    return pl.pallas_call(
        kern,
        out_shape=jax.ShapeDtypeStruct((n, d), F32),
        grid=(n // tn, steps),
        in_specs=[
            pl.BlockSpec((tn, d), lambda i, j: (i, 0), pipeline_mode=one),
            pl.BlockSpec(wqt.shape, lambda i, j: (0, 0), pipeline_mode=one),
            pl.BlockSpec(k1.shape, lambda i, j: (0, 0, 0), pipeline_mode=one),
            pl.BlockSpec(k2.shape, lambda i, j: (0, 0, 0), pipeline_mode=one),
            pl.BlockSpec((PEER_CAND_ROWS, PEER_SELECT_LANES), lambda i, j: (0, 0), pipeline_mode=one),
            pl.BlockSpec((2 * te, d), lambda i, j: (jnp.minimum(j, pairs - 1), 0)),
            pl.BlockSpec((1, d, 2 * te), lambda i, j: (jnp.maximum(j - 1, 0), 0, 0)),
            pl.BlockSpec((1, d), lambda i, j: (0, 0), pipeline_mode=one),
            pl.BlockSpec((1, d), lambda i, j: (0, 0), pipeline_mode=one),
        ],
        out_specs=pl.BlockSpec((tn, d), lambda i, j: (i, 0), pipeline_mode=one),
        scratch_shapes=[
            pltpu.VMEM((d, tn), BF16),
            pltpu.VMEM((d, tn), F32),
            pltpu.VMEM((PEER_HEADS, PEER_NKEYS, tn), F32),
            pltpu.VMEM((PEER_HEADS, PEER_NKEYS, tn), F32),
            pltpu.VMEM((PEER_HEADS, PEER_NKEYS, tn), BF16),
            pltpu.VMEM((PEER_HEADS, PEER_NKEYS, tn), BF16),
            pltpu.VMEM((2, PEER_NKEYS, tn), F32),
            pltpu.VMEM((te, tn), F32),
            pltpu.VMEM((te, tn), F32),
            pltpu.VMEM((2 * te, tn), BF16),
        ],
        compiler_params=_params("parallel", "arbitrary"),
        name="peer",
    )(x, wqt, k1, k2, _peer_cand_index(), u, vt, g, b)


GLA_PAD = GLA_HEADS * LANES


def _even_in_kernel(x_ref, wq_ref, wk_ref, wv_ref, wr_ref, wxb_ref, wgb_ref, wlr_ref, wg_ref, bg_ref,
                    qkl_ref, v_ref, rs_ref, xb_ref, gg_ref):
    xb16 = x_ref[...].astype(BF16)
    qkl_ref[:, 0:GLA_PAD] = _dot(xb16, wq_ref[...]) * (GLA_DK ** -0.5)
    qkl_ref[:, GLA_PAD:2 * GLA_PAD] = _dot(xb16, wk_ref[...])
    glr = _dot(xb16, wlr_ref[...])
    z = _dot(glr, wg_ref[...]) + bg_ref[...]
    qkl_ref[:, 2 * GLA_PAD:3 * GLA_PAD] = -_softplus(-z) * (1.0 / GLA_GATE_NORM)
    v_ref[...] = _dot(xb16, wv_ref[...])
    rs_ref[...] = _silu(_dot(xb16, wr_ref[...]))
    xb_ref[...] = _dot(xb16, wxb_ref[...])
    gg_ref[...] = _gelu_tanh(_dot(xb16, wgb_ref[...]))


def _even_in(x, consts, *, tm=512):
    n, d = x.shape
    row = lambda w: pl.BlockSpec((tm, w), lambda i: (i, 0))
    return pl.pallas_call(
        _even_in_kernel,
        out_shape=[jax.ShapeDtypeStruct((n, 3 * GLA_PAD), F32)] + [jax.ShapeDtypeStruct((n, LRU_WIDTH), F32)] * 4,
        grid=(n // tm,),
        in_specs=[row(d)] + [_const_spec(c.shape) for c in consts],
        out_specs=[row(3 * GLA_PAD)] + [row(LRU_WIDTH)] * 4,
        compiler_params=_params("parallel"),
        name="even_in",
    )(x, *consts)


GLA_SEQS_PER_STEP = 4
GLA_TIME_BLOCK = 3 * GLA_CHUNK


def _gla_kernel(*refs, tb, ns):
    qkl_refs, v_refs, rs_refs, s0_refs = (refs[i * ns:(i + 1) * ns] for i in range(4))
    tri_ref, gn_ref = refs[4 * ns:4 * ns + 2]
    o_refs = refs[4 * ns + 2:5 * ns + 2]
    sout_refs = refs[5 * ns + 2:6 * ns + 2]
    st_ref = refs[6 * ns + 2]
    t = pl.program_id(1)

    @pl.when(t == 0)
    def _init():
        for p in range(ns):
            st_ref[p] = s0_refs[p][0]

    tri = tri_ref[...]
    causal = tri > 0
    gn = gn_ref[...]

    def chunk(c, carry):
        rows = pl.ds(pl.multiple_of(c * GLA_CHUNK, GLA_CHUNK), GLA_CHUNK)
        seqs = range(ns)
        pairs = [(p, h) for p in seqs for h in range(GLA_HEADS)]
        sl = lambda h: slice(h * LANES, (h + 1) * LANES)
        las = [qkl_refs[p][rows, 2 * GLA_PAD:3 * GLA_PAD] for p in seqs]
        his = [la.astype(BF16) for la in las]
        r1s = [la - hi.astype(F32) for la, hi in zip(las, his)]
        mids = [r1.astype(BF16) for r1 in r1s]
        los = [(r1 - mid.astype(F32)).astype(BF16) for r1, mid in zip(r1s, mids)]
        parts = [[jnp.dot(tri, part, preferred_element_type=F32) for part in (hi, mid, lo)]
                 for hi, mid, lo in zip(his, mids, los)]
        cums = [a + b + c3 for a, b, c3 in parts]
        lasts = [cum[GLA_CHUNK - 1:GLA_CHUNK] for cum in cums]
        qes = [qkl_refs[p][rows, 0:GLA_PAD] * jnp.exp(cums[p]) for p in seqs]
        ks = [qkl_refs[p][rows, GLA_PAD:2 * GLA_PAD] for p in seqs]
        kes = [ks[p] * jnp.exp(-cums[p]) for p in seqs]
        kls = [ks[p] * jnp.exp(lasts[p] - cums[p]) for p in seqs]
        decs = [jnp.exp(last) for last in lasts]
        vs = [v_refs[p][rows, :] for p in seqs]
        sts = [st_ref[p, h] for p, h in pairs]
        atts = [_dot_nt(qes[p][:, sl(h)], kes[p][:, sl(h)]) for p, h in pairs]
        inters = [_dot_nt(qes[p][:, sl(h)], st) for (p, h), st in zip(pairs, sts)]
        upds = [_dot_tn(vs[p][:, sl(h)], kls[p][:, sl(h)]) for p, h in pairs]
        intras = [_dot(jnp.where(causal, att, 0.0), vs[p][:, sl(h)]) for (p, h), att in zip(pairs, atts)]
        for i, (p, h) in enumerate(pairs):
            st_ref[p, h] = sts[i] * decs[p][:, sl(h)] + upds[i]
            o = intras[i] + inters[i]
            on = o * lax.rsqrt(jnp.mean(o * o, axis=-1, keepdims=True) + LN_EPS) * gn
            o_refs[p][rows, sl(h)] = on * rs_refs[p][rows, sl(h)]
        return carry

    lax.fori_loop(0, tb // GLA_CHUNK, chunk, 0)

    @pl.when(t == pl.num_programs(1) - 1)
    def _fin():
        for p in range(ns):
            sout_refs[p][0] = st_ref[p]


def _gla(qkl, v, rs, s0t, gn, *, nb, t, tb):
    nt = t // tb
    ns = GLA_SEQS_PER_STEP
    assert nb % ns == 0
    per = nb // ns
    w = GLA_HEADS * GLA_DV
    tri = jnp.tril(jnp.ones((GLA_CHUNK, GLA_CHUNK), BF16))
    row_in = lambda width, p: pl.BlockSpec((tb, width), lambda g, i: ((p * per + g) * nt + i, 0))
    st_in = lambda p: pl.BlockSpec((1, GLA_HEADS, GLA_DV, LANES), lambda g, i: (p * per + g, 0, 0, 0))
    row_out = pl.BlockSpec((tb, w), lambda g, i: (g * nt + i, 0))
    st_out = pl.BlockSpec((1, GLA_HEADS, GLA_DV, LANES), lambda g, i: (g, 0, 0, 0))
    slots = range(ns)
    outs = pl.pallas_call(
        functools.partial(_gla_kernel, tb=tb, ns=ns),
        out_shape=[jax.ShapeDtypeStruct((per * t, w), F32)] * ns
        + [jax.ShapeDtypeStruct((per,) + s0t.shape[1:], F32)] * ns,
        grid=(per, nt),
        in_specs=[row_in(3 * GLA_PAD, p) for p in slots] + [row_in(w, p) for p in slots] + [row_in(w, p) for p in slots]
        + [st_in(p) for p in slots] + [_const_spec(tri.shape), _const_spec(gn.shape)],
        out_specs=[row_out] * ns + [st_out] * ns,
        scratch_shapes=[pltpu.VMEM((ns, GLA_HEADS, GLA_DV, LANES), F32)],
        compiler_params=_params("parallel", "arbitrary"),
        name="gla",
    )(*([qkl] * ns + [v] * ns + [rs] * ns + [s0t] * ns + [tri, gn]))
    return jnp.concatenate(outs[:ns], axis=0), jnp.concatenate(outs[ns:], axis=0)


def _lru_kernel(xb_ref, gg_ref, buf0_ref, h0_ref, cw_ref, cb_ref, wa_ref, ba_ref, wx_ref, bx_ref, lam_ref,
                y_ref, hlast_ref, cbuf_ref, xs_ref, a_ref, b_ref, hs_ref, hcar_ref, *, tb, t_last):
    t = pl.program_id(1)
    halo = SUBLANES

    @pl.when(t == 0)
    def _init():
        xs_ref[0:halo] = buf0_ref[0]
        hcar_ref[...] = h0_ref[0]

    xs_ref[halo:halo + tb] = xb_ref[...]
    xc = cb_ref[...] + sum(xs_ref[pl.ds(halo - (CONV_W - 1) + i, tb), :] * cw_ref[i:i + 1, :] for i in range(CONV_W))
    ga = _sigmoid(_dot(xc, wa_ref[...]) + ba_ref[...])
    gx = _sigmoid(_dot(xc, wx_ref[...]) + bx_ref[...])
    log_at = ga * (-LRU_C * _softplus(-lam_ref[...]))
    a = jnp.exp(log_at)
    a_ref[...] = a
    b_ref[...] = jnp.sqrt(-jnp.tanh(log_at) * (a * a + 1.0)) * (gx * xc)

    def step(i, h):
        h = a_ref[pl.ds(i, 1), :] * h + b_ref[pl.ds(i, 1), :]
        hs_ref[pl.ds(i, 1), :] = h
        return h

    hcar_ref[...] = lax.fori_loop(0, tb, step, hcar_ref[...], unroll=8)
    y_ref[...] = hs_ref[...] * gg_ref[...]

    @pl.when(t == pl.num_programs(1) - 1)
    def _fin():
        hlast_ref[0] = hs_ref[t_last:t_last + 1, :]
        cbuf_ref[0] = xs_ref[halo + t_last - (CONV_W - 2):halo + t_last + 1, :]

    xs_ref[0:halo] = xs_ref[tb:tb + halo]


def _lru(xb, gg, buf0, h0, consts, *, nb, t, tb, t_last):
    nt = t // tb
    w = LRU_WIDTH
    row = pl.BlockSpec((tb, w), lambda b, i: (b * nt + i, 0))
    per_b = lambda r: pl.BlockSpec((1, r, w), lambda b, i: (b, 0, 0))
    return pl.pallas_call(
        functools.partial(_lru_kernel, tb=tb, t_last=t_last),
        out_shape=[jax.ShapeDtypeStruct((nb * t, w), F32), jax.ShapeDtypeStruct((nb, 1, w), F32),
                   jax.ShapeDtypeStruct((nb, CONV_W - 1, w), F32)],
        grid=(nb, nt),
        in_specs=[row, row, per_b(SUBLANES), per_b(1)] + [_const_spec(c.shape) for c in consts],
        out_specs=[row, per_b(1), per_b(CONV_W - 1)],
        scratch_shapes=[pltpu.VMEM((tb + 2 * SUBLANES, w), F32), pltpu.VMEM((tb, w), F32), pltpu.VMEM((tb, w), F32),
                        pltpu.VMEM((tb, w), F32), pltpu.VMEM((1, w), F32)],
        compiler_params=_params("parallel", "arbitrary"),
        name="lru",
    )(xb, gg, buf0, h0, *consts)


def _mix_out_kernel(a_ref, b_ref, x_ref, wa_ref, wb_ref, g_ref, beta_ref, o_ref):
    m = _dot(a_ref[...], wa_ref[...]) + _dot(b_ref[...], wb_ref[...])
    o_ref[...] = _layer_norm_rows(DN_ALPHA * x_ref[...] + m, g_ref[...], beta_ref[...])


def _mix_out(a, b, x, wa, wb, g, beta, *, tm=512):
    n, d = x.shape
    row = lambda w: pl.BlockSpec((tm, w), lambda i: (i, 0))
    consts = [wa, wb, g, beta]
    return pl.pallas_call(
        _mix_out_kernel,
        out_shape=jax.ShapeDtypeStruct((n, d), F32),
        grid=(n // tm,),
        in_specs=[row(a.shape[1]), row(b.shape[1]), row(d)] + [_const_spec(c.shape) for c in consts],
        out_specs=row(d),
        compiler_params=_params("parallel"),
        name="mix_out",
    )(a, b, x, *consts)


def _pad_heads(w, heads, width):
    lead = w.shape[:-1]
    w = w.reshape(lead + (heads, width))
    w = jnp.pad(w, [(0, 0)] * len(lead) + [(0, 0), (0, LANES - width)])
    return w.reshape(lead + (heads * LANES,))


def _even_weights(w_in, w_gate, b_gate, gla_norm, conv_w, conv_b, lru_wa, lru_ba, lru_wx, lru_bx, lru_lambda, w_out):
    gk = GLA_HEADS * GLA_DK
    gv = GLA_HEADS * GLA_DV
    o = 0
    cols = {}
    for name, width in (("q", gk), ("k", gk), ("v", gv), ("lr", GLA_RANK), ("r", gv), ("xb", LRU_WIDTH), ("gb", LRU_WIDTH)):
        cols[name] = w_in[:, o:o + width]
        o += width
    wlr = jnp.pad(cols["lr"], ((0, 0), (0, LANES - GLA_RANK)))
    wg = jnp.pad(_pad_heads(w_gate, GLA_HEADS, GLA_DK), ((0, LANES - GLA_RANK), (0, 0)))
    bg = _pad_heads(b_gate[None], GLA_HEADS, GLA_DK)
    in_consts = [_pad_heads(cols["q"], GLA_HEADS, GLA_DK).astype(BF16), _pad_heads(cols["k"], GLA_HEADS, GLA_DK).astype(BF16),
                 cols["v"].astype(BF16), cols["r"].astype(BF16), cols["xb"].astype(BF16), cols["gb"].astype(BF16),
                 wlr.astype(BF16), wg.astype(BF16), bg]
    eye = jnp.eye(LRU_BLOCKS, dtype=F32)
    bd = lambda w: (eye[:, None, :, None] * w[:, :, None, :]).reshape(LRU_WIDTH, LRU_WIDTH).astype(BF16)
    lru_consts = [conv_w, conv_b[None], bd(lru_wa), lru_ba[None], bd(lru_wx), lru_bx[None], lru_lambda[None]]
    return in_consts, gla_norm[None], lru_consts, w_out[:gv].astype(BF16), w_out[gv:].astype(BF16)


def _seq_rows(a, off, nb, t, t_pad):
    a = a[off:off + nb * t].reshape(nb, t, a.shape[1])
    return jnp.pad(a, ((0, 0), (0, t_pad - t), (0, 0))).reshape(nb * t_pad, a.shape[2])


def _unpad_rows(a, nb, t, t_pad):
    return a.reshape(nb, t_pad, a.shape[1])[:, :t].reshape(nb * t, a.shape[1])


def _even_layer(x, groups, weights, g, beta, *, tm):
    in_consts, gn, lru_consts, wo_a, wo_b = weights
    qkl, v, rs, xb, gg = _even_in(x, in_consts, tm=tm)
    ogs, yls, states = [], [], []
    for off, nb, t, tb, s_gla, h_lru, cbuf in groups:
        s0t = jnp.pad(jnp.swapaxes(s_gla, 2, 3), ((0, 0), (0, 0), (0, 0), (0, LANES - GLA_DK)))
        buf0 = jnp.pad(cbuf, ((0, 0), (SUBLANES - (CONV_W - 1), 0), (0, 0)))
        if tb is None:
            tg = GLA_CHUNK
            tl = SUBLANES
            og, st = _gla(_seq_rows(qkl, off, nb, t, tg), _seq_rows(v, off, nb, t, tg), _seq_rows(rs, off, nb, t, tg),
                          s0t, gn, nb=nb, t=tg, tb=tg)
            yl, hl, cb = _lru(_seq_rows(xb, off, nb, t, tl), _seq_rows(gg, off, nb, t, tl), buf0, h_lru[:, None],
                              lru_consts, nb=nb, t=tl, tb=tl, t_last=t - 1)
            og = _unpad_rows(og, nb, t, tg)
            yl = _unpad_rows(yl, nb, t, tl)
        else:
            og, st = _gla(qkl, v, rs, s0t, gn, nb=nb, t=t, tb=GLA_TIME_BLOCK)
            yl, hl, cb = _lru(xb, gg, buf0, h_lru[:, None], lru_consts, nb=nb, t=t, tb=tb, t_last=tb - 1)
        ogs.append(og)
        yls.append(yl)
        states.append((jnp.swapaxes(st[..., :GLA_DK], 2, 3), hl[:, 0], cb))
    n_used = sum(o.shape[0] for o in ogs)
    tail = jnp.zeros((x.shape[0] - n_used, LRU_WIDTH), F32)
    og = jnp.concatenate(ogs + [tail], axis=0)
    yl = jnp.concatenate(yls + [tail], axis=0)
    return _mix_out(og, yl, x, wo_a, wo_b, g, beta, tm=tm), states


RWKV_STREAMS = 5


def _rwkv_in_kernel(*refs, has_vres, tm, long_seqs, short_seqs):
    if has_vres:
        (x_ref, halo_ref, sl_ref, ov_ref, mu_ref, wr_ref, wk_ref, wv_ref, w1_ref, w2_ref, a1_ref, a2_ref, g1_ref, g2_ref,
         w0_ref, a0_ref, vf_ref, v1_ref, v2_ref, v0_ref, p_ref, g_ref, xs_ref) = refs
    else:
        (x_ref, halo_ref, sl_ref, ov_ref, mu_ref, wr_ref, wk_ref, wv_ref, w1_ref, w2_ref, a1_ref, a2_ref, g1_ref, g2_ref,
         w0_ref, a0_ref, p_ref, g_ref, xs_ref) = refs
    d = D_MODEL
    x = x_ref[...]
    r0 = pl.program_id(0) * tm
    xs_ref[0:SUBLANES] = halo_ref[...]
    xs_ref[SUBLANES:SUBLANES + tm] = x
    prev = xs_ref[pl.ds(SUBLANES - 1, tm), :]
    rows = r0 + lax.broadcasted_iota(jnp.int32, (tm, 1), 0)
    off, cnt, length = long_seqs
    b = (jnp.maximum(r0 - off, 0) + (length - 1)) // length
    hit = jnp.logical_and(rows == off + b * length, b < cnt)
    prev = jnp.where(hit, sl_ref[pl.ds(jnp.minimum(b, cnt - 1), 1), :], prev)
    off, cnt, length = short_seqs
    rel = rows - off
    first = jnp.logical_and(jnp.logical_and(rel >= 0, rel < cnt * length), jnp.bitwise_and(rel, length - 1) == 0)
    prev = jnp.where(first, ov_ref[...], prev)
    xx = prev - x
    mix = lambda i: (x + xx * mu_ref[i:i + 1, :]).astype(BF16)
    xr, xw, xk, xv, xa, xg = (mix(i) for i in range(6))
    p_ref[:, 0:d] = _dot(xr, wr_ref[...])
    p_ref[:, d:2 * d] = _dot(xk, wk_ref[...])
    v = _dot(xv, wv_ref[...])
    if has_vres:
        v = v + (vf_ref[...] - v) * _sigmoid(v0_ref[...] + _dot(_dot(xv, v1_ref[...]), v2_ref[...]))
    p_ref[:, 2 * d:3 * d] = v
    p_ref[:, 3 * d:4 * d] = w0_ref[...] + _dot(jnp.tanh(_dot(xw, w1_ref[...])), w2_ref[...])
    p_ref[:, 4 * d:5 * d] = a0_ref[...] + _dot(_dot(xa, a1_ref[...]), a2_ref[...])
    g_ref[...] = _dot(_sigmoid(_dot(xg, g1_ref[...])), g2_ref[...])


def _rwkv_in(x, shift_long, shift_short, consts, vres, *, tm, long_seqs, short_seqs):
    n, d = x.shape
    off, cnt, length = short_seqs
    assert length & (length - 1) == 0 and long_seqs[2] >= tm and tm % SUBLANES == 0
    tile0 = off // tm
    tiles = -(-(off + cnt * length) // tm) - tile0
    ov = jnp.pad(shift_short[:, None], ((0, 0), (0, length - 1), (0, 0))).reshape(cnt * length, d)
    ov = jnp.pad(ov, ((off - tile0 * tm, tiles * tm - (off - tile0 * tm) - cnt * length), (0, 0)))
    sl = jnp.pad(shift_long, ((0, -shift_long.shape[0] % SUBLANES), (0, 0)))
    row = lambda w: pl.BlockSpec((tm, w), lambda i: (i, 0))
    halo = pl.BlockSpec((SUBLANES, d), lambda i: (jnp.maximum(i * (tm // SUBLANES) - 1, 0), 0))
    ov_spec = pl.BlockSpec((tm, d), lambda i: (jnp.clip(i - tile0, 0, tiles - 1), 0))
    args = [x, x, sl, ov] + list(consts)
    specs = [row(d), halo, _const_spec(sl.shape), ov_spec] + [_const_spec(c.shape) for c in consts]
    if vres is not None:
        vf, vconsts = vres
        args += [vf] + list(vconsts)
        specs += [row(d)] + [_const_spec(c.shape) for c in vconsts]
    return pl.pallas_call(
        functools.partial(_rwkv_in_kernel, has_vres=vres is not None, tm=tm, long_seqs=long_seqs, short_seqs=short_seqs),
        out_shape=[jax.ShapeDtypeStruct((n, RWKV_STREAMS * d), F32), jax.ShapeDtypeStruct((n, d), F32)],
        grid=(n // tm,),
        in_specs=specs,
        out_specs=[row(RWKV_STREAMS * d), row(d)],
        scratch_shapes=[pltpu.VMEM((tm + SUBLANES, d), F32)],
        compiler_params=_params("parallel"),
        name="rwkv_in",
    )(*args)


def _wkv_kernel(p_ref, s0_ref, kk_ref, ka_ref, rk_ref, lg_ref, lb_ref, y_ref, sout_ref,
                s_ref, dec_ref, a_ref, b_ref, km_ref, bon_ref, *, tt):
    t = pl.program_id(1)
    hk = RWKV_HEAD

    @pl.when(t == 0)
    def _init():
        s_ref[...] = s0_ref[...]

    r = p_ref[:, 0]
    k = p_ref[:, 1]
    v = p_ref[:, 2]
    dec_ref[...] = jnp.exp(-jnp.exp(-_softplus(-p_ref[:, 3]) - 0.5))
    ag = _sigmoid(p_ref[:, 4])
    kk = k * kk_ref[...][None]
    kk = kk / jnp.maximum(jnp.sqrt(jnp.sum(kk * kk, axis=1, keepdims=True)), 1e-12)
    km = k * (1.0 + (ag - 1.0) * ka_ref[...][None])
    km_ref[...] = km
    a_ref[...] = -kk
    b_ref[...] = kk * ag
    bon_ref[...] = jnp.sum(r * km * rk_ref[...][None], axis=1, keepdims=True) * v
    lg = lg_ref[...]
    lb = lb_ref[...]

    def step(i, carry):
        lanes = 4
        parts = [jnp.zeros((hk, LANES), F32) for _ in range(lanes)]
        for q in range(hk):
            parts[q % lanes] = parts[q % lanes] + s_ref[q] * a_ref[i, q:q + 1, :]
        sa = (parts[0] + parts[1]) + (parts[2] + parts[3])
        vv = p_ref[i, 2]
        parts = [jnp.zeros((hk, LANES), F32) for _ in range(lanes)]
        for q in range(hk):
            sn = s_ref[q] * dec_ref[i, q:q + 1, :] + (sa * b_ref[i, q:q + 1, :] + vv * km_ref[i, q:q + 1, :])
            s_ref[q] = sn
            parts[q % lanes] = parts[q % lanes] + sn * p_ref[i, 0, q:q + 1, :]
        y = (parts[0] + parts[1]) + (parts[2] + parts[3])
        mu = jnp.mean(y, axis=0, keepdims=True)
        dlt = y - mu
        var = jnp.mean(dlt * dlt, axis=0, keepdims=True)
        y_ref[i] = dlt * lax.rsqrt(var + RWKV_GN_EPS) * lg + lb + bon_ref[i]
        return carry

    lax.fori_loop(0, tt, step, 0)

    @pl.when(t == pl.num_programs(1) - 1)
    def _fin():
        sout_ref[...] = s_ref[...]


def _wkv(p, s0, lane_consts, *, tt):
    t, _, hk, l = p.shape
    lane2 = pl.BlockSpec((hk, LANES), lambda g, i: (0, g))
    st = pl.BlockSpec((hk, hk, LANES), lambda g, i: (0, 0, g))
    seq = lambda: pltpu.VMEM((tt, hk, LANES), F32)
    return pl.pallas_call(
        functools.partial(_wkv_kernel, tt=tt),
        out_shape=[jax.ShapeDtypeStruct((t, hk, l), F32), jax.ShapeDtypeStruct((hk, hk, l), F32)],
        grid=(l // LANES, t // tt),
        in_specs=[pl.BlockSpec((tt, RWKV_STREAMS, hk, LANES), lambda g, i: (i, 0, 0, g)), st] + [lane2] * 5,
        out_specs=[pl.BlockSpec((tt, hk, LANES), lambda g, i: (i, 0, g)), st],
        scratch_shapes=[pltpu.VMEM((hk, hk, LANES), F32), seq(), seq(), seq(), seq(), seq()],
        compiler_params=_params("parallel", "arbitrary"),
        name="wkv",
    )(p, s0, *lane_consts)


def _gate_out_kernel(y_ref, g_ref, x_ref, w_ref, lg_ref, lb_ref, o_ref):
    m = _dot(y_ref[...] * g_ref[...], w_ref[...])
    o_ref[...] = _layer_norm_rows(DN_ALPHA * x_ref[...] + m, lg_ref[...], lb_ref[...])


def _gate_out(y, g, x, w, lg, lb, *, tm):
    n, d = x.shape
    row = pl.BlockSpec((tm, d), lambda i: (i, 0))
    consts = [w, lg, lb]
    return pl.pallas_call(
        _gate_out_kernel,
        out_shape=jax.ShapeDtypeStruct((n, d), F32),
        grid=(n // tm,),
        in_specs=[row, row, row] + [_const_spec(c.shape) for c in consts],
        out_specs=row,
        compiler_params=_params("parallel"),
        name="gate_out",
    )(y, g, x, *consts)


def _pad_cols(w):
    return jnp.pad(w, ((0, 0), (0, LANES - w.shape[1]))).astype(BF16)


def _pad_rows(w):
    return jnp.pad(w, ((0, LANES - w.shape[0]), (0, 0))).astype(BF16)


def _rwkv_weights(mu, w_r, w_k, w_v, w_o, w0, w1, w2, a0, a1, a2, g1, g2, k_k, k_a, r_k, lnx_g, lnx_b, vres):
    consts = [jnp.pad(mu, ((0, SUBLANES - mu.shape[0]), (0, 0))), w_r.astype(BF16), w_k.astype(BF16), w_v.astype(BF16),
              _pad_cols(w1), _pad_rows(w2), _pad_cols(a1), _pad_rows(a2), _pad_cols(g1), _pad_rows(g2), w0[None], a0[None]]
    vconsts = None if vres is None else [_pad_cols(vres[1]), _pad_rows(vres[2]), vres[0][None]]
    per_head = [k_k.reshape(RWKV_HEADS, RWKV_HEAD).T, k_a.reshape(RWKV_HEADS, RWKV_HEAD).T, r_k.T,
                lnx_g.reshape(RWKV_HEADS, RWKV_HEAD).T, lnx_b.reshape(RWKV_HEADS, RWKV_HEAD).T]
    return consts, vconsts, per_head, w_o.astype(BF16)


def _rwkv_layer(x, groups, weights, v_first, g, beta, *, tm):
    consts, vconsts, per_head, w_o = weights
    n, d = x.shape
    (off_l, nb_l, t_l, _, _, shift_l), (off_s, nb_s, t_s, _, _, shift_s) = groups
    n_used = nb_l * t_l + nb_s * t_s
    vres = None if vconsts is None else (v_first, vconsts)
    p, gate = _rwkv_in(x, shift_l, shift_s, consts, vres, tm=tm,
                       long_seqs=(off_l, nb_l, t_l), short_seqs=(off_s, nb_s, t_s))
    ys, states = [], []
    for off, nb, t, tt, s, shift in groups:
        lanes = nb * RWKV_HEADS
        pg = p[off:off + nb * t].reshape(nb, t, RWKV_STREAMS, RWKV_HEADS, RWKV_HEAD)
        pg = jnp.transpose(pg, (1, 2, 4, 0, 3)).reshape(t, RWKV_STREAMS, RWKV_HEAD, lanes)
        s0 = jnp.transpose(s, (3, 2, 0, 1)).reshape(RWKV_HEAD, RWKV_HEAD, lanes)
        lane_consts = [jnp.tile(c, (1, nb)) for c in per_head]
        y, s_out = _wkv(pg, s0, lane_consts, tt=tt)
        ys.append(jnp.transpose(y.reshape(t, RWKV_HEAD, nb, RWKV_HEADS), (2, 0, 3, 1)).reshape(nb * t, d))
        s_new = jnp.transpose(s_out.reshape(RWKV_HEAD, RWKV_HEAD, nb, RWKV_HEADS), (2, 3, 1, 0))
        states.append((s_new, x[off:off + nb * t].reshape(nb, t, d)[:, -1]))
    y = jnp.concatenate(ys + [jnp.zeros((n - n_used, d), F32)], axis=0)
    return _gate_out(y, gate, x, w_o, g, beta, tm=tm), states, p[:, 2 * d:3 * d]


TOKEN_TILE = 512
PEER_TOKEN_TILE = 1024
PROMPT_TIME_BLOCK = 688
WKV_TIME_BLOCK = 16


def kernel(x_prompt, x_sample, state_gla, state_lru_h, state_lru_conv, state_rwkv, state_rwkv_shift, meta_tokens, ln_g, ln_b, ev_w_in, ev_gla_w_gate, ev_gla_b_gate, ev_gla_norm, ev_conv_w, ev_conv_b, ev_lru_wa, ev_lru_ba, ev_lru_wx, ev_lru_bx, ev_lru_lambda, ev_w_out, od_mu, od_w_r, od_w_k, od_w_v, od_w_o, od_w0, od_w1, od_w2, od_a0, od_a1, od_a2, od_v0, od_v1, od_v2, od_g1, od_g2, od_k_k, od_k_a, od_r_k, od_lnx_g, od_lnx_b, peer_w_q, peer_keys, peer_u, peer_v):
    bp, sp, d = x_prompt.shape
    bs, ss, _ = x_sample.shape
    tp = sp + N_META
    n_p, n_s = bp * tp, bs * ss
    n = -(-(n_p + n_s) // PEER_TOKEN_TILE) * PEER_TOKEN_TILE
    assert tp % PROMPT_TIME_BLOCK == 0 and tp % GLA_TIME_BLOCK == 0 and tp % WKV_TIME_BLOCK == 0 and n % TOKEN_TILE == 0

    xp = jnp.concatenate([jnp.broadcast_to(meta_tokens[None], (bp, N_META, d)), x_prompt], axis=1)
    x = jnp.concatenate([xp.reshape(n_p, d), x_sample.reshape(n_s, d), jnp.zeros((n - n_p - n_s, d), F32)], axis=0)

    n_pairs = DEPTH // 2
    zeros = lambda *s: jnp.zeros(s, F32)
    outs = {k: [] for k in ("p_gla", "p_h", "p_conv", "p_rwkv", "p_shift", "s_gla", "s_h", "s_conv", "s_rwkv", "s_shift")}
    v_first = None
    for layer in range(DEPTH):
        j = layer // 2
        g, beta = ln_g[layer, 0][None], ln_b[layer, 0][None]
        if layer % 2 == 0:
            weights = _even_weights(ev_w_in[j], ev_gla_w_gate[j], ev_gla_b_gate[j], ev_gla_norm[j], ev_conv_w[j], ev_conv_b[j],
                                    ev_lru_wa[j], ev_lru_ba[j], ev_lru_wx[j], ev_lru_bx[j], ev_lru_lambda[j], ev_w_out[j])
            groups = [(0, bp, tp, PROMPT_TIME_BLOCK, zeros(bp, GLA_HEADS, GLA_DK, GLA_DV), zeros(bp, LRU_WIDTH),
                       zeros(bp, CONV_W - 1, LRU_WIDTH)),
                      (n_p, bs, ss, None, state_gla[j], state_lru_h[j], state_lru_conv[j])]
            x, states = _even_layer(x, groups, weights, g, beta, tm=TOKEN_TILE)
            for pre, st in zip("ps", states):
                outs[pre + "_gla"].append(st[0])
                outs[pre + "_h"].append(st[1])
                outs[pre + "_conv"].append(st[2])
        else:
            vres = None if j == 0 else (od_v0[j - 1], od_v1[j - 1], od_v2[j - 1])
            weights = _rwkv_weights(od_mu[j], od_w_r[j], od_w_k[j], od_w_v[j], od_w_o[j], od_w0[j], od_w1[j], od_w2[j],
                                    od_a0[j], od_a1[j], od_a2[j], od_g1[j], od_g2[j], od_k_k[j], od_k_a[j], od_r_k[j],
                                    od_lnx_g[j], od_lnx_b[j], vres)
            groups = [(0, bp, tp, WKV_TIME_BLOCK, zeros(bp, RWKV_HEADS, RWKV_HEAD, RWKV_HEAD), zeros(bp, d)),
                      (n_p, bs, ss, ss, state_rwkv[j], state_rwkv_shift[j])]
            x, states, v = _rwkv_layer(x, groups, weights, v_first, g, beta, tm=TOKEN_TILE)
            if v_first is None:
                v_first = v
            for pre, st in zip("ps", states):
                outs[pre + "_rwkv"].append(st[0])
                outs[pre + "_shift"].append(st[1])
        x = _peer_layer(x, peer_w_q[layer].T.astype(BF16), peer_keys[layer, :, 0].astype(BF16),
                        peer_keys[layer, :, 1].astype(BF16), peer_u[layer].astype(BF16), peer_v[layer].astype(BF16),
                        ln_g[layer, 1][None], ln_b[layer, 1][None], tn=PEER_TOKEN_TILE)

    y_prompt = x[:n_p].reshape(bp, tp, d)[:, N_META:]
    y_sample = x[n_p:n_p + n_s].reshape(bs, ss, d)
    st = {k: jnp.stack(v) for k, v in outs.items()}
    return (y_prompt, y_sample, st["p_gla"], st["p_h"], st["p_conv"], st["p_rwkv"], st["p_shift"],
            st["s_gla"], st["s_h"], st["s_conv"], st["s_rwkv"], st["s_shift"])
```

```python
import functools

import jax
import jax.numpy as jnp
from jax import lax
from jax.experimental import pallas as pl
from jax.experimental.pallas import tpu as pltpu

F32 = jnp.float32
BF16 = jnp.bfloat16

D_MODEL = 1024
DEPTH = 4
N_META = 16
GLA_HEADS = 4
GLA_DK = 64
GLA_DV = 128
GLA_RANK = 16
GLA_GATE_NORM = 16.0
GLA_CHUNK = 16
LRU_WIDTH = 512
LRU_BLOCKS = 8
CONV_W = 4
LRU_C = 8.0
RWKV_HEAD = 64
RWKV_HEADS = D_MODEL // RWKV_HEAD
RWKV_GN_EPS = 64e-5
PEER_HEADS = 8
PEER_NKEYS = 128
PEER_DKEY = 256
PEER_HALF = PEER_DKEY // 2
PEER_TOPK = 16
DN_ALPHA = float((2 * DEPTH) ** 0.25)
LN_EPS = 1e-5

LANES = 128
SUBLANES = 8
VMEM_LIMIT = 56 * 1024 * 1024


def _dot(a, b):
    return jnp.dot(a.astype(BF16), b.astype(BF16), preferred_element_type=F32)


def _dot_nt(a, b):
    return lax.dot_general(a.astype(BF16), b.astype(BF16), (((1,), (1,)), ((), ())), preferred_element_type=F32)


def _dot_tn(a, b):
    return lax.dot_general(a.astype(BF16), b.astype(BF16), (((0,), (0,)), ((), ())), preferred_element_type=F32)


def _layer_norm_rows(z, g, b):
    mu = jnp.mean(z, axis=-1, keepdims=True)
    d = z - mu
    var = jnp.mean(d * d, axis=-1, keepdims=True)
    return d * lax.rsqrt(var + LN_EPS) * g + b


def _sigmoid(x):
    return 1.0 / (1.0 + jnp.exp(-x))


def _softplus(x):
    return jnp.maximum(x, 0.0) + jnp.log1p(jnp.exp(-jnp.abs(x)))


def _gelu_tanh(x):
    c0 = 0.7978845608028654
    c1 = c0 * 0.044715
    one = jnp.asarray(1.0, x.dtype)
    return (0.5 * x) * (one + jnp.tanh(x * (c0 + c1 * (x * x))))


def _silu(x):
    return x * _sigmoid(x)


def _params(*sem):
    return pltpu.CompilerParams(dimension_semantics=sem, vmem_limit_bytes=VMEM_LIMIT)


def _const_spec(shape):
    nd = len(shape)
    return pl.BlockSpec(shape, lambda *_: (0,) * nd)


PEER_CAND_ROWS = 80
PEER_PACK = 2 * SUBLANES
PEER_SELECT_LANES = 2 * LANES


def _peer_cand_index():
    rows = [r1 * PEER_TOPK for r1 in range(16)]
    for j in range(1, 8):
        rows += [r1 * PEER_TOPK + j for r1 in range(8)]
    rows += list(range(8, 16))
    return jnp.broadcast_to(jnp.asarray(rows, F32)[:, None], (PEER_CAND_ROWS, PEER_SELECT_LANES))


PEER_CODE_STEP = 2.0 ** 116
PEER_CODED_BELOW = -(2.0 ** 119)


def _rank_code(r):
    return -PEER_CODE_STEP * (PEER_TOPK + r)


def _decode_rank(s):
    return jnp.where(s < PEER_CODED_BELOW, s * (-1.0 / PEER_CODE_STEP) - PEER_TOPK, float(PEER_TOPK))


def _count_coded(s):
    return jnp.sum(jnp.where(s < PEER_CODED_BELOW, 1.0, 0.0), axis=0, keepdims=True)


def _top16(s, key_iota):
    row16 = lax.broadcasted_iota(jnp.int32, (PEER_TOPK, s.shape[1]), 0)
    vals = jnp.zeros((PEER_TOPK, s.shape[1]), F32)
    for r in range(PEER_TOPK):
        m = jnp.max(s, axis=0, keepdims=True)
        hit = s == m
        if key_iota is not None:
            first = jnp.min(jnp.where(hit, key_iota, float(PEER_NKEYS)), axis=0, keepdims=True)
            hit = key_iota == first
        s = jnp.where(hit, _rank_code(r), s)
        vals = jnp.where(row16 == r, m, vals)
    return vals, s


def _peer_candidates(a, b):
    blocks = [a + b[0:1]]
    for j in range(1, 8):
        blocks.append(a[0:8] + b[j:j + 1])
    blocks.append(a[0:1] + b[8:16])
    return jnp.concatenate(blocks, axis=0)


def _select16(c, cidx):
    for _ in range(PEER_TOPK):
        m = jnp.max(c, axis=0, keepdims=True)
        hit = c == m
        if cidx is not None:
            first = jnp.min(jnp.where(hit, cidx, 1e9), axis=0, keepdims=True)
            hit = cidx == first
        c = jnp.where(hit, _rank_code(0), c)
    return c


def _peer_gate_tiles(s1, s2, a, b, s1c, s2c, c, cc):
    sel = jnp.where(cc < PEER_CODED_BELOW, 1.0, 0.0)
    z = jnp.sum(sel * jnp.exp(c - c[0:1]), axis=0, keepdims=True)
    j_lo = sel[0:8]
    for j in range(1, 8):
        j_lo = j_lo + sel[8 + 8 * j:16 + 8 * j]
    extra = jnp.sum(sel[72:80], axis=0, keepdims=True)
    row8 = lax.broadcasted_iota(jnp.int32, j_lo.shape, 0)
    j_lo = j_lo + jnp.where(row8 == 0, extra, 0.0)
    jt = jnp.concatenate([j_lo, sel[8:16]], axis=0)
    r1 = _decode_rank(s1c)
    h1 = jnp.zeros_like(r1)
    for r in range(PEER_TOPK):
        h1 = jnp.where(r1 == float(r), jt[r:r + 1], h1)
    return h1, jnp.exp(s1 - a[0:1]) / z, _decode_rank(s2c), jnp.exp(s2 - b[0:1])


def _peer_kernel(x_ref, wqt_ref, k1_ref, k2_ref, cidx_ref, u_ref, vt_ref, g_ref, b_ref, o_ref,
                 xt_ref, acc_ref, h1_ref, c1_ref, r2_ref, e2_ref, s_ref, hd0_ref, hd1_ref, coef_ref, *, tn, te):
    j = pl.program_id(1)
    nj = pl.num_programs(1)

    @pl.when(j == 0)
    def _select():
        xt_ref[...] = x_ref[...].T.astype(BF16)
        acc_ref[...] = jnp.zeros_like(acc_ref)
        hd1_ref[...] = jnp.zeros_like(hd1_ref)
        coef_ref[0:te] = jnp.zeros((te, tn), BF16)
        key_iota = lax.broadcasted_iota(jnp.int32, (PEER_NKEYS, PEER_SELECT_LANES), 0).astype(F32)

        def head(h, carry):
            q = _dot(wqt_ref[pl.ds(pl.multiple_of(h * PEER_DKEY, PEER_DKEY), PEER_DKEY), :], xt_ref[...])
            mu = jnp.mean(q, axis=0, keepdims=True)
            d = q - mu
            qn = d * lax.rsqrt(jnp.mean(d * d, axis=0, keepdims=True) + LN_EPS)
            s_ref[0] = _dot(k1_ref[h], qn[0:PEER_HALF])
            s_ref[1] = _dot(k2_ref[h], qn[PEER_HALF:PEER_DKEY])

            def chunk(ci, carry2):
                ls = pl.ds(pl.multiple_of(ci * PEER_SELECT_LANES, PEER_SELECT_LANES), PEER_SELECT_LANES)
                s1 = s_ref[0, :, ls]
                s2 = s_ref[1, :, ls]

                def put(tiles):
                    h1_ref[h, :, ls] = tiles[0]
                    c1_ref[h, :, ls] = tiles[1]
                    r2_ref[h, :, ls] = tiles[2].astype(BF16)
                    e2_ref[h, :, ls] = tiles[3].astype(BF16)

                a, s1c = _top16(s1, None)
                b, s2c = _top16(s2, None)
                c = _peer_candidates(a, b)
                cc = _select16(c, None)
                put(_peer_gate_tiles(s1, s2, a, b, s1c, s2c, c, cc))
                miscount = (jnp.abs(_count_coded(s1c) - PEER_TOPK) + jnp.abs(_count_coded(s2c) - PEER_TOPK)
                            + jnp.abs(_count_coded(cc) - PEER_TOPK))

                @pl.when(jnp.max(miscount) > 0.0)
                def _with_ties():
                    a, s1c = _top16(s1, key_iota)
                    b, s2c = _top16(s2, key_iota)
                    c = _peer_candidates(a, b)
                    cc = _select16(c, cidx_ref[...])
                    put(_peer_gate_tiles(s1, s2, a, b, s1c, s2c, c, cc))

                return carry2

            lax.fori_loop(0, tn // PEER_SELECT_LANES, chunk, 0)
            return carry

        lax.fori_loop(0, PEER_HEADS, head, 0)

    groups = PEER_NKEYS // PEER_PACK
    last_key = PEER_NKEYS - 1
    zero_pack = jnp.zeros((PEER_PACK, LANES), F32)

    def gate_rows(tile):
        rows = []
        for ii in range(te // PEER_NKEYS):
            i1 = jnp.clip(tile * (te // PEER_NKEYS) + ii, 0, last_key)
            rows.append(([h1_ref[h, pl.ds(i1, 1), :] for h in range(PEER_HEADS)],
                         [c1_ref[h, pl.ds(i1, 1), :] for h in range(PEER_HEADS)]))
        return rows

    def gate_stage(rows, hd_ref, coef_rows):
        nk = len(rows)
        for c in range(tn // LANES):
            ls = slice(c * LANES, (c + 1) * LANES)
            gates = [[jnp.zeros((PEER_PACK, LANES), BF16) for _ in range(groups)] for _ in range(nk)]
            for h in range(PEER_HEADS):
                bounds = [(rows[ii][0][h][:, ls] + zero_pack).astype(BF16) for ii in range(nk)]
                scales = [(rows[ii][1][h][:, ls] + zero_pack).astype(BF16) for ii in range(nk)]
                for gi in range(groups):
                    krows = slice(gi * PEER_PACK, (gi + 1) * PEER_PACK)
                    rank = r2_ref[h, krows, ls]
                    fac = e2_ref[h, krows, ls]
                    for ii in range(nk):
                        gates[ii][gi] = gates[ii][gi] + jnp.where(rank < bounds[ii], fac * scales[ii], jnp.zeros((), BF16))
            for ii in range(nk):
                for gi in range(groups):
                    erows = slice(ii * PEER_NKEYS + gi * PEER_PACK, ii * PEER_NKEYS + (gi + 1) * PEER_PACK)
                    crows = slice(coef_rows + erows.start, coef_rows + erows.stop)
                    coef_ref[crows, ls] = gates[ii][gi] * _gelu_tanh(hd_ref[erows, ls].astype(BF16))

    xt = xt_ref[...]
    hd0_ref[...] = _dot(u_ref[0:te, :], xt)
    gate_stage(gate_rows(2 * j - 1), hd1_ref, te)
    acc_ref[...] += jnp.dot(vt_ref[0], coef_ref[...], preferred_element_type=F32)
    hd1_ref[...] = _dot(u_ref[te:2 * te, :], xt)
    gate_stage(gate_rows(2 * j), hd0_ref, 0)

    @pl.when(j == nj - 1)
    def _finish():
        y = acc_ref[...].T
        o_ref[...] = _layer_norm_rows(DN_ALPHA * x_ref[...] + y, g_ref[...], b_ref[...])


def _peer_layer(x, wqt, k1, k2, u, v, g, b, *, tn=1024, te=256):
    n, d = x.shape
    ne = u.shape[0]
    assert ne % (2 * te) == 0 and n % tn == 0
    pairs = ne // (2 * te)
    steps = pairs + 1
    vt = jnp.swapaxes(v.reshape(pairs, 2 * te, d), 1, 2)
    kern = functools.partial(_peer_kernel, tn=tn, te=te)
    one = pl.Buffered(1)
    return pl.pallas_call(
        kern,
        out_shape=jax.ShapeDtypeStruct((n, d), F32),
        grid=(n // tn, steps),
        in_specs=[
            pl.BlockSpec((tn, d), lambda i, j: (i, 0), pipeline_mode=one),
            pl.BlockSpec(wqt.shape, lambda i, j: (0, 0), pipeline_mode=one),
            pl.BlockSpec(k1.shape, lambda i, j: (0, 0, 0), pipeline_mode=one),
            pl.BlockSpec(k2.shape, lambda i, j: (0, 0, 0), pipeline_mode=one),
            pl.BlockSpec((PEER_CAND_ROWS, PEER_SELECT_LANES), lambda i, j: (0, 0), pipeline_mode=one),
            pl.BlockSpec((2 * te, d), lambda i, j: (jnp.minimum(j, pairs - 1), 0)),
            pl.BlockSpec((1, d, 2 * te), lambda i, j: (jnp.maximum(j - 1, 0), 0, 0)),
            pl.BlockSpec((1, d), lambda i, j: (0, 0), pipeline_mode=one),
            pl.BlockSpec((1, d), lambda i, j: (0, 0), pipeline_mode=one),
        ],
        out_specs=pl.BlockSpec((tn, d), lambda i, j: (i, 0), pipeline_mode=one),
        scratch_shapes=[
            pltpu.VMEM((d, tn), BF16),
            pltpu.VMEM((d, tn), F32),
            pltpu.VMEM((PEER_HEADS, PEER_NKEYS, tn), F32),
            pltpu.VMEM((PEER_HEADS, PEER_NKEYS, tn), F32),
            pltpu.VMEM((PEER_HEADS, PEER_NKEYS, tn), BF16),
            pltpu.VMEM((PEER_HEADS, PEER_NKEYS, tn), BF16),
            pltpu.VMEM((2, PEER_NKEYS, tn), F32),
            pltpu.VMEM((te, tn), F32),
            pltpu.VMEM((te, tn), F32),
            pltpu.VMEM((2 * te, tn), BF16),
        ],
        compiler_params=_params("parallel", "arbitrary"),
        name="peer",
    )(x, wqt, k1, k2, _peer_cand_index(), u, vt, g, b)


GLA_PAD = GLA_HEADS * LANES


def _even_in_kernel(x_ref, wq_ref, wk_ref, wv_ref, wr_ref, wxb_ref, wgb_ref, wlr_ref, wg_ref, bg_ref,
                    qkl_ref, v_ref, rs_ref, xb_ref, gg_ref):
    xb16 = x_ref[...].astype(BF16)
    qkl_ref[:, 0:GLA_PAD] = _dot(xb16, wq_ref[...]) * (GLA_DK ** -0.5)
    qkl_ref[:, GLA_PAD:2 * GLA_PAD] = _dot(xb16, wk_ref[...])
    glr = _dot(xb16, wlr_ref[...])
    z = _dot(glr, wg_ref[...]) + bg_ref[...]
    qkl_ref[:, 2 * GLA_PAD:3 * GLA_PAD] = -_softplus(-z) * (1.0 / GLA_GATE_NORM)
    v_ref[...] = _dot(xb16, wv_ref[...])
    rs_ref[...] = _silu(_dot(xb16, wr_ref[...]))
    xb_ref[...] = _dot(xb16, wxb_ref[...])
    gg_ref[...] = _gelu_tanh(_dot(xb16, wgb_ref[...]))


def _even_in(x, consts, *, tm=512):
    n, d = x.shape
    row = lambda w: pl.BlockSpec((tm, w), lambda i: (i, 0))
    return pl.pallas_call(
        _even_in_kernel,
        out_shape=[jax.ShapeDtypeStruct((n, 3 * GLA_PAD), F32)] + [jax.ShapeDtypeStruct((n, LRU_WIDTH), F32)] * 4,
        grid=(n // tm,),
        in_specs=[row(d)] + [_const_spec(c.shape) for c in consts],
        out_specs=[row(3 * GLA_PAD)] + [row(LRU_WIDTH)] * 4,
        compiler_params=_params("parallel"),
        name="even_in",
    )(x, *consts)


GLA_SEQS_PER_STEP = 4
GLA_TIME_BLOCK = 3 * GLA_CHUNK


def _gla_kernel(*refs, tb, ns):
    qkl_refs, v_refs, rs_refs, s0_refs = (refs[i * ns:(i + 1) * ns] for i in range(4))
    tri_ref, gn_ref = refs[4 * ns:4 * ns + 2]
    o_refs = refs[4 * ns + 2:5 * ns + 2]
    sout_refs = refs[5 * ns + 2:6 * ns + 2]
    st_ref = refs[6 * ns + 2]
    t = pl.program_id(1)

    @pl.when(t == 0)
    def _init():
        for p in range(ns):
            st_ref[p] = s0_refs[p][0]

    tri = tri_ref[...]
    causal = tri > 0
    gn = gn_ref[...]

    def chunk(c, carry):
        rows = pl.ds(pl.multiple_of(c * GLA_CHUNK, GLA_CHUNK), GLA_CHUNK)
        seqs = range(ns)
        pairs = [(p, h) for p in seqs for h in range(GLA_HEADS)]
        sl = lambda h: slice(h * LANES, (h + 1) * LANES)
        las = [qkl_refs[p][rows, 2 * GLA_PAD:3 * GLA_PAD] for p in seqs]
        his = [la.astype(BF16) for la in las]
        r1s = [la - hi.astype(F32) for la, hi in zip(las, his)]
        mids = [r1.astype(BF16) for r1 in r1s]
        los = [(r1 - mid.astype(F32)).astype(BF16) for r1, mid in zip(r1s, mids)]
        parts = [[jnp.dot(tri, part, preferred_element_type=F32) for part in (hi, mid, lo)]
                 for hi, mid, lo in zip(his, mids, los)]
        cums = [a + b + c3 for a, b, c3 in parts]
        lasts = [cum[GLA_CHUNK - 1:GLA_CHUNK] for cum in cums]
        qes = [qkl_refs[p][rows, 0:GLA_PAD] * jnp.exp(cums[p]) for p in seqs]
        ks = [qkl_refs[p][rows, GLA_PAD:2 * GLA_PAD] for p in seqs]
        kes = [ks[p] * jnp.exp(-cums[p]) for p in seqs]
        kls = [ks[p] * jnp.exp(lasts[p] - cums[p]) for p in seqs]
        decs = [jnp.exp(last) for last in lasts]
        vs = [v_refs[p][rows, :] for p in seqs]
        sts = [st_ref[p, h] for p, h in pairs]
        atts = [_dot_nt(qes[p][:, sl(h)], kes[p][:, sl(h)]) for p, h in pairs]
        inters = [_dot_nt(qes[p][:, sl(h)], st) for (p, h), st in zip(pairs, sts)]
        upds = [_dot_tn(vs[p][:, sl(h)], kls[p][:, sl(h)]) for p, h in pairs]
        intras = [_dot(jnp.where(causal, att, 0.0), vs[p][:, sl(h)]) for (p, h), att in zip(pairs, atts)]
        for i, (p, h) in enumerate(pairs):
            st_ref[p, h] = sts[i] * decs[p][:, sl(h)] + upds[i]
            o = intras[i] + inters[i]
            on = o * lax.rsqrt(jnp.mean(o * o, axis=-1, keepdims=True) + LN_EPS) * gn
            o_refs[p][rows, sl(h)] = on * rs_refs[p][rows, sl(h)]
        return carry

    lax.fori_loop(0, tb // GLA_CHUNK, chunk, 0)

    @pl.when(t == pl.num_programs(1) - 1)
    def _fin():
        for p in range(ns):
            sout_refs[p][0] = st_ref[p]


def _gla(qkl, v, rs, s0t, gn, *, nb, t, tb):
    nt = t // tb
    ns = GLA_SEQS_PER_STEP
    assert nb % ns == 0
    per = nb // ns
    w = GLA_HEADS * GLA_DV
    tri = jnp.tril(jnp.ones((GLA_CHUNK, GLA_CHUNK), BF16))
    row_in = lambda width, p: pl.BlockSpec((tb, width), lambda g, i: ((p * per + g) * nt + i, 0))
    st_in = lambda p: pl.BlockSpec((1, GLA_HEADS, GLA_DV, LANES), lambda g, i: (p * per + g, 0, 0, 0))
    row_out = pl.BlockSpec((tb, w), lambda g, i: (g * nt + i, 0))
    st_out = pl.BlockSpec((1, GLA_HEADS, GLA_DV, LANES), lambda g, i: (g, 0, 0, 0))
    slots = range(ns)
    outs = pl.pallas_call(
        functools.partial(_gla_kernel, tb=tb, ns=ns),
        out_shape=[jax.ShapeDtypeStruct((per * t, w), F32)] * ns
        + [jax.ShapeDtypeStruct((per,) + s0t.shape[1:], F32)] * ns,
        grid=(per, nt),
        in_specs=[row_in(3 * GLA_PAD, p) for p in slots] + [row_in(w, p) for p in slots] + [row_in(w, p) for p in slots]
        + [st_in(p) for p in slots] + [_const_spec(tri.shape), _const_spec(gn.shape)],
        out_specs=[row_out] * ns + [st_out] * ns,
        scratch_shapes=[pltpu.VMEM((ns, GLA_HEADS, GLA_DV, LANES), F32)],
        compiler_params=_params("parallel", "arbitrary"),
        name="gla",
    )(*([qkl] * ns + [v] * ns + [rs] * ns + [s0t] * ns + [tri, gn]))
    return jnp.concatenate(outs[:ns], axis=0), jnp.concatenate(outs[ns:], axis=0)


def _lru_kernel(xb_ref, gg_ref, buf0_ref, h0_ref, cw_ref, cb_ref, wa_ref, ba_ref, wx_ref, bx_ref, lam_ref,
                y_ref, hlast_ref, cbuf_ref, xs_ref, a_ref, b_ref, hs_ref, hcar_ref, *, tb, t_last):
    t = pl.program_id(1)
    halo = SUBLANES

    @pl.when(t == 0)
    def _init():
        xs_ref[0:halo] = buf0_ref[0]
        hcar_ref[...] = h0_ref[0]

    xs_ref[halo:halo + tb] = xb_ref[...]
    xc = cb_ref[...] + sum(xs_ref[pl.ds(halo - (CONV_W - 1) + i, tb), :] * cw_ref[i:i + 1, :] for i in range(CONV_W))
    ga = _sigmoid(_dot(xc, wa_ref[...]) + ba_ref[...])
    gx = _sigmoid(_dot(xc, wx_ref[...]) + bx_ref[...])
    log_at = ga * (-LRU_C * _softplus(-lam_ref[...]))
    a = jnp.exp(log_at)
    a_ref[...] = a
    b_ref[...] = jnp.sqrt(-jnp.tanh(log_at) * (a * a + 1.0)) * (gx * xc)

    def step(i, h):
        h = a_ref[pl.ds(i, 1), :] * h + b_ref[pl.ds(i, 1), :]
        hs_ref[pl.ds(i, 1), :] = h
        return h

    hcar_ref[...] = lax.fori_loop(0, tb, step, hcar_ref[...], unroll=8)
    y_ref[...] = hs_ref[...] * gg_ref[...]

    @pl.when(t == pl.num_programs(1) - 1)
    def _fin():
        hlast_ref[0] = hs_ref[t_last:t_last + 1, :]
        cbuf_ref[0] = xs_ref[halo + t_last - (CONV_W - 2):halo + t_last + 1, :]

    xs_ref[0:halo] = xs_ref[tb:tb + halo]


def _lru(xb, gg, buf0, h0, consts, *, nb, t, tb, t_last):
    nt = t // tb
    w = LRU_WIDTH
    row = pl.BlockSpec((tb, w), lambda b, i: (b * nt + i, 0))
    per_b = lambda r: pl.BlockSpec((1, r, w), lambda b, i: (b, 0, 0))
    return pl.pallas_call(
        functools.partial(_lru_kernel, tb=tb, t_last=t_last),
        out_shape=[jax.ShapeDtypeStruct((nb * t, w), F32), jax.ShapeDtypeStruct((nb, 1, w), F32),
                   jax.ShapeDtypeStruct((nb, CONV_W - 1, w), F32)],
        grid=(nb, nt),
        in_specs=[row, row, per_b(SUBLANES), per_b(1)] + [_const_spec(c.shape) for c in consts],
        out_specs=[row, per_b(1), per_b(CONV_W - 1)],
        scratch_shapes=[pltpu.VMEM((tb + 2 * SUBLANES, w), F32), pltpu.VMEM((tb, w), F32), pltpu.VMEM((tb, w), F32),
                        pltpu.VMEM((tb, w), F32), pltpu.VMEM((1, w), F32)],
        compiler_params=_params("parallel", "arbitrary"),
        name="lru",
    )(xb, gg, buf0, h0, *consts)


def _mix_out_kernel(a_ref, b_ref, x_ref, wa_ref, wb_ref, g_ref, beta_ref, o_ref):
    m = _dot(a_ref[...], wa_ref[...]) + _dot(b_ref[...], wb_ref[...])
    o_ref[...] = _layer_norm_rows(DN_ALPHA * x_ref[...] + m, g_ref[...], beta_ref[...])


def _mix_out(a, b, x, wa, wb, g, beta, *, tm=512):
    n, d = x.shape
    row = lambda w: pl.BlockSpec((tm, w), lambda i: (i, 0))
    consts = [wa, wb, g, beta]
    return pl.pallas_call(
        _mix_out_kernel,
        out_shape=jax.ShapeDtypeStruct((n, d), F32),
        grid=(n // tm,),
        in_specs=[row(a.shape[1]), row(b.shape[1]), row(d)] + [_const_spec(c.shape) for c in consts],
        out_specs=row(d),
        compiler_params=_params("parallel"),
        name="mix_out",
    )(a, b, x, *consts)


def _pad_heads(w, heads, width):
    lead = w.shape[:-1]
    w = w.reshape(lead + (heads, width))
    w = jnp.pad(w, [(0, 0)] * len(lead) + [(0, 0), (0, LANES - width)])
    return w.reshape(lead + (heads * LANES,))


def _even_weights(w_in, w_gate, b_gate, gla_norm, conv_w, conv_b, lru_wa, lru_ba, lru_wx, lru_bx, lru_lambda, w_out):
    gk = GLA_HEADS * GLA_DK
    gv = GLA_HEADS * GLA_DV
    o = 0
    cols = {}
    for name, width in (("q", gk), ("k", gk), ("v", gv), ("lr", GLA_RANK), ("r", gv), ("xb", LRU_WIDTH), ("gb", LRU_WIDTH)):
        cols[name] = w_in[:, o:o + width]
        o += width
    wlr = jnp.pad(cols["lr"], ((0, 0), (0, LANES - GLA_RANK)))
    wg = jnp.pad(_pad_heads(w_gate, GLA_HEADS, GLA_DK), ((0, LANES - GLA_RANK), (0, 0)))
    bg = _pad_heads(b_gate[None], GLA_HEADS, GLA_DK)
    in_consts = [_pad_heads(cols["q"], GLA_HEADS, GLA_DK).astype(BF16), _pad_heads(cols["k"], GLA_HEADS, GLA_DK).astype(BF16),
                 cols["v"].astype(BF16), cols["r"].astype(BF16), cols["xb"].astype(BF16), cols["gb"].astype(BF16),
                 wlr.astype(BF16), wg.astype(BF16), bg]
    eye = jnp.eye(LRU_BLOCKS, dtype=F32)
    bd = lambda w: (eye[:, None, :, None] * w[:, :, None, :]).reshape(LRU_WIDTH, LRU_WIDTH).astype(BF16)
    lru_consts = [conv_w, conv_b[None], bd(lru_wa), lru_ba[None], bd(lru_wx), lru_bx[None], lru_lambda[None]]
    return in_consts, gla_norm[None], lru_consts, w_out[:gv].astype(BF16), w_out[gv:].astype(BF16)


def _seq_rows(a, off, nb, t, t_pad):
    a = a[off:off + nb * t].reshape(nb, t, a.shape[1])
    return jnp.pad(a, ((0, 0), (0, t_pad - t), (0, 0))).reshape(nb * t_pad, a.shape[2])


def _unpad_rows(a, nb, t, t_pad):
    return a.reshape(nb, t_pad, a.shape[1])[:, :t].reshape(nb * t, a.shape[1])


def _even_layer(x, groups, weights, g, beta, *, tm):
    in_consts, gn, lru_consts, wo_a, wo_b = weights
    qkl, v, rs, xb, gg = _even_in(x, in_consts, tm=tm)
    ogs, yls, states = [], [], []
    for off, nb, t, tb, s_gla, h_lru, cbuf in groups:
        s0t = jnp.pad(jnp.swapaxes(s_gla, 2, 3), ((0, 0), (0, 0), (0, 0), (0, LANES - GLA_DK)))
        buf0 = jnp.pad(cbuf, ((0, 0), (SUBLANES - (CONV_W - 1), 0), (0, 0)))
        if tb is None:
            tg = GLA_CHUNK
            tl = SUBLANES
            og, st = _gla(_seq_rows(qkl, off, nb, t, tg), _seq_rows(v, off, nb, t, tg), _seq_rows(rs, off, nb, t, tg),
                          s0t, gn, nb=nb, t=tg, tb=tg)
            yl, hl, cb = _lru(_seq_rows(xb, off, nb, t, tl), _seq_rows(gg, off, nb, t, tl), buf0, h_lru[:, None],
                              lru_consts, nb=nb, t=tl, tb=tl, t_last=t - 1)
            og = _unpad_rows(og, nb, t, tg)
            yl = _unpad_rows(yl, nb, t, tl)
        else:
            og, st = _gla(qkl, v, rs, s0t, gn, nb=nb, t=t, tb=GLA_TIME_BLOCK)
            yl, hl, cb = _lru(xb, gg, buf0, h_lru[:, None], lru_consts, nb=nb, t=t, tb=tb, t_last=tb - 1)
        ogs.append(og)
        yls.append(yl)
        states.append((jnp.swapaxes(st[..., :GLA_DK], 2, 3), hl[:, 0], cb))
    n_used = sum(o.shape[0] for o in ogs)
    tail = jnp.zeros((x.shape[0] - n_used, LRU_WIDTH), F32)
    og = jnp.concatenate(ogs + [tail], axis=0)
    yl = jnp.concatenate(yls + [tail], axis=0)
    return _mix_out(og, yl, x, wo_a, wo_b, g, beta, tm=tm), states


RWKV_STREAMS = 5


def _rwkv_in_kernel(*refs, has_vres, tm, long_seqs, short_seqs):
    if has_vres:
        (x_ref, halo_ref, sl_ref, ov_ref, mu_ref, wr_ref, wk_ref, wv_ref, w1_ref, w2_ref, a1_ref, a2_ref, g1_ref, g2_ref,
         w0_ref, a0_ref, vf_ref, v1_ref, v2_ref, v0_ref, p_ref, g_ref, xs_ref) = refs
    else:
        (x_ref, halo_ref, sl_ref, ov_ref, mu_ref, wr_ref, wk_ref, wv_ref, w1_ref, w2_ref, a1_ref, a2_ref, g1_ref, g2_ref,
         w0_ref, a0_ref, p_ref, g_ref, xs_ref) = refs
    d = D_MODEL
    x = x_ref[...]
    r0 = pl.program_id(0) * tm
    xs_ref[0:SUBLANES] = halo_ref[...]
    xs_ref[SUBLANES:SUBLANES + tm] = x
    prev = xs_ref[pl.ds(SUBLANES - 1, tm), :]
    rows = r0 + lax.broadcasted_iota(jnp.int32, (tm, 1), 0)
    off, cnt, length = long_seqs
    b = (jnp.maximum(r0 - off, 0) + (length - 1)) // length
    hit = jnp.logical_and(rows == off + b * length, b < cnt)
    prev = jnp.where(hit, sl_ref[pl.ds(jnp.minimum(b, cnt - 1), 1), :], prev)
    off, cnt, length = short_seqs
    rel = rows - off
    first = jnp.logical_and(jnp.logical_and(rel >= 0, rel < cnt * length), jnp.bitwise_and(rel, length - 1) == 0)
    prev = jnp.where(first, ov_ref[...], prev)
    xx = prev - x
    mix = lambda i: (x + xx * mu_ref[i:i + 1, :]).astype(BF16)
    xr, xw, xk, xv, xa, xg = (mix(i) for i in range(6))
    p_ref[:, 0:d] = _dot(xr, wr_ref[...])
    p_ref[:, d:2 * d] = _dot(xk, wk_ref[...])
    v = _dot(xv, wv_ref[...])
    if has_vres:
        v = v + (vf_ref[...] - v) * _sigmoid(v0_ref[...] + _dot(_dot(xv, v1_ref[...]), v2_ref[...]))
    p_ref[:, 2 * d:3 * d] = v
    p_ref[:, 3 * d:4 * d] = w0_ref[...] + _dot(jnp.tanh(_dot(xw, w1_ref[...])), w2_ref[...])
    p_ref[:, 4 * d:5 * d] = a0_ref[...] + _dot(_dot(xa, a1_ref[...]), a2_ref[...])
    g_ref[...] = _dot(_sigmoid(_dot(xg, g1_ref[...])), g2_ref[...])


def _rwkv_in(x, shift_long, shift_short, consts, vres, *, tm, long_seqs, short_seqs):
    n, d = x.shape
    off, cnt, length = short_seqs
    assert length & (length - 1) == 0 and long_seqs[2] >= tm and tm % SUBLANES == 0
    tile0 = off // tm
    tiles = -(-(off + cnt * length) // tm) - tile0
    ov = jnp.pad(shift_short[:, None], ((0, 0), (0, length - 1), (0, 0))).reshape(cnt * length, d)
    ov = jnp.pad(ov, ((off - tile0 * tm, tiles * tm - (off - tile0 * tm) - cnt * length), (0, 0)))
    sl = jnp.pad(shift_long, ((0, -shift_long.shape[0] % SUBLANES), (0, 0)))
    row = lambda w: pl.BlockSpec((tm, w), lambda i: (i, 0))
    halo = pl.BlockSpec((SUBLANES, d), lambda i: (jnp.maximum(i * (tm // SUBLANES) - 1, 0), 0))
    ov_spec = pl.BlockSpec((tm, d), lambda i: (jnp.clip(i - tile0, 0, tiles - 1), 0))
    args = [x, x, sl, ov] + list(consts)
    specs = [row(d), halo, _const_spec(sl.shape), ov_spec] + [_const_spec(c.shape) for c in consts]
    if vres is not None:
        vf, vconsts = vres
        args += [vf] + list(vconsts)
        specs += [row(d)] + [_const_spec(c.shape) for c in vconsts]
    return pl.pallas_call(
        functools.partial(_rwkv_in_kernel, has_vres=vres is not None, tm=tm, long_seqs=long_seqs, short_seqs=short_seqs),
        out_shape=[jax.ShapeDtypeStruct((n, RWKV_STREAMS * d), F32), jax.ShapeDtypeStruct((n, d), F32)],
        grid=(n // tm,),
        in_specs=specs,
        out_specs=[row(RWKV_STREAMS * d), row(d)],
        scratch_shapes=[pltpu.VMEM((tm + SUBLANES, d), F32)],
        compiler_params=_params("parallel"),
        name="rwkv_in",
    )(*args)


def _wkv_kernel(p_ref, s0_ref, kk_ref, ka_ref, rk_ref, lg_ref, lb_ref, y_ref, sout_ref,
                s_ref, dec_ref, a_ref, b_ref, km_ref, bon_ref, *, tt):
    t = pl.program_id(1)
    hk = RWKV_HEAD

    @pl.when(t == 0)
    def _init():
        s_ref[...] = s0_ref[...]

    r = p_ref[:, 0]
    k = p_ref[:, 1]
    v = p_ref[:, 2]
    dec_ref[...] = jnp.exp(-jnp.exp(-_softplus(-p_ref[:, 3]) - 0.5))
    ag = _sigmoid(p_ref[:, 4])
    kk = k * kk_ref[...][None]
    kk = kk / jnp.maximum(jnp.sqrt(jnp.sum(kk * kk, axis=1, keepdims=True)), 1e-12)
    km = k * (1.0 + (ag - 1.0) * ka_ref[...][None])
    km_ref[...] = km
    a_ref[...] = -kk
    b_ref[...] = kk * ag
    bon_ref[...] = jnp.sum(r * km * rk_ref[...][None], axis=1, keepdims=True) * v
    lg = lg_ref[...]
    lb = lb_ref[...]

    def step(i, carry):
        lanes = 4
        parts = [jnp.zeros((hk, LANES), F32) for _ in range(lanes)]
        for q in range(hk):
            parts[q % lanes] = parts[q % lanes] + s_ref[q] * a_ref[i, q:q + 1, :]
        sa = (parts[0] + parts[1]) + (parts[2] + parts[3])
        vv = p_ref[i, 2]
        parts = [jnp.zeros((hk, LANES), F32) for _ in range(lanes)]
        for q in range(hk):
            sn = s_ref[q] * dec_ref[i, q:q + 1, :] + (sa * b_ref[i, q:q + 1, :] + vv * km_ref[i, q:q + 1, :])
            s_ref[q] = sn
            parts[q % lanes] = parts[q % lanes] + sn * p_ref[i, 0, q:q + 1, :]
        y = (parts[0] + parts[1]) + (parts[2] + parts[3])
        mu = jnp.mean(y, axis=0, keepdims=True)
        dlt = y - mu
        var = jnp.mean(dlt * dlt, axis=0, keepdims=True)
        y_ref[i] = dlt * lax.rsqrt(var + RWKV_GN_EPS) * lg + lb + bon_ref[i]
        return carry

    lax.fori_loop(0, tt, step, 0)

    @pl.when(t == pl.num_programs(1) - 1)
    def _fin():
        sout_ref[...] = s_ref[...]


def _wkv(p, s0, lane_consts, *, tt):
    t, _, hk, l = p.shape
    lane2 = pl.BlockSpec((hk, LANES), lambda g, i: (0, g))
    st = pl.BlockSpec((hk, hk, LANES), lambda g, i: (0, 0, g))
    seq = lambda: pltpu.VMEM((tt, hk, LANES), F32)
    return pl.pallas_call(
        functools.partial(_wkv_kernel, tt=tt),
        out_shape=[jax.ShapeDtypeStruct((t, hk, l), F32), jax.ShapeDtypeStruct((hk, hk, l), F32)],
        grid=(l // LANES, t // tt),
        in_specs=[pl.BlockSpec((tt, RWKV_STREAMS, hk, LANES), lambda g, i: (i, 0, 0, g)), st] + [lane2] * 5,
        out_specs=[pl.BlockSpec((tt, hk, LANES), lambda g, i: (i, 0, g)), st],
        scratch_shapes=[pltpu.VMEM((hk, hk, LANES), F32), seq(), seq(), seq(), seq(), seq()],
        compiler_params=_params("parallel", "arbitrary"),
        name="wkv",
    )(p, s0, *lane_consts)


def _gate_out_kernel(y_ref, g_ref, x_ref, w_ref, lg_ref, lb_ref, o_ref):
    m = _dot(y_ref[...] * g_ref[...], w_ref[...])
    o_ref[...] = _layer_norm_rows(DN_ALPHA * x_ref[...] + m, lg_ref[...], lb_ref[...])


def _gate_out(y, g, x, w, lg, lb, *, tm):
    n, d = x.shape
    row = pl.BlockSpec((tm, d), lambda i: (i, 0))
    consts = [w, lg, lb]
    return pl.pallas_call(
        _gate_out_kernel,
        out_shape=jax.ShapeDtypeStruct((n, d), F32),
        grid=(n // tm,),
        in_specs=[row, row, row] + [_const_spec(c.shape) for c in consts],
        out_specs=row,
        compiler_params=_params("parallel"),
        name="gate_out",
    )(y, g, x, *consts)


def _pad_cols(w):
    return jnp.pad(w, ((0, 0), (0, LANES - w.shape[1]))).astype(BF16)


def _pad_rows(w):
    return jnp.pad(w, ((0, LANES - w.shape[0]), (0, 0))).astype(BF16)


def _rwkv_weights(mu, w_r, w_k, w_v, w_o, w0, w1, w2, a0, a1, a2, g1, g2, k_k, k_a, r_k, lnx_g, lnx_b, vres):
    consts = [jnp.pad(mu, ((0, SUBLANES - mu.shape[0]), (0, 0))), w_r.astype(BF16), w_k.astype(BF16), w_v.astype(BF16),
              _pad_cols(w1), _pad_rows(w2), _pad_cols(a1), _pad_rows(a2), _pad_cols(g1), _pad_rows(g2), w0[None], a0[None]]
    vconsts = None if vres is None else [_pad_cols(vres[1]), _pad_rows(vres[2]), vres[0][None]]
    per_head = [k_k.reshape(RWKV_HEADS, RWKV_HEAD).T, k_a.reshape(RWKV_HEADS, RWKV_HEAD).T, r_k.T,
                lnx_g.reshape(RWKV_HEADS, RWKV_HEAD).T, lnx_b.reshape(RWKV_HEADS, RWKV_HEAD).T]
    return consts, vconsts, per_head, w_o.astype(BF16)


def _rwkv_layer(x, groups, weights, v_first, g, beta, *, tm):
    consts, vconsts, per_head, w_o = weights
    n, d = x.shape
    (off_l, nb_l, t_l, _, _, shift_l), (off_s, nb_s, t_s, _, _, shift_s) = groups
    n_used = nb_l * t_l + nb_s * t_s
    vres = None if vconsts is None else (v_first, vconsts)
    p, gate = _rwkv_in(x, shift_l, shift_s, consts, vres, tm=tm,
                       long_seqs=(off_l, nb_l, t_l), short_seqs=(off_s, nb_s, t_s))
    ys, states = [], []
    for off, nb, t, tt, s, shift in groups:
        lanes = nb * RWKV_HEADS
        pg = p[off:off + nb * t].reshape(nb, t, RWKV_STREAMS, RWKV_HEADS, RWKV_HEAD)
        pg = jnp.transpose(pg, (1, 2, 4, 0, 3)).reshape(t, RWKV_STREAMS, RWKV_HEAD, lanes)
        hk2 = RWKV_HEAD * RWKV_HEAD
        s0 = jnp.swapaxes(s, 2, 3).reshape(lanes, hk2).T.reshape(RWKV_HEAD, RWKV_HEAD, lanes)
        lane_consts = [jnp.tile(c, (1, nb)) for c in per_head]
        y, s_out = _wkv(pg, s0, lane_consts, tt=tt)
        yt = jnp.swapaxes(y, 1, 2).reshape(t, nb, d)
        ys.append(jnp.swapaxes(yt, 0, 1).reshape(nb * t, d))
        s_new = jnp.swapaxes(s_out.reshape(hk2, lanes).T.reshape(nb, RWKV_HEADS, RWKV_HEAD, RWKV_HEAD), 2, 3)
        states.append((s_new, x[off:off + nb * t].reshape(nb, t, d)[:, -1]))
    y = jnp.concatenate(ys + [jnp.zeros((n - n_used, d), F32)], axis=0)
    return _gate_out(y, gate, x, w_o, g, beta, tm=tm), states, p[:, 2 * d:3 * d]


TOKEN_TILE = 512
PEER_TOKEN_TILE = 1024
PROMPT_TIME_BLOCK = 688
WKV_TIME_BLOCK = 16


def kernel(x_prompt, x_sample, state_gla, state_lru_h, state_lru_conv, state_rwkv, state_rwkv_shift, meta_tokens, ln_g, ln_b, ev_w_in, ev_gla_w_gate, ev_gla_b_gate, ev_gla_norm, ev_conv_w, ev_conv_b, ev_lru_wa, ev_lru_ba, ev_lru_wx, ev_lru_bx, ev_lru_lambda, ev_w_out, od_mu, od_w_r, od_w_k, od_w_v, od_w_o, od_w0, od_w1, od_w2, od_a0, od_a1, od_a2, od_v0, od_v1, od_v2, od_g1, od_g2, od_k_k, od_k_a, od_r_k, od_lnx_g, od_lnx_b, peer_w_q, peer_keys, peer_u, peer_v):
    bp, sp, d = x_prompt.shape
    bs, ss, _ = x_sample.shape
    tp = sp + N_META
    n_p, n_s = bp * tp, bs * ss
    n = -(-(n_p + n_s) // PEER_TOKEN_TILE) * PEER_TOKEN_TILE
    assert tp % PROMPT_TIME_BLOCK == 0 and tp % GLA_TIME_BLOCK == 0 and tp % WKV_TIME_BLOCK == 0 and n % TOKEN_TILE == 0

    xp = jnp.concatenate([jnp.broadcast_to(meta_tokens[None], (bp, N_META, d)), x_prompt], axis=1)
    x = jnp.concatenate([xp.reshape(n_p, d), x_sample.reshape(n_s, d), jnp.zeros((n - n_p - n_s, d), F32)], axis=0)

    n_pairs = DEPTH // 2
    zeros = lambda *s: jnp.zeros(s, F32)
    outs = {k: [] for k in ("p_gla", "p_h", "p_conv", "p_rwkv", "p_shift", "s_gla", "s_h", "s_conv", "s_rwkv", "s_shift")}
    v_first = None
    for layer in range(DEPTH):
        j = layer // 2
        g, beta = ln_g[layer, 0][None], ln_b[layer, 0][None]
        if layer % 2 == 0:
            weights = _even_weights(ev_w_in[j], ev_gla_w_gate[j], ev_gla_b_gate[j], ev_gla_norm[j], ev_conv_w[j], ev_conv_b[j],
                                    ev_lru_wa[j], ev_lru_ba[j], ev_lru_wx[j], ev_lru_bx[j], ev_lru_lambda[j], ev_w_out[j])
            groups = [(0, bp, tp, PROMPT_TIME_BLOCK, zeros(bp, GLA_HEADS, GLA_DK, GLA_DV), zeros(bp, LRU_WIDTH),
                       zeros(bp, CONV_W - 1, LRU_WIDTH)),
                      (n_p, bs, ss, None, state_gla[j], state_lru_h[j], state_lru_conv[j])]
            x, states = _even_layer(x, groups, weights, g, beta, tm=TOKEN_TILE)
            for pre, st in zip("ps", states):
                outs[pre + "_gla"].append(st[0])
                outs[pre + "_h"].append(st[1])
                outs[pre + "_conv"].append(st[2])
        else:
            vres = None if j == 0 else (od_v0[j - 1], od_v1[j - 1], od_v2[j - 1])
            weights = _rwkv_weights(od_mu[j], od_w_r[j], od_w_k[j], od_w_v[j], od_w_o[j], od_w0[j], od_w1[j], od_w2[j],
                                    od_a0[j], od_a1[j], od_a2[j], od_g1[j], od_g2[j], od_k_k[j], od_k_a[j], od_r_k[j],
                                    od_lnx_g[j], od_lnx_b[j], vres)
            groups = [(0, bp, tp, WKV_TIME_BLOCK, zeros(bp, RWKV_HEADS, RWKV_HEAD, RWKV_HEAD), zeros(bp, d)),
                      (n_p, bs, ss, ss, state_rwkv[j], state_rwkv_shift[j])]
            x, states, v = _rwkv_layer(x, groups, weights, v_first, g, beta, tm=TOKEN_TILE)
            if v_first is None:
                v_first = v
            for pre, st in zip("ps", states):
                outs[pre + "_rwkv"].append(st[0])
                outs[pre + "_shift"].append(st[1])
        x = _peer_layer(x, peer_w_q[layer].T.astype(BF16), peer_keys[layer, :, 0].astype(BF16),
                        peer_keys[layer, :, 1].astype(BF16), peer_u[layer].astype(BF16), peer_v[layer].astype(BF16),
                        ln_g[layer, 1][None], ln_b[layer, 1][None], tn=PEER_TOKEN_TILE)

    y_prompt = x[:n_p].reshape(bp, tp, d)[:, N_META:]
    y_sample = x[n_p:n_p + n_s].reshape(bs, ss, d)
    st = {k: jnp.stack(v) for k, v in outs.items()}
    return (y_prompt, y_sample, st["p_gla"], st["p_h"], st["p_conv"], st["p_rwkv"], st["p_shift"],
            st["s_gla"], st["s_h"], st["s_conv"], st["s_rwkv"], st["s_shift"])
```

```python
import functools

import jax
import jax.numpy as jnp
from jax import lax
from jax.experimental import pallas as pl
from jax.experimental.pallas import tpu as pltpu

F32 = jnp.float32
BF16 = jnp.bfloat16

D_MODEL = 1024
DEPTH = 4
N_META = 16
GLA_HEADS = 4
GLA_DK = 64
GLA_DV = 128
GLA_RANK = 16
GLA_GATE_NORM = 16.0
GLA_CHUNK = 16
LRU_WIDTH = 512
LRU_BLOCKS = 8
CONV_W = 4
LRU_C = 8.0
RWKV_HEAD = 64
RWKV_HEADS = D_MODEL // RWKV_HEAD
RWKV_GN_EPS = 64e-5
PEER_HEADS = 8
PEER_NKEYS = 128
PEER_DKEY = 256
PEER_HALF = PEER_DKEY // 2
PEER_TOPK = 16
DN_ALPHA = float((2 * DEPTH) ** 0.25)
LN_EPS = 1e-5

LANES = 128
SUBLANES = 8
VMEM_LIMIT = 56 * 1024 * 1024


def _dot(a, b):
    return jnp.dot(a.astype(BF16), b.astype(BF16), preferred_element_type=F32)


def _dot_nt(a, b):
    return lax.dot_general(a.astype(BF16), b.astype(BF16), (((1,), (1,)), ((), ())), preferred_element_type=F32)


def _dot_tn(a, b):
    return lax.dot_general(a.astype(BF16), b.astype(BF16), (((0,), (0,)), ((), ())), preferred_element_type=F32)


def _layer_norm_rows(z, g, b):
    mu = jnp.mean(z, axis=-1, keepdims=True)
    d = z - mu
    var = jnp.mean(d * d, axis=-1, keepdims=True)
    return d * lax.rsqrt(var + LN_EPS) * g + b


def _sigmoid(x):
    return 1.0 / (1.0 + jnp.exp(-x))


def _softplus(x):
    return jnp.maximum(x, 0.0) + jnp.log1p(jnp.exp(-jnp.abs(x)))


def _gelu_tanh(x):
    c0 = 0.7978845608028654
    c1 = c0 * 0.044715
    one = jnp.asarray(1.0, x.dtype)
    return (0.5 * x) * (one + jnp.tanh(x * (c0 + c1 * (x * x))))


def _silu(x):
    return x * _sigmoid(x)


def _params(*sem):
    return pltpu.CompilerParams(dimension_semantics=sem, vmem_limit_bytes=VMEM_LIMIT)


def _const_spec(shape):
    nd = len(shape)
    return pl.BlockSpec(shape, lambda *_: (0,) * nd)


PEER_CAND_ROWS = 80
PEER_PACK = 2 * SUBLANES
PEER_SELECT_LANES = 2 * LANES


def _peer_cand_index():
    rows = [r1 * PEER_TOPK for r1 in range(16)]
    for j in range(1, 8):
        rows += [r1 * PEER_TOPK + j for r1 in range(8)]
    rows += list(range(8, 16))
    return jnp.broadcast_to(jnp.asarray(rows, F32)[:, None], (PEER_CAND_ROWS, PEER_SELECT_LANES))


PEER_CODE_STEP = 2.0 ** 116
PEER_CODED_BELOW = -(2.0 ** 119)


def _rank_code(r):
    return -PEER_CODE_STEP * (PEER_TOPK + r)


def _decode_rank(s):
    return jnp.where(s < PEER_CODED_BELOW, s * (-1.0 / PEER_CODE_STEP) - PEER_TOPK, float(PEER_TOPK))


def _count_coded(s):
    return jnp.sum(jnp.where(s < PEER_CODED_BELOW, 1.0, 0.0), axis=0, keepdims=True)


def _top16(s, key_iota):
    row16 = lax.broadcasted_iota(jnp.int32, (PEER_TOPK, s.shape[1]), 0)
    vals = jnp.zeros((PEER_TOPK, s.shape[1]), F32)
    for r in range(PEER_TOPK):
        m = jnp.max(s, axis=0, keepdims=True)
        hit = s == m
        if key_iota is not None:
            first = jnp.min(jnp.where(hit, key_iota, float(PEER_NKEYS)), axis=0, keepdims=True)
            hit = key_iota == first
        s = jnp.where(hit, _rank_code(r), s)
        vals = jnp.where(row16 == r, m, vals)
    return vals, s


def _peer_candidates(a, b):
    blocks = [a + b[0:1]]
    for j in range(1, 8):
        blocks.append(a[0:8] + b[j:j + 1])
    blocks.append(a[0:1] + b[8:16])
    return jnp.concatenate(blocks, axis=0)


def _select16(c, cidx):
    for _ in range(PEER_TOPK):
        m = jnp.max(c, axis=0, keepdims=True)
        hit = c == m
        if cidx is not None:
            first = jnp.min(jnp.where(hit, cidx, 1e9), axis=0, keepdims=True)
            hit = cidx == first
        c = jnp.where(hit, _rank_code(0), c)
    return c


def _peer_gate_tiles(s1, s2, a, b, s1c, s2c, c, cc):
    sel = jnp.where(cc < PEER_CODED_BELOW, 1.0, 0.0)
    z = jnp.sum(sel * jnp.exp(c - c[0:1]), axis=0, keepdims=True)
    j_lo = sel[0:8]
    for j in range(1, 8):
        j_lo = j_lo + sel[8 + 8 * j:16 + 8 * j]
    extra = jnp.sum(sel[72:80], axis=0, keepdims=True)
    row8 = lax.broadcasted_iota(jnp.int32, j_lo.shape, 0)
    j_lo = j_lo + jnp.where(row8 == 0, extra, 0.0)
    jt = jnp.concatenate([j_lo, sel[8:16]], axis=0)
    r1 = _decode_rank(s1c)
    h1 = jnp.zeros_like(r1)
    for r in range(PEER_TOPK):
        h1 = jnp.where(r1 == float(r), jt[r:r + 1], h1)
    return h1, jnp.exp(s1 - a[0:1]) / z, _decode_rank(s2c), jnp.exp(s2 - b[0:1])


def _peer_kernel(x_ref, wqt_ref, k1_ref, k2_ref, cidx_ref, u_ref, vt_ref, g_ref, b_ref, o_ref,
                 xt_ref, acc_ref, h1_ref, c1_ref, r2_ref, e2_ref, s_ref, hd0_ref, hd1_ref, coef_ref, *, tn, te):
    j = pl.program_id(1)
    nj = pl.num_programs(1)

    @pl.when(j == 0)
    def _select():
        xt_ref[...] = x_ref[...].T.astype(BF16)
        acc_ref[...] = jnp.zeros_like(acc_ref)
        hd1_ref[...] = jnp.zeros_like(hd1_ref)
        coef_ref[0:te] = jnp.zeros((te, tn), BF16)
        key_iota = lax.broadcasted_iota(jnp.int32, (PEER_NKEYS, PEER_SELECT_LANES), 0).astype(F32)

        def head(h, carry):
            q = _dot(wqt_ref[pl.ds(pl.multiple_of(h * PEER_DKEY, PEER_DKEY), PEER_DKEY), :], xt_ref[...])
            mu = jnp.mean(q, axis=0, keepdims=True)
            d = q - mu
            qn = d * lax.rsqrt(jnp.mean(d * d, axis=0, keepdims=True) + LN_EPS)
            s_ref[0] = _dot(k1_ref[h], qn[0:PEER_HALF])
            s_ref[1] = _dot(k2_ref[h], qn[PEER_HALF:PEER_DKEY])

            def chunk(ci, carry2):
                ls = pl.ds(pl.multiple_of(ci * PEER_SELECT_LANES, PEER_SELECT_LANES), PEER_SELECT_LANES)
                s1 = s_ref[0, :, ls]
                s2 = s_ref[1, :, ls]

                def put(tiles):
                    h1_ref[h, :, ls] = tiles[0]
                    c1_ref[h, :, ls] = tiles[1]
                    r2_ref[h, :, ls] = tiles[2].astype(BF16)
                    e2_ref[h, :, ls] = tiles[3].astype(BF16)

                a, s1c = _top16(s1, None)
                b, s2c = _top16(s2, None)
                c = _peer_candidates(a, b)
                cc = _select16(c, None)
                put(_peer_gate_tiles(s1, s2, a, b, s1c, s2c, c, cc))
                miscount = (jnp.abs(_count_coded(s1c) - PEER_TOPK) + jnp.abs(_count_coded(s2c) - PEER_TOPK)
                            + jnp.abs(_count_coded(cc) - PEER_TOPK))

                @pl.when(jnp.max(miscount) > 0.0)
                def _with_ties():
                    a, s1c = _top16(s1, key_iota)
                    b, s2c = _top16(s2, key_iota)
                    c = _peer_candidates(a, b)
                    cc = _select16(c, cidx_ref[...])
                    put(_peer_gate_tiles(s1, s2, a, b, s1c, s2c, c, cc))

                return carry2

            lax.fori_loop(0, tn // PEER_SELECT_LANES, chunk, 0)
            return carry

        lax.fori_loop(0, PEER_HEADS, head, 0)

    groups = PEER_NKEYS // PEER_PACK
    last_key = PEER_NKEYS - 1
    zero_pack = jnp.zeros((PEER_PACK, LANES), F32)

    def gate_rows(tile):
        rows = []
        for ii in range(te // PEER_NKEYS):
            i1 = jnp.clip(tile * (te // PEER_NKEYS) + ii, 0, last_key)
            rows.append(([h1_ref[h, pl.ds(i1, 1), :] for h in range(PEER_HEADS)],
                         [c1_ref[h, pl.ds(i1, 1), :] for h in range(PEER_HEADS)]))
        return rows

    def gate_stage(rows, hd_ref, coef_rows, lane_groups):
        nk = len(rows)
        for c in lane_groups:
            ls = slice(c * LANES, (c + 1) * LANES)
            gates = [[jnp.zeros((PEER_PACK, LANES), BF16) for _ in range(groups)] for _ in range(nk)]
            for h in range(PEER_HEADS):
                bounds = [(rows[ii][0][h][:, ls] + zero_pack).astype(BF16) for ii in range(nk)]
                scales = [(rows[ii][1][h][:, ls] + zero_pack).astype(BF16) for ii in range(nk)]
                for gi in range(groups):
                    krows = slice(gi * PEER_PACK, (gi + 1) * PEER_PACK)
                    rank = r2_ref[h, krows, ls]
                    fac = e2_ref[h, krows, ls]
                    for ii in range(nk):
                        gates[ii][gi] = gates[ii][gi] + jnp.where(rank < bounds[ii], fac * scales[ii], jnp.zeros((), BF16))
            for ii in range(nk):
                for gi in range(groups):
                    erows = slice(ii * PEER_NKEYS + gi * PEER_PACK, ii * PEER_NKEYS + (gi + 1) * PEER_PACK)
                    crows = slice(coef_rows + erows.start, coef_rows + erows.stop)
                    coef_ref[crows, ls] = gates[ii][gi] * _gelu_tanh(hd_ref[erows, ls].astype(BF16))

    piece = 2 * LANES
    per = piece // LANES
    rows_odd = gate_rows(2 * j - 1)
    rows_even = gate_rows(2 * j)
    pieces = tn // piece
    lanes_of = lambda c: slice(c * piece, (c + 1) * piece)

    def down_project(c):
        acc_ref[:, lanes_of(c)] += jnp.dot(vt_ref[0], coef_ref[:, lanes_of(c)], preferred_element_type=F32)

    for c in range(pieces):
        gate_stage(rows_odd, hd1_ref, te, range(c * per, (c + 1) * per))
        hd0_ref[:, lanes_of(c)] = _dot(u_ref[0:te, :], xt_ref[:, lanes_of(c)])
        hd1_ref[:, lanes_of(c)] = _dot(u_ref[te:2 * te, :], xt_ref[:, lanes_of(c)])
        if c > 0:
            down_project(c - 1)
    down_project(pieces - 1)
    for c in range(pieces):
        gate_stage(rows_even, hd0_ref, 0, range(c * per, (c + 1) * per))

    @pl.when(j == nj - 1)
    def _finish():
        y = acc_ref[...].T
        o_ref[...] = _layer_norm_rows(DN_ALPHA * x_ref[...] + y, g_ref[...], b_ref[...])


def _peer_layer(x, wqt, k1, k2, u, v, g, b, *, tn=1024, te=256):
    n, d = x.shape
    ne = u.shape[0]
    assert ne % (2 * te) == 0 and n % tn == 0
    pairs = ne // (2 * te)
    steps = pairs + 1
    vt = jnp.swapaxes(v.reshape(pairs, 2 * te, d), 1, 2)
    kern = functools.partial(_peer_kernel, tn=tn, te=te)
    one = pl.Buffered(1)
    return pl.pallas_call(
        kern,
        out_shape=jax.ShapeDtypeStruct((n, d), F32),
        grid=(n // tn, steps),
        in_specs=[
            pl.BlockSpec((tn, d), lambda i, j: (i, 0), pipeline_mode=one),
            pl.BlockSpec(wqt.shape, lambda i, j: (0, 0), pipeline_mode=one),
            pl.BlockSpec(k1.shape, lambda i, j: (0, 0, 0), pipeline_mode=one),
            pl.BlockSpec(k2.shape, lambda i, j: (0, 0, 0), pipeline_mode=one),
            pl.BlockSpec((PEER_CAND_ROWS, PEER_SELECT_LANES), lambda i, j: (0, 0), pipeline_mode=one),
            pl.BlockSpec((2 * te, d), lambda i, j: (jnp.minimum(j, pairs - 1), 0)),
            pl.BlockSpec((1, d, 2 * te), lambda i, j: (jnp.maximum(j - 1, 0), 0, 0)),
            pl.BlockSpec((1, d), lambda i, j: (0, 0), pipeline_mode=one),
            pl.BlockSpec((1, d), lambda i, j: (0, 0), pipeline_mode=one),
        ],
        out_specs=pl.BlockSpec((tn, d), lambda i, j: (i, 0), pipeline_mode=one),
        scratch_shapes=[
            pltpu.VMEM((d, tn), BF16),
            pltpu.VMEM((d, tn), F32),
            pltpu.VMEM((PEER_HEADS, PEER_NKEYS, tn), F32),
            pltpu.VMEM((PEER_HEADS, PEER_NKEYS, tn), F32),
            pltpu.VMEM((PEER_HEADS, PEER_NKEYS, tn), BF16),
            pltpu.VMEM((PEER_HEADS, PEER_NKEYS, tn), BF16),
            pltpu.VMEM((2, PEER_NKEYS, tn), F32),
            pltpu.VMEM((te, tn), F32),
            pltpu.VMEM((te, tn), F32),
            pltpu.VMEM((2 * te, tn), BF16),
        ],
        compiler_params=_params("parallel", "arbitrary"),
        name="peer",
    )(x, wqt, k1, k2, _peer_cand_index(), u, vt, g, b)


GLA_PAD = GLA_HEADS * LANES


def _even_in_kernel(x_ref, wq_ref, wk_ref, wv_ref, wr_ref, wxb_ref, wgb_ref, wlr_ref, wg_ref, bg_ref,
                    qkl_ref, v_ref, rs_ref, xb_ref, gg_ref):
    xb16 = x_ref[...].astype(BF16)
    qkl_ref[:, 0:GLA_PAD] = _dot(xb16, wq_ref[...]) * (GLA_DK ** -0.5)
    qkl_ref[:, GLA_PAD:2 * GLA_PAD] = _dot(xb16, wk_ref[...])
    glr = _dot(xb16, wlr_ref[...])
    z = _dot(glr, wg_ref[...]) + bg_ref[...]
    qkl_ref[:, 2 * GLA_PAD:3 * GLA_PAD] = -_softplus(-z) * (1.0 / GLA_GATE_NORM)
    v_ref[...] = _dot(xb16, wv_ref[...])
    rs_ref[...] = _silu(_dot(xb16, wr_ref[...]))
    xb_ref[...] = _dot(xb16, wxb_ref[...])
    gg_ref[...] = _gelu_tanh(_dot(xb16, wgb_ref[...]))


def _even_in(x, consts, *, tm=512):
    n, d = x.shape
    row = lambda w: pl.BlockSpec((tm, w), lambda i: (i, 0))
    return pl.pallas_call(
        _even_in_kernel,
        out_shape=[jax.ShapeDtypeStruct((n, 3 * GLA_PAD), F32)] + [jax.ShapeDtypeStruct((n, LRU_WIDTH), F32)] * 4,
        grid=(n // tm,),
        in_specs=[row(d)] + [_const_spec(c.shape) for c in consts],
        out_specs=[row(3 * GLA_PAD)] + [row(LRU_WIDTH)] * 4,
        compiler_params=_params("parallel"),
        name="even_in",
    )(x, *consts)


GLA_SEQS_PER_STEP = 4
GLA_TIME_BLOCK = 3 * GLA_CHUNK


def _gla_kernel(*refs, tb, ns):
    qkl_refs, v_refs, rs_refs, s0_refs = (refs[i * ns:(i + 1) * ns] for i in range(4))
    tri_ref, gn_ref = refs[4 * ns:4 * ns + 2]
    o_refs = refs[4 * ns + 2:5 * ns + 2]
    sout_refs = refs[5 * ns + 2:6 * ns + 2]
    st_ref = refs[6 * ns + 2]
    t = pl.program_id(1)

    @pl.when(t == 0)
    def _init():
        for p in range(ns):
            st_ref[p] = s0_refs[p][0]

    tri = tri_ref[...]
    causal = tri > 0
    gn = gn_ref[...]

    def chunk(c, carry):
        rows = pl.ds(pl.multiple_of(c * GLA_CHUNK, GLA_CHUNK), GLA_CHUNK)
        seqs = range(ns)
        pairs = [(p, h) for p in seqs for h in range(GLA_HEADS)]
        sl = lambda h: slice(h * LANES, (h + 1) * LANES)
        las = [qkl_refs[p][rows, 2 * GLA_PAD:3 * GLA_PAD] for p in seqs]
        his = [la.astype(BF16) for la in las]
        r1s = [la - hi.astype(F32) for la, hi in zip(las, his)]
        mids = [r1.astype(BF16) for r1 in r1s]
        los = [(r1 - mid.astype(F32)).astype(BF16) for r1, mid in zip(r1s, mids)]
        parts = [[jnp.dot(tri, part, preferred_element_type=F32) for part in (hi, mid, lo)]
                 for hi, mid, lo in zip(his, mids, los)]
        cums = [a + b + c3 for a, b, c3 in parts]
        lasts = [cum[GLA_CHUNK - 1:GLA_CHUNK] for cum in cums]
        qes = [qkl_refs[p][rows, 0:GLA_PAD] * jnp.exp(cums[p]) for p in seqs]
        ks = [qkl_refs[p][rows, GLA_PAD:2 * GLA_PAD] for p in seqs]
        kes = [ks[p] * jnp.exp(-cums[p]) for p in seqs]
        kls = [ks[p] * jnp.exp(lasts[p] - cums[p]) for p in seqs]
        decs = [jnp.exp(last) for last in lasts]
        vs = [v_refs[p][rows, :] for p in seqs]
        sts = [st_ref[p, h] for p, h in pairs]
        atts = [_dot_nt(qes[p][:, sl(h)], kes[p][:, sl(h)]) for p, h in pairs]
        inters = [_dot_nt(qes[p][:, sl(h)], st) for (p, h), st in zip(pairs, sts)]
        upds = [_dot_tn(vs[p][:, sl(h)], kls[p][:, sl(h)]) for p, h in pairs]
        intras = [_dot(jnp.where(causal, att, 0.0), vs[p][:, sl(h)]) for (p, h), att in zip(pairs, atts)]
        for i, (p, h) in enumerate(pairs):
            st_ref[p, h] = sts[i] * decs[p][:, sl(h)] + upds[i]
            o = intras[i] + inters[i]
            on = o * lax.rsqrt(jnp.mean(o * o, axis=-1, keepdims=True) + LN_EPS) * gn
            o_refs[p][rows, sl(h)] = on * rs_refs[p][rows, sl(h)]
        return carry

    lax.fori_loop(0, tb // GLA_CHUNK, chunk, 0)

    @pl.when(t == pl.num_programs(1) - 1)
    def _fin():
        for p in range(ns):
            sout_refs[p][0] = st_ref[p]


def _gla(qkl, v, rs, s0t, gn, *, nb, t, tb):
    nt = t // tb
    ns = GLA_SEQS_PER_STEP
    assert nb % ns == 0
    per = nb // ns
    w = GLA_HEADS * GLA_DV
    tri = jnp.tril(jnp.ones((GLA_CHUNK, GLA_CHUNK), BF16))
    row_in = lambda width, p: pl.BlockSpec((tb, width), lambda g, i: ((p * per + g) * nt + i, 0))
    st_in = lambda p: pl.BlockSpec((1, GLA_HEADS, GLA_DV, LANES), lambda g, i: (p * per + g, 0, 0, 0))
    row_out = pl.BlockSpec((tb, w), lambda g, i: (g * nt + i, 0))
    st_out = pl.BlockSpec((1, GLA_HEADS, GLA_DV, LANES), lambda g, i: (g, 0, 0, 0))
    slots = range(ns)
    outs = pl.pallas_call(
        functools.partial(_gla_kernel, tb=tb, ns=ns),
        out_shape=[jax.ShapeDtypeStruct((per * t, w), F32)] * ns
        + [jax.ShapeDtypeStruct((per,) + s0t.shape[1:], F32)] * ns,
        grid=(per, nt),
        in_specs=[row_in(3 * GLA_PAD, p) for p in slots] + [row_in(w, p) for p in slots] + [row_in(w, p) for p in slots]
        + [st_in(p) for p in slots] + [_const_spec(tri.shape), _const_spec(gn.shape)],
        out_specs=[row_out] * ns + [st_out] * ns,
        scratch_shapes=[pltpu.VMEM((ns, GLA_HEADS, GLA_DV, LANES), F32)],
        compiler_params=_params("parallel", "arbitrary"),
        name="gla",
    )(*([qkl] * ns + [v] * ns + [rs] * ns + [s0t] * ns + [tri, gn]))
    return jnp.concatenate(outs[:ns], axis=0), jnp.concatenate(outs[ns:], axis=0)


def _lru_kernel(xb_ref, gg_ref, buf0_ref, h0_ref, cw_ref, cb_ref, wa_ref, ba_ref, wx_ref, bx_ref, lam_ref,
                y_ref, hlast_ref, cbuf_ref, xs_ref, a_ref, b_ref, hs_ref, hcar_ref, *, tb, t_last):
    t = pl.program_id(1)
    halo = SUBLANES

    @pl.when(t == 0)
    def _init():
        xs_ref[0:halo] = buf0_ref[0]
        hcar_ref[...] = h0_ref[0]

    xs_ref[halo:halo + tb] = xb_ref[...]
    xc = cb_ref[...] + sum(xs_ref[pl.ds(halo - (CONV_W - 1) + i, tb), :] * cw_ref[i:i + 1, :] for i in range(CONV_W))
    ga = _sigmoid(_dot(xc, wa_ref[...]) + ba_ref[...])
    gx = _sigmoid(_dot(xc, wx_ref[...]) + bx_ref[...])
    log_at = ga * (-LRU_C * _softplus(-lam_ref[...]))
    a = jnp.exp(log_at)
    a_ref[...] = a
    b_ref[...] = jnp.sqrt(-jnp.tanh(log_at) * (a * a + 1.0)) * (gx * xc)

    def step(i, h):
        h = a_ref[pl.ds(i, 1), :] * h + b_ref[pl.ds(i, 1), :]
        hs_ref[pl.ds(i, 1), :] = h
        return h

    hcar_ref[...] = lax.fori_loop(0, tb, step, hcar_ref[...], unroll=8)
    y_ref[...] = hs_ref[...] * gg_ref[...]

    @pl.when(t == pl.num_programs(1) - 1)
    def _fin():
        hlast_ref[0] = hs_ref[t_last:t_last + 1, :]
        cbuf_ref[0] = xs_ref[halo + t_last - (CONV_W - 2):halo + t_last + 1, :]

    xs_ref[0:halo] = xs_ref[tb:tb + halo]


def _lru(xb, gg, buf0, h0, consts, *, nb, t, tb, t_last):
    nt = t // tb
    w = LRU_WIDTH
    row = pl.BlockSpec((tb, w), lambda b, i: (b * nt + i, 0))
    per_b = lambda r: pl.BlockSpec((1, r, w), lambda b, i: (b, 0, 0))
    return pl.pallas_call(
        functools.partial(_lru_kernel, tb=tb, t_last=t_last),
        out_shape=[jax.ShapeDtypeStruct((nb * t, w), F32), jax.ShapeDtypeStruct((nb, 1, w), F32),
                   jax.ShapeDtypeStruct((nb, CONV_W - 1, w), F32)],
        grid=(nb, nt),
        in_specs=[row, row, per_b(SUBLANES), per_b(1)] + [_const_spec(c.shape) for c in consts],
        out_specs=[row, per_b(1), per_b(CONV_W - 1)],
        scratch_shapes=[pltpu.VMEM((tb + 2 * SUBLANES, w), F32), pltpu.VMEM((tb, w), F32), pltpu.VMEM((tb, w), F32),
                        pltpu.VMEM((tb, w), F32), pltpu.VMEM((1, w), F32)],
        compiler_params=_params("parallel", "arbitrary"),
        name="lru",
    )(xb, gg, buf0, h0, *consts)


def _mix_out_kernel(a_ref, b_ref, x_ref, wa_ref, wb_ref, g_ref, beta_ref, o_ref):
    m = _dot(a_ref[...], wa_ref[...]) + _dot(b_ref[...], wb_ref[...])
    o_ref[...] = _layer_norm_rows(DN_ALPHA * x_ref[...] + m, g_ref[...], beta_ref[...])


def _mix_out(a, b, x, wa, wb, g, beta, *, tm=512):
    n, d = x.shape
    row = lambda w: pl.BlockSpec((tm, w), lambda i: (i, 0))
    consts = [wa, wb, g, beta]
    return pl.pallas_call(
        _mix_out_kernel,
        out_shape=jax.ShapeDtypeStruct((n, d), F32),
        grid=(n // tm,),
        in_specs=[row(a.shape[1]), row(b.shape[1]), row(d)] + [_const_spec(c.shape) for c in consts],
        out_specs=row(d),
        compiler_params=_params("parallel"),
        name="mix_out",
    )(a, b, x, *consts)


def _pad_heads(w, heads, width):
    lead = w.shape[:-1]
    w = w.reshape(lead + (heads, width))
    w = jnp.pad(w, [(0, 0)] * len(lead) + [(0, 0), (0, LANES - width)])
    return w.reshape(lead + (heads * LANES,))


def _even_weights(w_in, w_gate, b_gate, gla_norm, conv_w, conv_b, lru_wa, lru_ba, lru_wx, lru_bx, lru_lambda, w_out):
    gk = GLA_HEADS * GLA_DK
    gv = GLA_HEADS * GLA_DV
    o = 0
    cols = {}
    for name, width in (("q", gk), ("k", gk), ("v", gv), ("lr", GLA_RANK), ("r", gv), ("xb", LRU_WIDTH), ("gb", LRU_WIDTH)):
        cols[name] = w_in[:, o:o + width]
        o += width
    wlr = jnp.pad(cols["lr"], ((0, 0), (0, LANES - GLA_RANK)))
    wg = jnp.pad(_pad_heads(w_gate, GLA_HEADS, GLA_DK), ((0, LANES - GLA_RANK), (0, 0)))
    bg = _pad_heads(b_gate[None], GLA_HEADS, GLA_DK)
    in_consts = [_pad_heads(cols["q"], GLA_HEADS, GLA_DK).astype(BF16), _pad_heads(cols["k"], GLA_HEADS, GLA_DK).astype(BF16),
                 cols["v"].astype(BF16), cols["r"].astype(BF16), cols["xb"].astype(BF16), cols["gb"].astype(BF16),
                 wlr.astype(BF16), wg.astype(BF16), bg]
    eye = jnp.eye(LRU_BLOCKS, dtype=F32)
    bd = lambda w: (eye[:, None, :, None] * w[:, :, None, :]).reshape(LRU_WIDTH, LRU_WIDTH).astype(BF16)
    lru_consts = [conv_w, conv_b[None], bd(lru_wa), lru_ba[None], bd(lru_wx), lru_bx[None], lru_lambda[None]]
    return in_consts, gla_norm[None], lru_consts, w_out[:gv].astype(BF16), w_out[gv:].astype(BF16)


def _seq_rows(a, off, nb, t, t_pad):
    a = a[off:off + nb * t].reshape(nb, t, a.shape[1])
    return jnp.pad(a, ((0, 0), (0, t_pad - t), (0, 0))).reshape(nb * t_pad, a.shape[2])


def _unpad_rows(a, nb, t, t_pad):
    return a.reshape(nb, t_pad, a.shape[1])[:, :t].reshape(nb * t, a.shape[1])


def _even_layer(x, groups, weights, g, beta, *, tm):
    in_consts, gn, lru_consts, wo_a, wo_b = weights
    qkl, v, rs, xb, gg = _even_in(x, in_consts, tm=tm)
    ogs, yls, states = [], [], []
    for off, nb, t, tb, s_gla, h_lru, cbuf in groups:
        s0t = jnp.pad(jnp.swapaxes(s_gla, 2, 3), ((0, 0), (0, 0), (0, 0), (0, LANES - GLA_DK)))
        buf0 = jnp.pad(cbuf, ((0, 0), (SUBLANES - (CONV_W - 1), 0), (0, 0)))
        if tb is None:
            tg = GLA_CHUNK
            tl = SUBLANES
            og, st = _gla(_seq_rows(qkl, off, nb, t, tg), _seq_rows(v, off, nb, t, tg), _seq_rows(rs, off, nb, t, tg),
                          s0t, gn, nb=nb, t=tg, tb=tg)
            yl, hl, cb = _lru(_seq_rows(xb, off, nb, t, tl), _seq_rows(gg, off, nb, t, tl), buf0, h_lru[:, None],
                              lru_consts, nb=nb, t=tl, tb=tl, t_last=t - 1)
            og = _unpad_rows(og, nb, t, tg)
            yl = _unpad_rows(yl, nb, t, tl)
        else:
            og, st = _gla(qkl, v, rs, s0t, gn, nb=nb, t=t, tb=GLA_TIME_BLOCK)
            yl, hl, cb = _lru(xb, gg, buf0, h_lru[:, None], lru_consts, nb=nb, t=t, tb=tb, t_last=tb - 1)
        ogs.append(og)
        yls.append(yl)
        states.append((jnp.swapaxes(st[..., :GLA_DK], 2, 3), hl[:, 0], cb))
    n_used = sum(o.shape[0] for o in ogs)
    tail = jnp.zeros((x.shape[0] - n_used, LRU_WIDTH), F32)
    og = jnp.concatenate(ogs + [tail], axis=0)
    yl = jnp.concatenate(yls + [tail], axis=0)
    return _mix_out(og, yl, x, wo_a, wo_b, g, beta, tm=tm), states


RWKV_STREAMS = 5


def _rwkv_in_kernel(*refs, has_vres, tm, long_seqs, short_seqs):
    if has_vres:
        (x_ref, halo_ref, sl_ref, ov_ref, mu_ref, wr_ref, wk_ref, wv_ref, w1_ref, w2_ref, a1_ref, a2_ref, g1_ref, g2_ref,
         w0_ref, a0_ref, vf_ref, v1_ref, v2_ref, v0_ref, p_ref, g_ref, xs_ref) = refs
    else:
        (x_ref, halo_ref, sl_ref, ov_ref, mu_ref, wr_ref, wk_ref, wv_ref, w1_ref, w2_ref, a1_ref, a2_ref, g1_ref, g2_ref,
         w0_ref, a0_ref, p_ref, g_ref, xs_ref) = refs
    d = D_MODEL
    x = x_ref[...]
    r0 = pl.program_id(0) * tm
    xs_ref[0:SUBLANES] = halo_ref[...]
    xs_ref[SUBLANES:SUBLANES + tm] = x
    prev = xs_ref[pl.ds(SUBLANES - 1, tm), :]
    rows = r0 + lax.broadcasted_iota(jnp.int32, (tm, 1), 0)
    off, cnt, length = long_seqs
    b = (jnp.maximum(r0 - off, 0) + (length - 1)) // length
    hit = jnp.logical_and(rows == off + b * length, b < cnt)
    prev = jnp.where(hit, sl_ref[pl.ds(jnp.minimum(b, cnt - 1), 1), :], prev)
    off, cnt, length = short_seqs
    rel = rows - off
    first = jnp.logical_and(jnp.logical_and(rel >= 0, rel < cnt * length), jnp.bitwise_and(rel, length - 1) == 0)
    prev = jnp.where(first, ov_ref[...], prev)
    xx = prev - x
    mix = lambda i: (x + xx * mu_ref[i:i + 1, :]).astype(BF16)
    xr, xw, xk, xv, xa, xg = (mix(i) for i in range(6))
    p_ref[:, 0:d] = _dot(xr, wr_ref[...])
    p_ref[:, d:2 * d] = _dot(xk, wk_ref[...])
    v = _dot(xv, wv_ref[...])
    if has_vres:
        v = v + (vf_ref[...] - v) * _sigmoid(v0_ref[...] + _dot(_dot(xv, v1_ref[...]), v2_ref[...]))
    p_ref[:, 2 * d:3 * d] = v
    p_ref[:, 3 * d:4 * d] = w0_ref[...] + _dot(jnp.tanh(_dot(xw, w1_ref[...])), w2_ref[...])
    p_ref[:, 4 * d:5 * d] = a0_ref[...] + _dot(_dot(xa, a1_ref[...]), a2_ref[...])
    g_ref[...] = _dot(_sigmoid(_dot(xg, g1_ref[...])), g2_ref[...])


def _rwkv_in(x, shift_long, shift_short, consts, vres, *, tm, long_seqs, short_seqs):
    n, d = x.shape
    off, cnt, length = short_seqs
    assert length & (length - 1) == 0 and long_seqs[2] >= tm and tm % SUBLANES == 0
    tile0 = off // tm
    tiles = -(-(off + cnt * length) // tm) - tile0
    ov = jnp.pad(shift_short[:, None], ((0, 0), (0, length - 1), (0, 0))).reshape(cnt * length, d)
    ov = jnp.pad(ov, ((off - tile0 * tm, tiles * tm - (off - tile0 * tm) - cnt * length), (0, 0)))
    sl = jnp.pad(shift_long, ((0, -shift_long.shape[0] % SUBLANES), (0, 0)))
    row = lambda w: pl.BlockSpec((tm, w), lambda i: (i, 0))
    halo = pl.BlockSpec((SUBLANES, d), lambda i: (jnp.maximum(i * (tm // SUBLANES) - 1, 0), 0))
    ov_spec = pl.BlockSpec((tm, d), lambda i: (jnp.clip(i - tile0, 0, tiles - 1), 0))
    args = [x, x, sl, ov] + list(consts)
    specs = [row(d), halo, _const_spec(sl.shape), ov_spec] + [_const_spec(c.shape) for c in consts]
    if vres is not None:
        vf, vconsts = vres
        args += [vf] + list(vconsts)
        specs += [row(d)] + [_const_spec(c.shape) for c in vconsts]
    return pl.pallas_call(
        functools.partial(_rwkv_in_kernel, has_vres=vres is not None, tm=tm, long_seqs=long_seqs, short_seqs=short_seqs),
        out_shape=[jax.ShapeDtypeStruct((n, RWKV_STREAMS * d), F32), jax.ShapeDtypeStruct((n, d), F32)],
        grid=(n // tm,),
        in_specs=specs,
        out_specs=[row(RWKV_STREAMS * d), row(d)],
        scratch_shapes=[pltpu.VMEM((tm + SUBLANES, d), F32)],
        compiler_params=_params("parallel"),
        name="rwkv_in",
    )(*args)


def _wkv_kernel(p_ref, s0_ref, kk_ref, ka_ref, rk_ref, lg_ref, lb_ref, y_ref, sout_ref,
                s_ref, dec_ref, a_ref, b_ref, km_ref, bon_ref, *, tt):
    t = pl.program_id(1)
    hk = RWKV_HEAD

    @pl.when(t == 0)
    def _init():
        s_ref[...] = s0_ref[...]

    r = p_ref[:, 0]
    k = p_ref[:, 1]
    v = p_ref[:, 2]
    dec_ref[...] = jnp.exp(-jnp.exp(-_softplus(-p_ref[:, 3]) - 0.5))
    ag = _sigmoid(p_ref[:, 4])
    kk = k * kk_ref[...][None]
    kk = kk / jnp.maximum(jnp.sqrt(jnp.sum(kk * kk, axis=1, keepdims=True)), 1e-12)
    km = k * (1.0 + (ag - 1.0) * ka_ref[...][None])
    km_ref[...] = km
    a_ref[...] = -kk
    b_ref[...] = kk * ag
    bon_ref[...] = jnp.sum(r * km * rk_ref[...][None], axis=1, keepdims=True) * v
    lg = lg_ref[...]
    lb = lb_ref[...]

    def step(i, carry):
        lanes = 4
        parts = [jnp.zeros((hk, LANES), F32) for _ in range(lanes)]
        for q in range(hk):
            parts[q % lanes] = parts[q % lanes] + s_ref[q] * a_ref[i, q:q + 1, :]
        sa = (parts[0] + parts[1]) + (parts[2] + parts[3])
        vv = p_ref[i, 2]
        parts = [jnp.zeros((hk, LANES), F32) for _ in range(lanes)]
        for q in range(hk):
            sn = s_ref[q] * dec_ref[i, q:q + 1, :] + (sa * b_ref[i, q:q + 1, :] + vv * km_ref[i, q:q + 1, :])
            s_ref[q] = sn
            parts[q % lanes] = parts[q % lanes] + sn * p_ref[i, 0, q:q + 1, :]
        y = (parts[0] + parts[1]) + (parts[2] + parts[3])
        mu = jnp.mean(y, axis=0, keepdims=True)
        dlt = y - mu
        var = jnp.mean(dlt * dlt, axis=0, keepdims=True)
        y_ref[i] = dlt * lax.rsqrt(var + RWKV_GN_EPS) * lg + lb + bon_ref[i]
        return carry

    lax.fori_loop(0, tt, step, 0)

    @pl.when(t == pl.num_programs(1) - 1)
    def _fin():
        sout_ref[...] = s_ref[...]


def _wkv(p, s0, lane_consts, *, tt):
    t, _, hk, l = p.shape
    lane2 = pl.BlockSpec((hk, LANES), lambda g, i: (0, g))
    st = pl.BlockSpec((hk, hk, LANES), lambda g, i: (0, 0, g))
    seq = lambda: pltpu.VMEM((tt, hk, LANES), F32)
    return pl.pallas_call(
        functools.partial(_wkv_kernel, tt=tt),
        out_shape=[jax.ShapeDtypeStruct((t, hk, l), F32), jax.ShapeDtypeStruct((hk, hk, l), F32)],
        grid=(l // LANES, t // tt),
        in_specs=[pl.BlockSpec((tt, RWKV_STREAMS, hk, LANES), lambda g, i: (i, 0, 0, g)), st] + [lane2] * 5,
        out_specs=[pl.BlockSpec((tt, hk, LANES), lambda g, i: (i, 0, g)), st],
        scratch_shapes=[pltpu.VMEM((hk, hk, LANES), F32), seq(), seq(), seq(), seq(), seq()],
        compiler_params=_params("parallel", "arbitrary"),
        name="wkv",
    )(p, s0, *lane_consts)


def _gate_out_kernel(y_ref, g_ref, x_ref, w_ref, lg_ref, lb_ref, o_ref):
    m = _dot(y_ref[...] * g_ref[...], w_ref[...])
    o_ref[...] = _layer_norm_rows(DN_ALPHA * x_ref[...] + m, lg_ref[...], lb_ref[...])


def _gate_out(y, g, x, w, lg, lb, *, tm):
    n, d = x.shape
    row = pl.BlockSpec((tm, d), lambda i: (i, 0))
    consts = [w, lg, lb]
    return pl.pallas_call(
        _gate_out_kernel,
        out_shape=jax.ShapeDtypeStruct((n, d), F32),
        grid=(n // tm,),
        in_specs=[row, row, row] + [_const_spec(c.shape) for c in consts],
        out_specs=row,
        compiler_params=_params("parallel"),
        name="gate_out",
    )(y, g, x, *consts)


def _pad_cols(w):
    return jnp.pad(w, ((0, 0), (0, LANES - w.shape[1]))).astype(BF16)


def _pad_rows(w):
    return jnp.pad(w, ((0, LANES - w.shape[0]), (0, 0))).astype(BF16)


def _rwkv_weights(mu, w_r, w_k, w_v, w_o, w0, w1, w2, a0, a1, a2, g1, g2, k_k, k_a, r_k, lnx_g, lnx_b, vres):
    consts = [jnp.pad(mu, ((0, SUBLANES - mu.shape[0]), (0, 0))), w_r.astype(BF16), w_k.astype(BF16), w_v.astype(BF16),
              _pad_cols(w1), _pad_rows(w2), _pad_cols(a1), _pad_rows(a2), _pad_cols(g1), _pad_rows(g2), w0[None], a0[None]]
    vconsts = None if vres is None else [_pad_cols(vres[1]), _pad_rows(vres[2]), vres[0][None]]
    per_head = [k_k.reshape(RWKV_HEADS, RWKV_HEAD).T, k_a.reshape(RWKV_HEADS, RWKV_HEAD).T, r_k.T,
                lnx_g.reshape(RWKV_HEADS, RWKV_HEAD).T, lnx_b.reshape(RWKV_HEADS, RWKV_HEAD).T]
    return consts, vconsts, per_head, w_o.astype(BF16)


def _rwkv_layer(x, groups, weights, v_first, g, beta, *, tm):
    consts, vconsts, per_head, w_o = weights
    n, d = x.shape
    (off_l, nb_l, t_l, _, _, shift_l), (off_s, nb_s, t_s, _, _, shift_s) = groups
    n_used = nb_l * t_l + nb_s * t_s
    vres = None if vconsts is None else (v_first, vconsts)
    p, gate = _rwkv_in(x, shift_l, shift_s, consts, vres, tm=tm,
                       long_seqs=(off_l, nb_l, t_l), short_seqs=(off_s, nb_s, t_s))
    ys, states = [], []
    for off, nb, t, tt, s, shift in groups:
        lanes = nb * RWKV_HEADS
        pg = p[off:off + nb * t].reshape(nb, t, RWKV_STREAMS, RWKV_HEADS, RWKV_HEAD)
        pg = jnp.transpose(pg, (1, 2, 4, 0, 3)).reshape(t, RWKV_STREAMS, RWKV_HEAD, lanes)
        hk2 = RWKV_HEAD * RWKV_HEAD
        s0 = jnp.swapaxes(s, 2, 3).reshape(lanes, hk2).T.reshape(RWKV_HEAD, RWKV_HEAD, lanes)
        lane_consts = [jnp.tile(c, (1, nb)) for c in per_head]
        y, s_out = _wkv(pg, s0, lane_consts, tt=tt)
        yt = jnp.swapaxes(y, 1, 2).reshape(t, nb, d)
        ys.append(jnp.swapaxes(yt, 0, 1).reshape(nb * t, d))
        s_new = jnp.swapaxes(s_out.reshape(hk2, lanes).T.reshape(nb, RWKV_HEADS, RWKV_HEAD, RWKV_HEAD), 2, 3)
        states.append((s_new, x[off:off + nb * t].reshape(nb, t, d)[:, -1]))
    y = jnp.concatenate(ys + [jnp.zeros((n - n_used, d), F32)], axis=0)
    return _gate_out(y, gate, x, w_o, g, beta, tm=tm), states, p[:, 2 * d:3 * d]


TOKEN_TILE = 512
PEER_TOKEN_TILE = 1024
PROMPT_TIME_BLOCK = 688
WKV_TIME_BLOCK = 16


def kernel(x_prompt, x_sample, state_gla, state_lru_h, state_lru_conv, state_rwkv, state_rwkv_shift, meta_tokens, ln_g, ln_b, ev_w_in, ev_gla_w_gate, ev_gla_b_gate, ev_gla_norm, ev_conv_w, ev_conv_b, ev_lru_wa, ev_lru_ba, ev_lru_wx, ev_lru_bx, ev_lru_lambda, ev_w_out, od_mu, od_w_r, od_w_k, od_w_v, od_w_o, od_w0, od_w1, od_w2, od_a0, od_a1, od_a2, od_v0, od_v1, od_v2, od_g1, od_g2, od_k_k, od_k_a, od_r_k, od_lnx_g, od_lnx_b, peer_w_q, peer_keys, peer_u, peer_v):
    bp, sp, d = x_prompt.shape
    bs, ss, _ = x_sample.shape
    tp = sp + N_META
    n_p, n_s = bp * tp, bs * ss
    n = -(-(n_p + n_s) // PEER_TOKEN_TILE) * PEER_TOKEN_TILE
    assert tp % PROMPT_TIME_BLOCK == 0 and tp % GLA_TIME_BLOCK == 0 and tp % WKV_TIME_BLOCK == 0 and n % TOKEN_TILE == 0

    xp = jnp.concatenate([jnp.broadcast_to(meta_tokens[None], (bp, N_META, d)), x_prompt], axis=1)
    x = jnp.concatenate([xp.reshape(n_p, d), x_sample.reshape(n_s, d), jnp.zeros((n - n_p - n_s, d), F32)], axis=0)

    n_pairs = DEPTH // 2
    zeros = lambda *s: jnp.zeros(s, F32)
    outs = {k: [] for k in ("p_gla", "p_h", "p_conv", "p_rwkv", "p_shift", "s_gla", "s_h", "s_conv", "s_rwkv", "s_shift")}
    v_first = None
    for layer in range(DEPTH):
        j = layer // 2
        g, beta = ln_g[layer, 0][None], ln_b[layer, 0][None]
        if layer % 2 == 0:
            weights = _even_weights(ev_w_in[j], ev_gla_w_gate[j], ev_gla_b_gate[j], ev_gla_norm[j], ev_conv_w[j], ev_conv_b[j],
                                    ev_lru_wa[j], ev_lru_ba[j], ev_lru_wx[j], ev_lru_bx[j], ev_lru_lambda[j], ev_w_out[j])
            groups = [(0, bp, tp, PROMPT_TIME_BLOCK, zeros(bp, GLA_HEADS, GLA_DK, GLA_DV), zeros(bp, LRU_WIDTH),
                       zeros(bp, CONV_W - 1, LRU_WIDTH)),
                      (n_p, bs, ss, None, state_gla[j], state_lru_h[j], state_lru_conv[j])]
            x, states = _even_layer(x, groups, weights, g, beta, tm=TOKEN_TILE)
            for pre, st in zip("ps", states):
                outs[pre + "_gla"].append(st[0])
                outs[pre + "_h"].append(st[1])
                outs[pre + "_conv"].append(st[2])
        else:
            vres = None if j == 0 else (od_v0[j - 1], od_v1[j - 1], od_v2[j - 1])
            weights = _rwkv_weights(od_mu[j], od_w_r[j], od_w_k[j], od_w_v[j], od_w_o[j], od_w0[j], od_w1[j], od_w2[j],
                                    od_a0[j], od_a1[j], od_a2[j], od_g1[j], od_g2[j], od_k_k[j], od_k_a[j], od_r_k[j],
                                    od_lnx_g[j], od_lnx_b[j], vres)
            groups = [(0, bp, tp, WKV_TIME_BLOCK, zeros(bp, RWKV_HEADS, RWKV_HEAD, RWKV_HEAD), zeros(bp, d)),
                      (n_p, bs, ss, ss, state_rwkv[j], state_rwkv_shift[j])]
            x, states, v = _rwkv_layer(x, groups, weights, v_first, g, beta, tm=TOKEN_TILE)
            if v_first is None:
                v_first = v
            for pre, st in zip("ps", states):
                outs[pre + "_rwkv"].append(st[0])
                outs[pre + "_shift"].append(st[1])
        x = _peer_layer(x, peer_w_q[layer].T.astype(BF16), peer_keys[layer, :, 0].astype(BF16),
                        peer_keys[layer, :, 1].astype(BF16), peer_u[layer].astype(BF16), peer_v[layer].astype(BF16),
                        ln_g[layer, 1][None], ln_b[layer, 1][None], tn=PEER_TOKEN_TILE)

    y_prompt = x[:n_p].reshape(bp, tp, d)[:, N_META:]
    y_sample = x[n_p:n_p + n_s].reshape(bs, ss, d)
    st = {k: jnp.stack(v) for k, v in outs.items()}
    return (y_prompt, y_sample, st["p_gla"], st["p_h"], st["p_conv"], st["p_rwkv"], st["p_shift"],
            st["s_gla"], st["s_h"], st["s_conv"], st["s_rwkv"], st["s_shift"])
```

```python
import functools

import jax
import jax.numpy as jnp
from jax import lax
from jax.experimental import pallas as pl
from jax.experimental.pallas import tpu as pltpu

F32 = jnp.float32
BF16 = jnp.bfloat16

D_MODEL = 1024
DEPTH = 4
N_META = 16
GLA_HEADS = 4
GLA_DK = 64
GLA_DV = 128
GLA_RANK = 16
GLA_GATE_NORM = 16.0
GLA_CHUNK = 16
LRU_WIDTH = 512
LRU_BLOCKS = 8
CONV_W = 4
LRU_C = 8.0
RWKV_HEAD = 64
RWKV_HEADS = D_MODEL // RWKV_HEAD
RWKV_GN_EPS = 64e-5
PEER_HEADS = 8
PEER_NKEYS = 128
PEER_DKEY = 256
PEER_HALF = PEER_DKEY // 2
PEER_TOPK = 16
DN_ALPHA = float((2 * DEPTH) ** 0.25)
LN_EPS = 1e-5

LANES = 128
SUBLANES = 8
VMEM_LIMIT = 56 * 1024 * 1024


def _dot(a, b):
    return jnp.dot(a.astype(BF16), b.astype(BF16), preferred_element_type=F32)


def _dot_nt(a, b):
    return lax.dot_general(a.astype(BF16), b.astype(BF16), (((1,), (1,)), ((), ())), preferred_element_type=F32)


def _dot_tn(a, b):
    return lax.dot_general(a.astype(BF16), b.astype(BF16), (((0,), (0,)), ((), ())), preferred_element_type=F32)


def _layer_norm_rows(z, g, b):
    mu = jnp.mean(z, axis=-1, keepdims=True)
    d = z - mu
    var = jnp.mean(d * d, axis=-1, keepdims=True)
    return d * lax.rsqrt(var + LN_EPS) * g + b


def _sigmoid(x):
    return 1.0 / (1.0 + jnp.exp(-x))


def _softplus(x):
    return jnp.maximum(x, 0.0) + jnp.log1p(jnp.exp(-jnp.abs(x)))


def _gelu_tanh(x):
    c0 = 0.7978845608028654
    c1 = c0 * 0.044715
    one = jnp.asarray(1.0, x.dtype)
    return (0.5 * x) * (one + jnp.tanh(x * (c0 + c1 * (x * x))))


def _silu(x):
    return x * _sigmoid(x)


def _params(*sem):
    return pltpu.CompilerParams(dimension_semantics=sem, vmem_limit_bytes=VMEM_LIMIT)


def _const_spec(shape):
    nd = len(shape)
    return pl.BlockSpec(shape, lambda *_: (0,) * nd)


PEER_CAND_ROWS = 80
PEER_PACK = 2 * SUBLANES
PEER_SELECT_LANES = 2 * LANES


def _peer_cand_index():
    rows = [r1 * PEER_TOPK for r1 in range(16)]
    for j in range(1, 8):
        rows += [r1 * PEER_TOPK + j for r1 in range(8)]
    rows += list(range(8, 16))
    return jnp.broadcast_to(jnp.asarray(rows, F32)[:, None], (PEER_CAND_ROWS, PEER_SELECT_LANES))


PEER_CODE_STEP = 2.0 ** 116
PEER_CODED_BELOW = -(2.0 ** 119)


def _rank_code(r):
    return -PEER_CODE_STEP * (PEER_TOPK + r)


def _decode_rank(s):
    return jnp.where(s < PEER_CODED_BELOW, s * (-1.0 / PEER_CODE_STEP) - PEER_TOPK, float(PEER_TOPK))


def _count_coded(s):
    return jnp.sum(jnp.where(s < PEER_CODED_BELOW, 1.0, 0.0), axis=0, keepdims=True)


def _top16(scores, key_iota):
    row16 = lax.broadcasted_iota(jnp.int32, (PEER_TOPK, scores[0].shape[1]), 0)
    vals = [jnp.zeros((PEER_TOPK, s.shape[1]), F32) for s in scores]
    scores = list(scores)
    for r in range(PEER_TOPK):
        tops = [jnp.max(s, axis=0, keepdims=True) for s in scores]
        hits = [s == m for s, m in zip(scores, tops)]
        if key_iota is not None:
            firsts = [jnp.min(jnp.where(hit, key_iota, float(PEER_NKEYS)), axis=0, keepdims=True) for hit in hits]
            hits = [key_iota == first for first in firsts]
        scores = [jnp.where(hit, _rank_code(r), s) for hit, s in zip(hits, scores)]
        vals = [jnp.where(row16 == r, m, v) for m, v in zip(tops, vals)]
    return list(zip(vals, scores))


def _peer_candidates(a, b):
    blocks = [a + b[0:1]]
    for j in range(1, 8):
        blocks.append(a[0:8] + b[j:j + 1])
    blocks.append(a[0:1] + b[8:16])
    return jnp.concatenate(blocks, axis=0)


def _select16(c, cidx):
    for _ in range(PEER_TOPK):
        m = jnp.max(c, axis=0, keepdims=True)
        hit = c == m
        if cidx is not None:
            first = jnp.min(jnp.where(hit, cidx, 1e9), axis=0, keepdims=True)
            hit = cidx == first
        c = jnp.where(hit, _rank_code(0), c)
    return c


def _peer_gate_tiles(s1, s2, a, b, s1c, s2c, c, cc):
    sel = jnp.where(cc < PEER_CODED_BELOW, 1.0, 0.0)
    z = jnp.sum(sel * jnp.exp(c - c[0:1]), axis=0, keepdims=True)
    j_lo = sel[0:8]
    for j in range(1, 8):
        j_lo = j_lo + sel[8 + 8 * j:16 + 8 * j]
    extra = jnp.sum(sel[72:80], axis=0, keepdims=True)
    row8 = lax.broadcasted_iota(jnp.int32, j_lo.shape, 0)
    j_lo = j_lo + jnp.where(row8 == 0, extra, 0.0)
    jt = jnp.concatenate([j_lo, sel[8:16]], axis=0)
    r1 = _decode_rank(s1c)
    h1 = jnp.zeros_like(r1)
    for r in range(PEER_TOPK):
        h1 = jnp.where(r1 == float(r), jt[r:r + 1], h1)
    return h1, jnp.exp(s1 - a[0:1]) / z, _decode_rank(s2c), jnp.exp(s2 - b[0:1])


def _peer_kernel(x_ref, wqt_ref, k1_ref, k2_ref, cidx_ref, u_ref, vt_ref, g_ref, b_ref, o_ref,
                 xt_ref, acc_ref, h1_ref, c1_ref, r2_ref, e2_ref, s_ref, hd0_ref, hd1_ref, coef_ref, *, tn, te):
    j = pl.program_id(1)
    nj = pl.num_programs(1)

    @pl.when(j == 0)
    def _select():
        xt_ref[...] = x_ref[...].T.astype(BF16)
        acc_ref[...] = jnp.zeros_like(acc_ref)
        hd1_ref[...] = jnp.zeros_like(hd1_ref)
        coef_ref[0:te] = jnp.zeros((te, tn), BF16)
        key_iota = lax.broadcasted_iota(jnp.int32, (PEER_NKEYS, PEER_SELECT_LANES), 0).astype(F32)

        def head(h, carry):
            q = _dot(wqt_ref[pl.ds(pl.multiple_of(h * PEER_DKEY, PEER_DKEY), PEER_DKEY), :], xt_ref[...])
            mu = jnp.mean(q, axis=0, keepdims=True)
            d = q - mu
            qn = d * lax.rsqrt(jnp.mean(d * d, axis=0, keepdims=True) + LN_EPS)
            s_ref[0] = _dot(k1_ref[h], qn[0:PEER_HALF])
            s_ref[1] = _dot(k2_ref[h], qn[PEER_HALF:PEER_DKEY])

            def chunk(ci, carry2):
                ls = pl.ds(pl.multiple_of(ci * PEER_SELECT_LANES, PEER_SELECT_LANES), PEER_SELECT_LANES)
                s1 = s_ref[0, :, ls]
                s2 = s_ref[1, :, ls]

                def put(tiles):
                    h1_ref[h, :, ls] = tiles[0]
                    c1_ref[h, :, ls] = tiles[1]
                    r2_ref[h, :, ls] = tiles[2].astype(BF16)
                    e2_ref[h, :, ls] = tiles[3].astype(BF16)

                (a, s1c), (b, s2c) = _top16([s1, s2], None)
                c = _peer_candidates(a, b)
                cc = _select16(c, None)
                put(_peer_gate_tiles(s1, s2, a, b, s1c, s2c, c, cc))
                miscount = (jnp.abs(_count_coded(s1c) - PEER_TOPK) + jnp.abs(_count_coded(s2c) - PEER_TOPK)
                            + jnp.abs(_count_coded(cc) - PEER_TOPK))

                @pl.when(jnp.max(miscount) > 0.0)
                def _with_ties():
                    (a, s1c), (b, s2c) = _top16([s1, s2], key_iota)
                    c = _peer_candidates(a, b)
                    cc = _select16(c, cidx_ref[...])
                    put(_peer_gate_tiles(s1, s2, a, b, s1c, s2c, c, cc))

                return carry2

            lax.fori_loop(0, tn // PEER_SELECT_LANES, chunk, 0)
            return carry

        lax.fori_loop(0, PEER_HEADS, head, 0)

    groups = PEER_NKEYS // PEER_PACK
    last_key = PEER_NKEYS - 1
    zero_pack = jnp.zeros((PEER_PACK, LANES), F32)

    def gate_rows(tile):
        rows = []
        for ii in range(te // PEER_NKEYS):
            i1 = jnp.clip(tile * (te // PEER_NKEYS) + ii, 0, last_key)
            rows.append(([h1_ref[h, pl.ds(i1, 1), :] for h in range(PEER_HEADS)],
                         [c1_ref[h, pl.ds(i1, 1), :] for h in range(PEER_HEADS)]))
        return rows

    def gate_stage(rows, hd_ref, coef_rows, lane_groups):
        nk = len(rows)
        for c in lane_groups:
            ls = slice(c * LANES, (c + 1) * LANES)
            gates = [[jnp.zeros((PEER_PACK, LANES), BF16) for _ in range(groups)] for _ in range(nk)]
            for h in range(PEER_HEADS):
                bounds = [(rows[ii][0][h][:, ls] + zero_pack).astype(BF16) for ii in range(nk)]
                scales = [(rows[ii][1][h][:, ls] + zero_pack).astype(BF16) for ii in range(nk)]
                for gi in range(groups):
                    krows = slice(gi * PEER_PACK, (gi + 1) * PEER_PACK)
                    rank = r2_ref[h, krows, ls]
                    fac = e2_ref[h, krows, ls]
                    for ii in range(nk):
                        gates[ii][gi] = gates[ii][gi] + jnp.where(rank < bounds[ii], fac * scales[ii], jnp.zeros((), BF16))
            for ii in range(nk):
                for gi in range(groups):
                    erows = slice(ii * PEER_NKEYS + gi * PEER_PACK, ii * PEER_NKEYS + (gi + 1) * PEER_PACK)
                    crows = slice(coef_rows + erows.start, coef_rows + erows.stop)
                    coef_ref[crows, ls] = gates[ii][gi] * _gelu_tanh(hd_ref[erows, ls].astype(BF16))

    piece = 2 * LANES
    per = piece // LANES
    rows_odd = gate_rows(2 * j - 1)
    rows_even = gate_rows(2 * j)
    pieces = tn // piece
    lanes_of = lambda c: slice(c * piece, (c + 1) * piece)

    def down_project(c):
        acc_ref[:, lanes_of(c)] += jnp.dot(vt_ref[0], coef_ref[:, lanes_of(c)], preferred_element_type=F32)

    for c in range(pieces):
        gate_stage(rows_odd, hd1_ref, te, range(c * per, (c + 1) * per))
        hd0_ref[:, lanes_of(c)] = _dot(u_ref[0:te, :], xt_ref[:, lanes_of(c)])
        hd1_ref[:, lanes_of(c)] = _dot(u_ref[te:2 * te, :], xt_ref[:, lanes_of(c)])
        if c > 0:
            down_project(c - 1)
    down_project(pieces - 1)
    for c in range(pieces):
        gate_stage(rows_even, hd0_ref, 0, range(c * per, (c + 1) * per))

    @pl.when(j == nj - 1)
    def _finish():
        y = acc_ref[...].T
        o_ref[...] = _layer_norm_rows(DN_ALPHA * x_ref[...] + y, g_ref[...], b_ref[...])


def _peer_layer(x, wqt, k1, k2, u, v, g, b, *, tn=1024, te=256):
    n, d = x.shape
    ne = u.shape[0]
    assert ne % (2 * te) == 0 and n % tn == 0
    pairs = ne // (2 * te)
    steps = pairs + 1
    vt = jnp.swapaxes(v.reshape(pairs, 2 * te, d), 1, 2)
    kern = functools.partial(_peer_kernel, tn=tn, te=te)
    one = pl.Buffered(1)
    return pl.pallas_call(
        kern,
        out_shape=jax.ShapeDtypeStruct((n, d), F32),
        grid=(n // tn, steps),
        in_specs=[
            pl.BlockSpec((tn, d), lambda i, j: (i, 0), pipeline_mode=one),
            pl.BlockSpec(wqt.shape, lambda i, j: (0, 0), pipeline_mode=one),
            pl.BlockSpec(k1.shape, lambda i, j: (0, 0, 0), pipeline_mode=one),
            pl.BlockSpec(k2.shape, lambda i, j: (0, 0, 0), pipeline_mode=one),
            pl.BlockSpec((PEER_CAND_ROWS, PEER_SELECT_LANES), lambda i, j: (0, 0), pipeline_mode=one),
            pl.BlockSpec((2 * te, d), lambda i, j: (jnp.minimum(j, pairs - 1), 0)),
            pl.BlockSpec((1, d, 2 * te), lambda i, j: (jnp.maximum(j - 1, 0), 0, 0)),
            pl.BlockSpec((1, d), lambda i, j: (0, 0), pipeline_mode=one),
            pl.BlockSpec((1, d), lambda i, j: (0, 0), pipeline_mode=one),
        ],
        out_specs=pl.BlockSpec((tn, d), lambda i, j: (i, 0), pipeline_mode=one),
        scratch_shapes=[
            pltpu.VMEM((d, tn), BF16),
            pltpu.VMEM((d, tn), F32),
            pltpu.VMEM((PEER_HEADS, PEER_NKEYS, tn), F32),
            pltpu.VMEM((PEER_HEADS, PEER_NKEYS, tn), F32),
            pltpu.VMEM((PEER_HEADS, PEER_NKEYS, tn), BF16),
            pltpu.VMEM((PEER_HEADS, PEER_NKEYS, tn), BF16),
            pltpu.VMEM((2, PEER_NKEYS, tn), F32),
            pltpu.VMEM((te, tn), F32),
            pltpu.VMEM((te, tn), F32),
            pltpu.VMEM((2 * te, tn), BF16),
        ],
        compiler_params=_params("parallel", "arbitrary"),
        name="peer",
    )(x, wqt, k1, k2, _peer_cand_index(), u, vt, g, b)


GLA_PAD = GLA_HEADS * LANES


def _even_in_kernel(x_ref, wq_ref, wk_ref, wv_ref, wr_ref, wxb_ref, wgb_ref, wlr_ref, wg_ref, bg_ref,
                    qkl_ref, v_ref, rs_ref, xb_ref, gg_ref):
    xb16 = x_ref[...].astype(BF16)
    qkl_ref[:, 0:GLA_PAD] = _dot(xb16, wq_ref[...]) * (GLA_DK ** -0.5)
    qkl_ref[:, GLA_PAD:2 * GLA_PAD] = _dot(xb16, wk_ref[...])
    glr = _dot(xb16, wlr_ref[...])
    z = _dot(glr, wg_ref[...]) + bg_ref[...]
    qkl_ref[:, 2 * GLA_PAD:3 * GLA_PAD] = -_softplus(-z) * (1.0 / GLA_GATE_NORM)
    v_ref[...] = _dot(xb16, wv_ref[...])
    rs_ref[...] = _silu(_dot(xb16, wr_ref[...]))
    xb_ref[...] = _dot(xb16, wxb_ref[...])
    gg_ref[...] = _gelu_tanh(_dot(xb16, wgb_ref[...]))


def _even_in(x, consts, *, tm=512):
    n, d = x.shape
    row = lambda w: pl.BlockSpec((tm, w), lambda i: (i, 0))
    return pl.pallas_call(
        _even_in_kernel,
        out_shape=[jax.ShapeDtypeStruct((n, 3 * GLA_PAD), F32)] + [jax.ShapeDtypeStruct((n, LRU_WIDTH), F32)] * 4,
        grid=(n // tm,),
        in_specs=[row(d)] + [_const_spec(c.shape) for c in consts],
        out_specs=[row(3 * GLA_PAD)] + [row(LRU_WIDTH)] * 4,
        compiler_params=_params("parallel"),
        name="even_in",
    )(x, *consts)


GLA_SEQS_PER_STEP = 8
GLA_TIME_BLOCK = 3 * GLA_CHUNK


def _gla_kernel(*refs, tb, ns):
    qkl_refs, v_refs, rs_refs, s0_refs = (refs[i * ns:(i + 1) * ns] for i in range(4))
    tri_ref, gn_ref = refs[4 * ns:4 * ns + 2]
    o_refs = refs[4 * ns + 2:5 * ns + 2]
    sout_refs = refs[5 * ns + 2:6 * ns + 2]
    st_ref = refs[6 * ns + 2]
    t = pl.program_id(1)

    @pl.when(t == 0)
    def _init():
        for p in range(ns):
            st_ref[p] = s0_refs[p][0]

    tri = tri_ref[...]
    causal = tri > 0
    gn = gn_ref[...]

    def chunk(c, carry):
        rows = pl.ds(pl.multiple_of(c * GLA_CHUNK, GLA_CHUNK), GLA_CHUNK)
        seqs = range(ns)
        pairs = [(p, h) for p in seqs for h in range(GLA_HEADS)]
        sl = lambda h: slice(h * LANES, (h + 1) * LANES)
        las = [qkl_refs[p][rows, 2 * GLA_PAD:3 * GLA_PAD] for p in seqs]
        his = [la.astype(BF16) for la in las]
        r1s = [la - hi.astype(F32) for la, hi in zip(las, his)]
        mids = [r1.astype(BF16) for r1 in r1s]
        los = [(r1 - mid.astype(F32)).astype(BF16) for r1, mid in zip(r1s, mids)]
        parts = [[jnp.dot(tri, part, preferred_element_type=F32) for part in (hi, mid, lo)]
                 for hi, mid, lo in zip(his, mids, los)]
        cums = [a + b + c3 for a, b, c3 in parts]
        lasts = [cum[GLA_CHUNK - 1:GLA_CHUNK] for cum in cums]
        qes = [qkl_refs[p][rows, 0:GLA_PAD] * jnp.exp(cums[p]) for p in seqs]
        ks = [qkl_refs[p][rows, GLA_PAD:2 * GLA_PAD] for p in seqs]
        kes = [ks[p] * jnp.exp(-cums[p]) for p in seqs]
        kls = [ks[p] * jnp.exp(lasts[p] - cums[p]) for p in seqs]
        decs = [jnp.exp(last) for last in lasts]
        vs = [v_refs[p][rows, :] for p in seqs]
        sts = [st_ref[p, h] for p, h in pairs]
        atts = [_dot_nt(qes[p][:, sl(h)], kes[p][:, sl(h)]) for p, h in pairs]
        inters = [_dot_nt(qes[p][:, sl(h)], st) for (p, h), st in zip(pairs, sts)]
        upds = [_dot_tn(vs[p][:, sl(h)], kls[p][:, sl(h)]) for p, h in pairs]
        intras = [_dot(jnp.where(causal, att, 0.0), vs[p][:, sl(h)]) for (p, h), att in zip(pairs, atts)]
        for i, (p, h) in enumerate(pairs):
            st_ref[p, h] = sts[i] * decs[p][:, sl(h)] + upds[i]
            o = intras[i] + inters[i]
            on = o * lax.rsqrt(jnp.mean(o * o, axis=-1, keepdims=True) + LN_EPS) * gn
            o_refs[p][rows, sl(h)] = on * rs_refs[p][rows, sl(h)]
        return carry

    lax.fori_loop(0, tb // GLA_CHUNK, chunk, 0)

    @pl.when(t == pl.num_programs(1) - 1)
    def _fin():
        for p in range(ns):
            sout_refs[p][0] = st_ref[p]


def _gla(qkl, v, rs, s0t, gn, *, nb, t, tb):
    nt = t // tb
    ns = GLA_SEQS_PER_STEP
    assert nb % ns == 0
    per = nb // ns
    w = GLA_HEADS * GLA_DV
    tri = jnp.tril(jnp.ones((GLA_CHUNK, GLA_CHUNK), BF16))
    row_in = lambda width, p: pl.BlockSpec((tb, width), lambda g, i: ((p * per + g) * nt + i, 0))
    st_in = lambda p: pl.BlockSpec((1, GLA_HEADS, GLA_DV, LANES), lambda g, i: (p * per + g, 0, 0, 0))
    row_out = pl.BlockSpec((tb, w), lambda g, i: (g * nt + i, 0))
    st_out = pl.BlockSpec((1, GLA_HEADS, GLA_DV, LANES), lambda g, i: (g, 0, 0, 0))
    slots = range(ns)
    outs = pl.pallas_call(
        functools.partial(_gla_kernel, tb=tb, ns=ns),
        out_shape=[jax.ShapeDtypeStruct((per * t, w), F32)] * ns
        + [jax.ShapeDtypeStruct((per,) + s0t.shape[1:], F32)] * ns,
        grid=(per, nt),
        in_specs=[row_in(3 * GLA_PAD, p) for p in slots] + [row_in(w, p) for p in slots] + [row_in(w, p) for p in slots]
        + [st_in(p) for p in slots] + [_const_spec(tri.shape), _const_spec(gn.shape)],
        out_specs=[row_out] * ns + [st_out] * ns,
        scratch_shapes=[pltpu.VMEM((ns, GLA_HEADS, GLA_DV, LANES), F32)],
        compiler_params=_params("parallel", "arbitrary"),
        name="gla",
    )(*([qkl] * ns + [v] * ns + [rs] * ns + [s0t] * ns + [tri, gn]))
    return jnp.concatenate(outs[:ns], axis=0), jnp.concatenate(outs[ns:], axis=0)


def _lru_kernel(xb_ref, gg_ref, buf0_ref, h0_ref, cw_ref, cb_ref, wa_ref, ba_ref, wx_ref, bx_ref, lam_ref,
                y_ref, hlast_ref, cbuf_ref, xs_ref, a_ref, b_ref, hs_ref, hcar_ref, *, tb, t_last):
    t = pl.program_id(1)
    halo = SUBLANES

    @pl.when(t == 0)
    def _init():
        xs_ref[0:halo] = buf0_ref[0]
        hcar_ref[...] = h0_ref[0]

    xs_ref[halo:halo + tb] = xb_ref[...]
    xc = cb_ref[...] + sum(xs_ref[pl.ds(halo - (CONV_W - 1) + i, tb), :] * cw_ref[i:i + 1, :] for i in range(CONV_W))
    ga = _sigmoid(_dot(xc, wa_ref[...]) + ba_ref[...])
    gx = _sigmoid(_dot(xc, wx_ref[...]) + bx_ref[...])
    log_at = ga * (-LRU_C * _softplus(-lam_ref[...]))
    a = jnp.exp(log_at)
    a_ref[...] = a
    b_ref[...] = jnp.sqrt(-jnp.tanh(log_at) * (a * a + 1.0)) * (gx * xc)

    def step(i, h):
        h = a_ref[pl.ds(i, 1), :] * h + b_ref[pl.ds(i, 1), :]
        hs_ref[pl.ds(i, 1), :] = h
        return h

    hcar_ref[...] = lax.fori_loop(0, tb, step, hcar_ref[...], unroll=8)
    y_ref[...] = hs_ref[...] * gg_ref[...]

    @pl.when(t == pl.num_programs(1) - 1)
    def _fin():
        hlast_ref[0] = hs_ref[t_last:t_last + 1, :]
        cbuf_ref[0] = xs_ref[halo + t_last - (CONV_W - 2):halo + t_last + 1, :]

    xs_ref[0:halo] = xs_ref[tb:tb + halo]


def _lru(xb, gg, buf0, h0, consts, *, nb, t, tb, t_last):
    nt = t // tb
    w = LRU_WIDTH
    row = pl.BlockSpec((tb, w), lambda b, i: (b * nt + i, 0))
    per_b = lambda r: pl.BlockSpec((1, r, w), lambda b, i: (b, 0, 0))
    return pl.pallas_call(
        functools.partial(_lru_kernel, tb=tb, t_last=t_last),
        out_shape=[jax.ShapeDtypeStruct((nb * t, w), F32), jax.ShapeDtypeStruct((nb, 1, w), F32),
                   jax.ShapeDtypeStruct((nb, CONV_W - 1, w), F32)],
        grid=(nb, nt),
        in_specs=[row, row, per_b(SUBLANES), per_b(1)] + [_const_spec(c.shape) for c in consts],
        out_specs=[row, per_b(1), per_b(CONV_W - 1)],
        scratch_shapes=[pltpu.VMEM((tb + 2 * SUBLANES, w), F32), pltpu.VMEM((tb, w), F32), pltpu.VMEM((tb, w), F32),
                        pltpu.VMEM((tb, w), F32), pltpu.VMEM((1, w), F32)],
        compiler_params=_params("parallel", "arbitrary"),
        name="lru",
    )(xb, gg, buf0, h0, *consts)


def _mix_out_kernel(a_ref, b_ref, x_ref, wa_ref, wb_ref, g_ref, beta_ref, o_ref):
    m = _dot(a_ref[...], wa_ref[...]) + _dot(b_ref[...], wb_ref[...])
    o_ref[...] = _layer_norm_rows(DN_ALPHA * x_ref[...] + m, g_ref[...], beta_ref[...])


def _mix_out(a, b, x, wa, wb, g, beta, *, tm=512):
    n, d = x.shape
    row = lambda w: pl.BlockSpec((tm, w), lambda i: (i, 0))
    consts = [wa, wb, g, beta]
    return pl.pallas_call(
        _mix_out_kernel,
        out_shape=jax.ShapeDtypeStruct((n, d), F32),
        grid=(n // tm,),
        in_specs=[row(a.shape[1]), row(b.shape[1]), row(d)] + [_const_spec(c.shape) for c in consts],
        out_specs=row(d),
        compiler_params=_params("parallel"),
        name="mix_out",
    )(a, b, x, *consts)


def _pad_heads(w, heads, width):
    lead = w.shape[:-1]
    w = w.reshape(lead + (heads, width))
    w = jnp.pad(w, [(0, 0)] * len(lead) + [(0, 0), (0, LANES - width)])
    return w.reshape(lead + (heads * LANES,))


def _even_weights(w_in, w_gate, b_gate, gla_norm, conv_w, conv_b, lru_wa, lru_ba, lru_wx, lru_bx, lru_lambda, w_out):
    gk = GLA_HEADS * GLA_DK
    gv = GLA_HEADS * GLA_DV
    o = 0
    cols = {}
    for name, width in (("q", gk), ("k", gk), ("v", gv), ("lr", GLA_RANK), ("r", gv), ("xb", LRU_WIDTH), ("gb", LRU_WIDTH)):
        cols[name] = w_in[:, o:o + width]
        o += width
    wlr = jnp.pad(cols["lr"], ((0, 0), (0, LANES - GLA_RANK)))
    wg = jnp.pad(_pad_heads(w_gate, GLA_HEADS, GLA_DK), ((0, LANES - GLA_RANK), (0, 0)))
    bg = _pad_heads(b_gate[None], GLA_HEADS, GLA_DK)
    in_consts = [_pad_heads(cols["q"], GLA_HEADS, GLA_DK).astype(BF16), _pad_heads(cols["k"], GLA_HEADS, GLA_DK).astype(BF16),
                 cols["v"].astype(BF16), cols["r"].astype(BF16), cols["xb"].astype(BF16), cols["gb"].astype(BF16),
                 wlr.astype(BF16), wg.astype(BF16), bg]
    eye = jnp.eye(LRU_BLOCKS, dtype=F32)
    bd = lambda w: (eye[:, None, :, None] * w[:, :, None, :]).reshape(LRU_WIDTH, LRU_WIDTH).astype(BF16)
    lru_consts = [conv_w, conv_b[None], bd(lru_wa), lru_ba[None], bd(lru_wx), lru_bx[None], lru_lambda[None]]
    return in_consts, gla_norm[None], lru_consts, w_out[:gv].astype(BF16), w_out[gv:].astype(BF16)


def _seq_rows(a, off, nb, t, t_pad):
    a = a[off:off + nb * t].reshape(nb, t, a.shape[1])
    return jnp.pad(a, ((0, 0), (0, t_pad - t), (0, 0))).reshape(nb * t_pad, a.shape[2])


def _unpad_rows(a, nb, t, t_pad):
    return a.reshape(nb, t_pad, a.shape[1])[:, :t].reshape(nb * t, a.shape[1])


def _even_layer(x, groups, weights, g, beta, *, tm):
    in_consts, gn, lru_consts, wo_a, wo_b = weights
    qkl, v, rs, xb, gg = _even_in(x, in_consts, tm=tm)
    ogs, yls, states = [], [], []
    for off, nb, t, tb, s_gla, h_lru, cbuf in groups:
        s0t = jnp.pad(jnp.swapaxes(s_gla, 2, 3), ((0, 0), (0, 0), (0, 0), (0, LANES - GLA_DK)))
        buf0 = jnp.pad(cbuf, ((0, 0), (SUBLANES - (CONV_W - 1), 0), (0, 0)))
        if tb is None:
            tg = GLA_CHUNK
            tl = SUBLANES
            og, st = _gla(_seq_rows(qkl, off, nb, t, tg), _seq_rows(v, off, nb, t, tg), _seq_rows(rs, off, nb, t, tg),
                          s0t, gn, nb=nb, t=tg, tb=tg)
            yl, hl, cb = _lru(_seq_rows(xb, off, nb, t, tl), _seq_rows(gg, off, nb, t, tl), buf0, h_lru[:, None],
                              lru_consts, nb=nb, t=tl, tb=tl, t_last=t - 1)
            og = _unpad_rows(og, nb, t, tg)
            yl = _unpad_rows(yl, nb, t, tl)
        else:
            og, st = _gla(qkl, v, rs, s0t, gn, nb=nb, t=t, tb=GLA_TIME_BLOCK)
            yl, hl, cb = _lru(xb, gg, buf0, h_lru[:, None], lru_consts, nb=nb, t=t, tb=tb, t_last=tb - 1)
        ogs.append(og)
        yls.append(yl)
        states.append((jnp.swapaxes(st[..., :GLA_DK], 2, 3), hl[:, 0], cb))
    n_used = sum(o.shape[0] for o in ogs)
    tail = jnp.zeros((x.shape[0] - n_used, LRU_WIDTH), F32)
    og = jnp.concatenate(ogs + [tail], axis=0)
    yl = jnp.concatenate(yls + [tail], axis=0)
    return _mix_out(og, yl, x, wo_a, wo_b, g, beta, tm=tm), states


RWKV_STREAMS = 5


def _rwkv_in_kernel(*refs, has_vres, tm, long_seqs, short_seqs):
    if has_vres:
        (x_ref, halo_ref, sl_ref, ov_ref, mu_ref, wr_ref, wk_ref, wv_ref, w1_ref, w2_ref, a1_ref, a2_ref, g1_ref, g2_ref,
         w0_ref, a0_ref, vf_ref, v1_ref, v2_ref, v0_ref, p_ref, g_ref, xs_ref) = refs
    else:
        (x_ref, halo_ref, sl_ref, ov_ref, mu_ref, wr_ref, wk_ref, wv_ref, w1_ref, w2_ref, a1_ref, a2_ref, g1_ref, g2_ref,
         w0_ref, a0_ref, p_ref, g_ref, xs_ref) = refs
    d = D_MODEL
    x = x_ref[...]
    r0 = pl.program_id(0) * tm
    xs_ref[0:SUBLANES] = halo_ref[...]
    xs_ref[SUBLANES:SUBLANES + tm] = x
    prev = xs_ref[pl.ds(SUBLANES - 1, tm), :]
    rows = r0 + lax.broadcasted_iota(jnp.int32, (tm, 1), 0)
    off, cnt, length = long_seqs
    b = (jnp.maximum(r0 - off, 0) + (length - 1)) // length
    hit = jnp.logical_and(rows == off + b * length, b < cnt)
    prev = jnp.where(hit, sl_ref[pl.ds(jnp.minimum(b, cnt - 1), 1), :], prev)
    off, cnt, length = short_seqs
    rel = rows - off
    first = jnp.logical_and(jnp.logical_and(rel >= 0, rel < cnt * length), jnp.bitwise_and(rel, length - 1) == 0)
    prev = jnp.where(first, ov_ref[...], prev)
    xx = prev - x
    mix = lambda i: (x + xx * mu_ref[i:i + 1, :]).astype(BF16)
    xr, xw, xk, xv, xa, xg = (mix(i) for i in range(6))
    p_ref[:, 0:d] = _dot(xr, wr_ref[...])
    p_ref[:, d:2 * d] = _dot(xk, wk_ref[...])
    v = _dot(xv, wv_ref[...])
    if has_vres:
        v = v + (vf_ref[...] - v) * _sigmoid(v0_ref[...] + _dot(_dot(xv, v1_ref[...]), v2_ref[...]))
    p_ref[:, 2 * d:3 * d] = v
    p_ref[:, 3 * d:4 * d] = w0_ref[...] + _dot(jnp.tanh(_dot(xw, w1_ref[...])), w2_ref[...])
    p_ref[:, 4 * d:5 * d] = a0_ref[...] + _dot(_dot(xa, a1_ref[...]), a2_ref[...])
    g_ref[...] = _dot(_sigmoid(_dot(xg, g1_ref[...])), g2_ref[...])


def _rwkv_in(x, shift_long, shift_short, consts, vres, *, tm, long_seqs, short_seqs):
    n, d = x.shape
    off, cnt, length = short_seqs
    assert length & (length - 1) == 0 and long_seqs[2] >= tm and tm % SUBLANES == 0
    tile0 = off // tm
    tiles = -(-(off + cnt * length) // tm) - tile0
    ov = jnp.pad(shift_short[:, None], ((0, 0), (0, length - 1), (0, 0))).reshape(cnt * length, d)
    ov = jnp.pad(ov, ((off - tile0 * tm, tiles * tm - (off - tile0 * tm) - cnt * length), (0, 0)))
    sl = jnp.pad(shift_long, ((0, -shift_long.shape[0] % SUBLANES), (0, 0)))
    row = lambda w: pl.BlockSpec((tm, w), lambda i: (i, 0))
    halo = pl.BlockSpec((SUBLANES, d), lambda i: (jnp.maximum(i * (tm // SUBLANES) - 1, 0), 0))
    ov_spec = pl.BlockSpec((tm, d), lambda i: (jnp.clip(i - tile0, 0, tiles - 1), 0))
    args = [x, x, sl, ov] + list(consts)
    specs = [row(d), halo, _const_spec(sl.shape), ov_spec] + [_const_spec(c.shape) for c in consts]
    if vres is not None:
        vf, vconsts = vres
        args += [vf] + list(vconsts)
        specs += [row(d)] + [_const_spec(c.shape) for c in vconsts]
    return pl.pallas_call(
        functools.partial(_rwkv_in_kernel, has_vres=vres is not None, tm=tm, long_seqs=long_seqs, short_seqs=short_seqs),
        out_shape=[jax.ShapeDtypeStruct((n, RWKV_STREAMS * d), F32), jax.ShapeDtypeStruct((n, d), F32)],
        grid=(n // tm,),
        in_specs=specs,
        out_specs=[row(RWKV_STREAMS * d), row(d)],
        scratch_shapes=[pltpu.VMEM((tm + SUBLANES, d), F32)],
        compiler_params=_params("parallel"),
        name="rwkv_in",
    )(*args)


def _wkv_kernel(p_ref, s0_ref, kk_ref, ka_ref, rk_ref, lg_ref, lb_ref, y_ref, sout_ref,
                s_ref, dec_ref, a_ref, b_ref, km_ref, bon_ref, *, tt):
    t = pl.program_id(1)
    hk = RWKV_HEAD

    @pl.when(t == 0)
    def _init():
        s_ref[...] = s0_ref[...]

    r = p_ref[:, 0]
    k = p_ref[:, 1]
    v = p_ref[:, 2]
    dec_ref[...] = jnp.exp(-jnp.exp(-_softplus(-p_ref[:, 3]) - 0.5))
    ag = _sigmoid(p_ref[:, 4])
    kk = k * kk_ref[...][None]
    kk = kk / jnp.maximum(jnp.sqrt(jnp.sum(kk * kk, axis=1, keepdims=True)), 1e-12)
    km = k * (1.0 + (ag - 1.0) * ka_ref[...][None])
    km_ref[...] = km
    a_ref[...] = -kk
    b_ref[...] = kk * ag
    bon_ref[...] = jnp.sum(r * km * rk_ref[...][None], axis=1, keepdims=True) * v
    lg = lg_ref[...]
    lb = lb_ref[...]

    def step(i, carry):
        lanes = 4
        parts = [jnp.zeros((hk, LANES), F32) for _ in range(lanes)]
        for q in range(hk):
            parts[q % lanes] = parts[q % lanes] + s_ref[q] * a_ref[i, q:q + 1, :]
        sa = (parts[0] + parts[1]) + (parts[2] + parts[3])
        vv = p_ref[i, 2]
        parts = [jnp.zeros((hk, LANES), F32) for _ in range(lanes)]
        for q in range(hk):
            sn = s_ref[q] * dec_ref[i, q:q + 1, :] + (sa * b_ref[i, q:q + 1, :] + vv * km_ref[i, q:q + 1, :])
            s_ref[q] = sn
            parts[q % lanes] = parts[q % lanes] + sn * p_ref[i, 0, q:q + 1, :]
        y = (parts[0] + parts[1]) + (parts[2] + parts[3])
        mu = jnp.mean(y, axis=0, keepdims=True)
        dlt = y - mu
        var = jnp.mean(dlt * dlt, axis=0, keepdims=True)
        y_ref[i] = dlt * lax.rsqrt(var + RWKV_GN_EPS) * lg + lb + bon_ref[i]
        return carry

    lax.fori_loop(0, tt, step, 0)

    @pl.when(t == pl.num_programs(1) - 1)
    def _fin():
        sout_ref[...] = s_ref[...]


def _wkv(p, s0, lane_consts, *, tt):
    t, _, hk, l = p.shape
    lane2 = pl.BlockSpec((hk, LANES), lambda g, i: (0, g))
    st = pl.BlockSpec((hk, hk, LANES), lambda g, i: (0, 0, g))
    seq = lambda: pltpu.VMEM((tt, hk, LANES), F32)
    return pl.pallas_call(
        functools.partial(_wkv_kernel, tt=tt),
        out_shape=[jax.ShapeDtypeStruct((t, hk, l), F32), jax.ShapeDtypeStruct((hk, hk, l), F32)],
        grid=(l // LANES, t // tt),
        in_specs=[pl.BlockSpec((tt, RWKV_STREAMS, hk, LANES), lambda g, i: (i, 0, 0, g)), st] + [lane2] * 5,
        out_specs=[pl.BlockSpec((tt, hk, LANES), lambda g, i: (i, 0, g)), st],
        scratch_shapes=[pltpu.VMEM((hk, hk, LANES), F32), seq(), seq(), seq(), seq(), seq()],
        compiler_params=_params("parallel", "arbitrary"),
        name="wkv",
    )(p, s0, *lane_consts)


def _gate_out_kernel(y_ref, g_ref, x_ref, w_ref, lg_ref, lb_ref, o_ref):
    m = _dot(y_ref[...] * g_ref[...], w_ref[...])
    o_ref[...] = _layer_norm_rows(DN_ALPHA * x_ref[...] + m, lg_ref[...], lb_ref[...])


def _gate_out(y, g, x, w, lg, lb, *, tm):
    n, d = x.shape
    row = pl.BlockSpec((tm, d), lambda i: (i, 0))
    consts = [w, lg, lb]
    return pl.pallas_call(
        _gate_out_kernel,
        out_shape=jax.ShapeDtypeStruct((n, d), F32),
        grid=(n // tm,),
        in_specs=[row, row, row] + [_const_spec(c.shape) for c in consts],
        out_specs=row,
        compiler_params=_params("parallel"),
        name="gate_out",
    )(y, g, x, *consts)


def _pad_cols(w):
    return jnp.pad(w, ((0, 0), (0, LANES - w.shape[1]))).astype(BF16)


def _pad_rows(w):
    return jnp.pad(w, ((0, LANES - w.shape[0]), (0, 0))).astype(BF16)


def _rwkv_weights(mu, w_r, w_k, w_v, w_o, w0, w1, w2, a0, a1, a2, g1, g2, k_k, k_a, r_k, lnx_g, lnx_b, vres):
    consts = [jnp.pad(mu, ((0, SUBLANES - mu.shape[0]), (0, 0))), w_r.astype(BF16), w_k.astype(BF16), w_v.astype(BF16),
              _pad_cols(w1), _pad_rows(w2), _pad_cols(a1), _pad_rows(a2), _pad_cols(g1), _pad_rows(g2), w0[None], a0[None]]
    vconsts = None if vres is None else [_pad_cols(vres[1]), _pad_rows(vres[2]), vres[0][None]]
    per_head = [k_k.reshape(RWKV_HEADS, RWKV_HEAD).T, k_a.reshape(RWKV_HEADS, RWKV_HEAD).T, r_k.T,
                lnx_g.reshape(RWKV_HEADS, RWKV_HEAD).T, lnx_b.reshape(RWKV_HEADS, RWKV_HEAD).T]
    return consts, vconsts, per_head, w_o.astype(BF16)


def _rwkv_layer(x, groups, weights, v_first, g, beta, *, tm):
    consts, vconsts, per_head, w_o = weights
    n, d = x.shape
    (off_l, nb_l, t_l, _, _, shift_l), (off_s, nb_s, t_s, _, _, shift_s) = groups
    n_used = nb_l * t_l + nb_s * t_s
    vres = None if vconsts is None else (v_first, vconsts)
    p, gate = _rwkv_in(x, shift_l, shift_s, consts, vres, tm=tm,
                       long_seqs=(off_l, nb_l, t_l), short_seqs=(off_s, nb_s, t_s))
    ys, states = [], []
    for off, nb, t, tt, s, shift in groups:
        lanes = nb * RWKV_HEADS
        pg = p[off:off + nb * t].reshape(nb, t, RWKV_STREAMS, RWKV_HEADS, RWKV_HEAD)
        pg = jnp.transpose(pg, (1, 2, 4, 0, 3)).reshape(t, RWKV_STREAMS, RWKV_HEAD, lanes)
        hk2 = RWKV_HEAD * RWKV_HEAD
        s0 = jnp.swapaxes(s, 2, 3).reshape(lanes, hk2).T.reshape(RWKV_HEAD, RWKV_HEAD, lanes)
        lane_consts = [jnp.tile(c, (1, nb)) for c in per_head]
        y, s_out = _wkv(pg, s0, lane_consts, tt=tt)
        yt = jnp.swapaxes(y, 1, 2).reshape(t, nb, d)
        ys.append(jnp.swapaxes(yt, 0, 1).reshape(nb * t, d))
        s_new = jnp.swapaxes(s_out.reshape(hk2, lanes).T.reshape(nb, RWKV_HEADS, RWKV_HEAD, RWKV_HEAD), 2, 3)
        states.append((s_new, x[off:off + nb * t].reshape(nb, t, d)[:, -1]))
    y = jnp.concatenate(ys + [jnp.zeros((n - n_used, d), F32)], axis=0)
    return _gate_out(y, gate, x, w_o, g, beta, tm=tm), states, p[:, 2 * d:3 * d]


TOKEN_TILE = 512
PEER_TOKEN_TILE = 1024
PROMPT_TIME_BLOCK = 688
WKV_TIME_BLOCK = 16


def kernel(x_prompt, x_sample, state_gla, state_lru_h, state_lru_conv, state_rwkv, state_rwkv_shift, meta_tokens, ln_g, ln_b, ev_w_in, ev_gla_w_gate, ev_gla_b_gate, ev_gla_norm, ev_conv_w, ev_conv_b, ev_lru_wa, ev_lru_ba, ev_lru_wx, ev_lru_bx, ev_lru_lambda, ev_w_out, od_mu, od_w_r, od_w_k, od_w_v, od_w_o, od_w0, od_w1, od_w2, od_a0, od_a1, od_a2, od_v0, od_v1, od_v2, od_g1, od_g2, od_k_k, od_k_a, od_r_k, od_lnx_g, od_lnx_b, peer_w_q, peer_keys, peer_u, peer_v):
    bp, sp, d = x_prompt.shape
    bs, ss, _ = x_sample.shape
    tp = sp + N_META
    n_p, n_s = bp * tp, bs * ss
    n = -(-(n_p + n_s) // PEER_TOKEN_TILE) * PEER_TOKEN_TILE
    assert tp % PROMPT_TIME_BLOCK == 0 and tp % GLA_TIME_BLOCK == 0 and tp % WKV_TIME_BLOCK == 0 and n % TOKEN_TILE == 0

    xp = jnp.concatenate([jnp.broadcast_to(meta_tokens[None], (bp, N_META, d)), x_prompt], axis=1)
    x = jnp.concatenate([xp.reshape(n_p, d), x_sample.reshape(n_s, d), jnp.zeros((n - n_p - n_s, d), F32)], axis=0)

    n_pairs = DEPTH // 2
    zeros = lambda *s: jnp.zeros(s, F32)
    outs = {k: [] for k in ("p_gla", "p_h", "p_conv", "p_rwkv", "p_shift", "s_gla", "s_h", "s_conv", "s_rwkv", "s_shift")}
    v_first = None
    for layer in range(DEPTH):
        j = layer // 2
        g, beta = ln_g[layer, 0][None], ln_b[layer, 0][None]
        if layer % 2 == 0:
            weights = _even_weights(ev_w_in[j], ev_gla_w_gate[j], ev_gla_b_gate[j], ev_gla_norm[j], ev_conv_w[j], ev_conv_b[j],
                                    ev_lru_wa[j], ev_lru_ba[j], ev_lru_wx[j], ev_lru_bx[j], ev_lru_lambda[j], ev_w_out[j])
            groups = [(0, bp, tp, PROMPT_TIME_BLOCK, zeros(bp, GLA_HEADS, GLA_DK, GLA_DV), zeros(bp, LRU_WIDTH),
                       zeros(bp, CONV_W - 1, LRU_WIDTH)),
                      (n_p, bs, ss, None, state_gla[j], state_lru_h[j], state_lru_conv[j])]
            x, states = _even_layer(x, groups, weights, g, beta, tm=TOKEN_TILE)
            for pre, st in zip("ps", states):
                outs[pre + "_gla"].append(st[0])
                outs[pre + "_h"].append(st[1])
                outs[pre + "_conv"].append(st[2])
        else:
            vres = None if j == 0 else (od_v0[j - 1], od_v1[j - 1], od_v2[j - 1])
            weights = _rwkv_weights(od_mu[j], od_w_r[j], od_w_k[j], od_w_v[j], od_w_o[j], od_w0[j], od_w1[j], od_w2[j],
                                    od_a0[j], od_a1[j], od_a2[j], od_g1[j], od_g2[j], od_k_k[j], od_k_a[j], od_r_k[j],
                                    od_lnx_g[j], od_lnx_b[j], vres)
            groups = [(0, bp, tp, WKV_TIME_BLOCK, zeros(bp, RWKV_HEADS, RWKV_HEAD, RWKV_HEAD), zeros(bp, d)),
                      (n_p, bs, ss, ss, state_rwkv[j], state_rwkv_shift[j])]
            x, states, v = _rwkv_layer(x, groups, weights, v_first, g, beta, tm=TOKEN_TILE)
            if v_first is None:
                v_first = v
            for pre, st in zip("ps", states):
                outs[pre + "_rwkv"].append(st[0])
                outs[pre + "_shift"].append(st[1])
        x = _peer_layer(x, peer_w_q[layer].T.astype(BF16), peer_keys[layer, :, 0].astype(BF16),
                        peer_keys[layer, :, 1].astype(BF16), peer_u[layer].astype(BF16), peer_v[layer].astype(BF16),
                        ln_g[layer, 1][None], ln_b[layer, 1][None], tn=PEER_TOKEN_TILE)

    y_prompt = x[:n_p].reshape(bp, tp, d)[:, N_META:]
    y_sample = x[n_p:n_p + n_s].reshape(bs, ss, d)
    st = {k: jnp.stack(v) for k, v in outs.items()}
    return (y_prompt, y_sample, st["p_gla"], st["p_h"], st["p_conv"], st["p_rwkv"], st["p_shift"],
            st["s_gla"], st["s_h"], st["s_conv"], st["s_rwkv"], st["s_shift"])
```

```python
import functools

import jax
import jax.numpy as jnp
from jax import lax
from jax.experimental import pallas as pl
from jax.experimental.pallas import tpu as pltpu

F32 = jnp.float32
BF16 = jnp.bfloat16

D_MODEL = 1024
DEPTH = 4
N_META = 16
GLA_HEADS = 4
GLA_DK = 64
GLA_DV = 128
GLA_RANK = 16
GLA_GATE_NORM = 16.0
GLA_CHUNK = 16
LRU_WIDTH = 512
LRU_BLOCKS = 8
CONV_W = 4
LRU_C = 8.0
RWKV_HEAD = 64
RWKV_HEADS = D_MODEL // RWKV_HEAD
RWKV_GN_EPS = 64e-5
PEER_HEADS = 8
PEER_NKEYS = 128
PEER_DKEY = 256
PEER_HALF = PEER_DKEY // 2
PEER_TOPK = 16
DN_ALPHA = float((2 * DEPTH) ** 0.25)
LN_EPS = 1e-5

LANES = 128
SUBLANES = 8
VMEM_LIMIT = 56 * 1024 * 1024


def _dot(a, b):
    return jnp.dot(a.astype(BF16), b.astype(BF16), preferred_element_type=F32)


def _dot_nt(a, b):
    return lax.dot_general(a.astype(BF16), b.astype(BF16), (((1,), (1,)), ((), ())), preferred_element_type=F32)


def _dot_tn(a, b):
    return lax.dot_general(a.astype(BF16), b.astype(BF16), (((0,), (0,)), ((), ())), preferred_element_type=F32)


def _layer_norm_rows(z, g, b):
    mu = jnp.mean(z, axis=-1, keepdims=True)
    d = z - mu
    var = jnp.mean(d * d, axis=-1, keepdims=True)
    return d * lax.rsqrt(var + LN_EPS) * g + b


def _sigmoid(x):
    return 1.0 / (1.0 + jnp.exp(-x))


def _softplus(x):
    return jnp.maximum(x, 0.0) + jnp.log1p(jnp.exp(-jnp.abs(x)))


def _gelu_tanh(x):
    c0 = 0.7978845608028654
    c1 = c0 * 0.044715
    one = jnp.asarray(1.0, x.dtype)
    return (0.5 * x) * (one + jnp.tanh(x * (c0 + c1 * (x * x))))


def _silu(x):
    return x * _sigmoid(x)


def _params(*sem):
    return pltpu.CompilerParams(dimension_semantics=sem, vmem_limit_bytes=VMEM_LIMIT)


def _const_spec(shape):
    nd = len(shape)
    return pl.BlockSpec(shape, lambda *_: (0,) * nd)


PEER_CAND_ROWS = 80
PEER_PACK = 2 * SUBLANES
PEER_SELECT_LANES = 2 * LANES


def _peer_cand_index():
    rows = [r1 * PEER_TOPK for r1 in range(16)]
    for j in range(1, 8):
        rows += [r1 * PEER_TOPK + j for r1 in range(8)]
    rows += list(range(8, 16))
    return jnp.broadcast_to(jnp.asarray(rows, F32)[:, None], (PEER_CAND_ROWS, PEER_SELECT_LANES))


PEER_CODE_STEP = 2.0 ** 116
PEER_CODED_BELOW = -(2.0 ** 119)


def _rank_code(r):
    return -PEER_CODE_STEP * (PEER_TOPK + r)


def _decode_rank(s):
    return jnp.where(s < PEER_CODED_BELOW, s * (-1.0 / PEER_CODE_STEP) - PEER_TOPK, float(PEER_TOPK))


def _count_coded(s):
    return jnp.sum(jnp.where(s < PEER_CODED_BELOW, 1.0, 0.0), axis=0, keepdims=True)


def _top16(scores, key_iota):
    row16 = lax.broadcasted_iota(jnp.int32, (PEER_TOPK, scores[0].shape[1]), 0)
    vals = [jnp.zeros((PEER_TOPK, s.shape[1]), F32) for s in scores]
    scores = list(scores)
    for r in range(PEER_TOPK):
        tops = [jnp.max(s, axis=0, keepdims=True) for s in scores]
        hits = [s == m for s, m in zip(scores, tops)]
        if key_iota is not None:
            firsts = [jnp.min(jnp.where(hit, key_iota, float(PEER_NKEYS)), axis=0, keepdims=True) for hit in hits]
            hits = [key_iota == first for first in firsts]
        scores = [jnp.where(hit, _rank_code(r), s) for hit, s in zip(hits, scores)]
        vals = [jnp.where(row16 == r, m, v) for m, v in zip(tops, vals)]
    return list(zip(vals, scores))


def _peer_candidates(a, b):
    blocks = [a + b[0:1]]
    for j in range(1, 8):
        blocks.append(a[0:8] + b[j:j + 1])
    blocks.append(a[0:1] + b[8:16])
    return jnp.concatenate(blocks, axis=0)


def _select16(c, cidx):
    for _ in range(PEER_TOPK):
        m = jnp.max(c, axis=0, keepdims=True)
        hit = c == m
        if cidx is not None:
            first = jnp.min(jnp.where(hit, cidx, 1e9), axis=0, keepdims=True)
            hit = cidx == first
        c = jnp.where(hit, _rank_code(0), c)
    return c


def _peer_gate_tiles(s1, s2, a, b, s1c, s2c, c, cc):
    sel = jnp.where(cc < PEER_CODED_BELOW, 1.0, 0.0)
    z = jnp.sum(sel * jnp.exp(c - c[0:1]), axis=0, keepdims=True)
    j_lo = sel[0:8]
    for j in range(1, 8):
        j_lo = j_lo + sel[8 + 8 * j:16 + 8 * j]
    extra = jnp.sum(sel[72:80], axis=0, keepdims=True)
    row8 = lax.broadcasted_iota(jnp.int32, j_lo.shape, 0)
    j_lo = j_lo + jnp.where(row8 == 0, extra, 0.0)
    jt = jnp.concatenate([j_lo, sel[8:16]], axis=0)
    r1 = _decode_rank(s1c)
    h1 = jnp.zeros_like(r1)
    for r in range(PEER_TOPK):
        h1 = jnp.where(r1 == float(r), jt[r:r + 1], h1)
    return h1, jnp.exp(s1 - a[0:1]) / z, _decode_rank(s2c), jnp.exp(s2 - b[0:1])


def _peer_kernel(x_ref, wqt_ref, k1_ref, k2_ref, cidx_ref, u_ref, vt_ref, g_ref, b_ref, o_ref,
                 xt_ref, acc_ref, h1_ref, c1_ref, r2_ref, e2_ref, s_ref, hd0_ref, hd1_ref, coef_ref, *, tn, te):
    j = pl.program_id(1)
    nj = pl.num_programs(1)

    @pl.when(j == 0)
    def _select():
        xt_ref[...] = x_ref[...].T.astype(BF16)
        acc_ref[...] = jnp.zeros_like(acc_ref)
        hd1_ref[...] = jnp.zeros_like(hd1_ref)
        coef_ref[0:te] = jnp.zeros((te, tn), BF16)
        key_iota = lax.broadcasted_iota(jnp.int32, (PEER_NKEYS, PEER_SELECT_LANES), 0).astype(F32)

        def head(h, carry):
            q = _dot(wqt_ref[pl.ds(pl.multiple_of(h * PEER_DKEY, PEER_DKEY), PEER_DKEY), :], xt_ref[...])
            mu = jnp.mean(q, axis=0, keepdims=True)
            d = q - mu
            qn = d * lax.rsqrt(jnp.mean(d * d, axis=0, keepdims=True) + LN_EPS)
            s_ref[0] = _dot(k1_ref[h], qn[0:PEER_HALF])
            s_ref[1] = _dot(k2_ref[h], qn[PEER_HALF:PEER_DKEY])

            def chunk(ci, carry2):
                ls = pl.ds(pl.multiple_of(ci * PEER_SELECT_LANES, PEER_SELECT_LANES), PEER_SELECT_LANES)
                s1 = s_ref[0, :, ls]
                s2 = s_ref[1, :, ls]

                def put(tiles):
                    h1_ref[h, :, ls] = tiles[0]
                    c1_ref[h, :, ls] = tiles[1]
                    r2_ref[h, :, ls] = tiles[2].astype(BF16)
                    e2_ref[h, :, ls] = tiles[3].astype(BF16)

                (a, s1c), (b, s2c) = _top16([s1, s2], None)
                c = _peer_candidates(a, b)
                cc = _select16(c, None)
                put(_peer_gate_tiles(s1, s2, a, b, s1c, s2c, c, cc))
                miscount = (jnp.abs(_count_coded(s1c) - PEER_TOPK) + jnp.abs(_count_coded(s2c) - PEER_TOPK)
                            + jnp.abs(_count_coded(cc) - PEER_TOPK))

                @pl.when(jnp.max(miscount) > 0.0)
                def _with_ties():
                    (a, s1c), (b, s2c) = _top16([s1, s2], key_iota)
                    c = _peer_candidates(a, b)
                    cc = _select16(c, cidx_ref[...])
                    put(_peer_gate_tiles(s1, s2, a, b, s1c, s2c, c, cc))

                return carry2

            lax.fori_loop(0, tn // PEER_SELECT_LANES, chunk, 0)
            return carry

        lax.fori_loop(0, PEER_HEADS, head, 0)

    groups = PEER_NKEYS // PEER_PACK
    last_key = PEER_NKEYS - 1
    zero_pack = jnp.zeros((PEER_PACK, LANES), F32)

    def gate_rows(tile):
        rows = []
        for ii in range(te // PEER_NKEYS):
            i1 = jnp.clip(tile * (te // PEER_NKEYS) + ii, 0, last_key)
            rows.append(([h1_ref[h, pl.ds(i1, 1), :] for h in range(PEER_HEADS)],
                         [c1_ref[h, pl.ds(i1, 1), :] for h in range(PEER_HEADS)]))
        return rows

    def gate_stage(rows, hd_ref, coef_rows, lane_groups):
        nk = len(rows)
        for c in lane_groups:
            ls = slice(c * LANES, (c + 1) * LANES)
            gates = [[jnp.zeros((PEER_PACK, LANES), BF16) for _ in range(groups)] for _ in range(nk)]
            for h in range(PEER_HEADS):
                bounds = [(rows[ii][0][h][:, ls] + zero_pack).astype(BF16) for ii in range(nk)]
                scales = [(rows[ii][1][h][:, ls] + zero_pack).astype(BF16) for ii in range(nk)]
                for gi in range(groups):
                    krows = slice(gi * PEER_PACK, (gi + 1) * PEER_PACK)
                    rank = r2_ref[h, krows, ls]
                    fac = e2_ref[h, krows, ls]
                    for ii in range(nk):
                        gates[ii][gi] = gates[ii][gi] + jnp.where(rank < bounds[ii], fac * scales[ii], jnp.zeros((), BF16))
            for ii in range(nk):
                for gi in range(groups):
                    erows = slice(ii * PEER_NKEYS + gi * PEER_PACK, ii * PEER_NKEYS + (gi + 1) * PEER_PACK)
                    crows = slice(coef_rows + erows.start, coef_rows + erows.stop)
                    coef_ref[crows, ls] = gates[ii][gi] * _gelu_tanh(hd_ref[erows, ls].astype(BF16))

    piece = 2 * LANES
    per = piece // LANES
    rows_odd = gate_rows(2 * j - 1)
    rows_even = gate_rows(2 * j)
    pieces = tn // piece
    lanes_of = lambda c: slice(c * piece, (c + 1) * piece)

    def down_project(c):
        acc_ref[:, lanes_of(c)] += jnp.dot(vt_ref[0], coef_ref[:, lanes_of(c)], preferred_element_type=F32)

    for c in range(pieces):
        gate_stage(rows_odd, hd1_ref, te, range(c * per, (c + 1) * per))
        hd0_ref[:, lanes_of(c)] = _dot(u_ref[0:te, :], xt_ref[:, lanes_of(c)])
        hd1_ref[:, lanes_of(c)] = _dot(u_ref[te:2 * te, :], xt_ref[:, lanes_of(c)])
        if c > 0:
            down_project(c - 1)
    down_project(pieces - 1)
    for c in range(pieces):
        gate_stage(rows_even, hd0_ref, 0, range(c * per, (c + 1) * per))

    @pl.when(j == nj - 1)
    def _finish():
        y = acc_ref[...].T
        o_ref[...] = _layer_norm_rows(DN_ALPHA * x_ref[...] + y, g_ref[...], b_ref[...])


def _peer_layer(x, wqt, k1, k2, u, v, g, b, *, tn=1024, te=256):
    n, d = x.shape
    ne = u.shape[0]
    assert ne % (2 * te) == 0 and n % tn == 0
    pairs = ne // (2 * te)
    steps = pairs + 1
    vt = jnp.swapaxes(v.reshape(pairs, 2 * te, d), 1, 2)
    kern = functools.partial(_peer_kernel, tn=tn, te=te)
    one = pl.Buffered(1)
    return pl.pallas_call(
        kern,
        out_shape=jax.ShapeDtypeStruct((n, d), F32),
        grid=(n // tn, steps),
        in_specs=[
            pl.BlockSpec((tn, d), lambda i, j: (i, 0), pipeline_mode=one),
            pl.BlockSpec(wqt.shape, lambda i, j: (0, 0), pipeline_mode=one),
            pl.BlockSpec(k1.shape, lambda i, j: (0, 0, 0), pipeline_mode=one),
            pl.BlockSpec(k2.shape, lambda i, j: (0, 0, 0), pipeline_mode=one),
            pl.BlockSpec((PEER_CAND_ROWS, PEER_SELECT_LANES), lambda i, j: (0, 0), pipeline_mode=one),
            pl.BlockSpec((2 * te, d), lambda i, j: (jnp.minimum(j, pairs - 1), 0)),
            pl.BlockSpec((1, d, 2 * te), lambda i, j: (jnp.maximum(j - 1, 0), 0, 0)),
            pl.BlockSpec((1, d), lambda i, j: (0, 0), pipeline_mode=one),
            pl.BlockSpec((1, d), lambda i, j: (0, 0), pipeline_mode=one),
        ],
        out_specs=pl.BlockSpec((tn, d), lambda i, j: (i, 0), pipeline_mode=one),
        scratch_shapes=[
            pltpu.VMEM((d, tn), BF16),
            pltpu.VMEM((d, tn), F32),
            pltpu.VMEM((PEER_HEADS, PEER_NKEYS, tn), F32),
            pltpu.VMEM((PEER_HEADS, PEER_NKEYS, tn), F32),
            pltpu.VMEM((PEER_HEADS, PEER_NKEYS, tn), BF16),
            pltpu.VMEM((PEER_HEADS, PEER_NKEYS, tn), BF16),
            pltpu.VMEM((2, PEER_NKEYS, tn), F32),
            pltpu.VMEM((te, tn), F32),
            pltpu.VMEM((te, tn), F32),
            pltpu.VMEM((2 * te, tn), BF16),
        ],
        compiler_params=_params("parallel", "arbitrary"),
        name="peer",
    )(x, wqt, k1, k2, _peer_cand_index(), u, vt, g, b)


GLA_PAD = GLA_HEADS * LANES


def _even_in_kernel(x_ref, wq_ref, wk_ref, wv_ref, wr_ref, wxb_ref, wgb_ref, wlr_ref, wg_ref, bg_ref,
                    qkl_ref, v_ref, rs_ref, xb_ref, gg_ref):
    xb16 = x_ref[...].astype(BF16)
    qkl_ref[:, 0:GLA_PAD] = _dot(xb16, wq_ref[...]) * (GLA_DK ** -0.5)
    qkl_ref[:, GLA_PAD:2 * GLA_PAD] = _dot(xb16, wk_ref[...])
    glr = _dot(xb16, wlr_ref[...])
    z = _dot(glr, wg_ref[...]) + bg_ref[...]
    qkl_ref[:, 2 * GLA_PAD:3 * GLA_PAD] = -_softplus(-z) * (1.0 / GLA_GATE_NORM)
    v_ref[...] = _dot(xb16, wv_ref[...])
    rs_ref[...] = _silu(_dot(xb16, wr_ref[...]))
    xb_ref[...] = _dot(xb16, wxb_ref[...])
    gg_ref[...] = _gelu_tanh(_dot(xb16, wgb_ref[...]))


def _even_in(x, consts, *, tm=512):
    n, d = x.shape
    row = lambda w: pl.BlockSpec((tm, w), lambda i: (i, 0))
    return pl.pallas_call(
        _even_in_kernel,
        out_shape=[jax.ShapeDtypeStruct((n, 3 * GLA_PAD), F32)] + [jax.ShapeDtypeStruct((n, LRU_WIDTH), F32)] * 4,
        grid=(n // tm,),
        in_specs=[row(d)] + [_const_spec(c.shape) for c in consts],
        out_specs=[row(3 * GLA_PAD)] + [row(LRU_WIDTH)] * 4,
        compiler_params=_params("parallel"),
        name="even_in",
    )(x, *consts)


GLA_SEQS_PER_STEP = 8
GLA_TIME_BLOCK = 3 * GLA_CHUNK


def _gla_kernel(*refs, tb, ns):
    qkl_refs, v_refs, rs_refs, s0_refs = (refs[i * ns:(i + 1) * ns] for i in range(4))
    tri_ref, gn_ref = refs[4 * ns:4 * ns + 2]
    o_refs = refs[4 * ns + 2:5 * ns + 2]
    sout_refs = refs[5 * ns + 2:6 * ns + 2]
    st_ref = refs[6 * ns + 2]
    t = pl.program_id(1)

    @pl.when(t == 0)
    def _init():
        for p in range(ns):
            st_ref[p] = s0_refs[p][0]

    tri = tri_ref[...]
    causal = tri > 0
    gn = gn_ref[...]

    def chunk(c, carry):
        rows = pl.ds(pl.multiple_of(c * GLA_CHUNK, GLA_CHUNK), GLA_CHUNK)
        seqs = range(ns)
        pairs = [(p, h) for p in seqs for h in range(GLA_HEADS)]
        sl = lambda h: slice(h * LANES, (h + 1) * LANES)
        las = [qkl_refs[p][rows, 2 * GLA_PAD:3 * GLA_PAD] for p in seqs]
        his = [la.astype(BF16) for la in las]
        r1s = [la - hi.astype(F32) for la, hi in zip(las, his)]
        mids = [r1.astype(BF16) for r1 in r1s]
        los = [(r1 - mid.astype(F32)).astype(BF16) for r1, mid in zip(r1s, mids)]
        parts = [[jnp.dot(tri, part, preferred_element_type=F32) for part in (hi, mid, lo)]
                 for hi, mid, lo in zip(his, mids, los)]
        cums = [a + b + c3 for a, b, c3 in parts]
        lasts = [cum[GLA_CHUNK - 1:GLA_CHUNK] for cum in cums]
        qes = [qkl_refs[p][rows, 0:GLA_PAD] * jnp.exp(cums[p]) for p in seqs]
        ks = [qkl_refs[p][rows, GLA_PAD:2 * GLA_PAD] for p in seqs]
        kes = [ks[p] * jnp.exp(-cums[p]) for p in seqs]
        kls = [ks[p] * jnp.exp(lasts[p] - cums[p]) for p in seqs]
        decs = [jnp.exp(last) for last in lasts]
        vs = [v_refs[p][rows, :] for p in seqs]
        sts = [st_ref[p, h] for p, h in pairs]
        atts = [_dot_nt(qes[p][:, sl(h)], kes[p][:, sl(h)]) for p, h in pairs]
        inters = [_dot_nt(qes[p][:, sl(h)], st) for (p, h), st in zip(pairs, sts)]
        upds = [_dot_tn(vs[p][:, sl(h)], kls[p][:, sl(h)]) for p, h in pairs]
        intras = [_dot(jnp.where(causal, att, 0.0), vs[p][:, sl(h)]) for (p, h), att in zip(pairs, atts)]
        for i, (p, h) in enumerate(pairs):
            st_ref[p, h] = sts[i] * decs[p][:, sl(h)] + upds[i]
            o = intras[i] + inters[i]
            on = o * lax.rsqrt(jnp.mean(o * o, axis=-1, keepdims=True) + LN_EPS) * gn
            o_refs[p][rows, sl(h)] = on * rs_refs[p][rows, sl(h)]
        return carry

    lax.fori_loop(0, tb // GLA_CHUNK, chunk, 0)

    @pl.when(t == pl.num_programs(1) - 1)
    def _fin():
        for p in range(ns):
            sout_refs[p][0] = st_ref[p]


def _gla(qkl, v, rs, s0t, gn, *, nb, t, tb):
    nt = t // tb
    ns = GLA_SEQS_PER_STEP
    assert nb % ns == 0
    per = nb // ns
    w = GLA_HEADS * GLA_DV
    tri = jnp.tril(jnp.ones((GLA_CHUNK, GLA_CHUNK), BF16))
    row_in = lambda width, p: pl.BlockSpec((tb, width), lambda g, i: ((p * per + g) * nt + i, 0))
    st_in = lambda p: pl.BlockSpec((1, GLA_HEADS, GLA_DV, LANES), lambda g, i: (p * per + g, 0, 0, 0))
    row_out = pl.BlockSpec((tb, w), lambda g, i: (g * nt + i, 0))
    st_out = pl.BlockSpec((1, GLA_HEADS, GLA_DV, LANES), lambda g, i: (g, 0, 0, 0))
    slots = range(ns)
    outs = pl.pallas_call(
        functools.partial(_gla_kernel, tb=tb, ns=ns),
        out_shape=[jax.ShapeDtypeStruct((per * t, w), F32)] * ns
        + [jax.ShapeDtypeStruct((per,) + s0t.shape[1:], F32)] * ns,
        grid=(per, nt),
        in_specs=[row_in(3 * GLA_PAD, p) for p in slots] + [row_in(w, p) for p in slots] + [row_in(w, p) for p in slots]
        + [st_in(p) for p in slots] + [_const_spec(tri.shape), _const_spec(gn.shape)],
        out_specs=[row_out] * ns + [st_out] * ns,
        scratch_shapes=[pltpu.VMEM((ns, GLA_HEADS, GLA_DV, LANES), F32)],
        compiler_params=_params("parallel", "arbitrary"),
        name="gla",
    )(*([qkl] * ns + [v] * ns + [rs] * ns + [s0t] * ns + [tri, gn]))
    return jnp.concatenate(outs[:ns], axis=0), jnp.concatenate(outs[ns:], axis=0)


def _lru_kernel(xb_ref, gg_ref, buf0_ref, h0_ref, cw_ref, cb_ref, wa_ref, ba_ref, wx_ref, bx_ref, lam_ref,
                y_ref, hlast_ref, cbuf_ref, xs_ref, a_ref, b_ref, hs_ref, hcar_ref, *, tb, t_last):
    t = pl.program_id(1)
    halo = SUBLANES

    @pl.when(t == 0)
    def _init():
        xs_ref[0:halo] = buf0_ref[0]
        hcar_ref[...] = h0_ref[0]

    xs_ref[halo:halo + tb] = xb_ref[...]
    xc = cb_ref[...] + sum(xs_ref[pl.ds(halo - (CONV_W - 1) + i, tb), :] * cw_ref[i:i + 1, :] for i in range(CONV_W))
    ga = _sigmoid(_dot(xc, wa_ref[...]) + ba_ref[...])
    gx = _sigmoid(_dot(xc, wx_ref[...]) + bx_ref[...])
    log_at = ga * (-LRU_C * _softplus(-lam_ref[...]))
    a = jnp.exp(log_at)
    a_ref[...] = a
    b_ref[...] = jnp.sqrt(-jnp.tanh(log_at) * (a * a + 1.0)) * (gx * xc)

    def step(i, h):
        h = a_ref[pl.ds(i, 1), :] * h + b_ref[pl.ds(i, 1), :]
        hs_ref[pl.ds(i, 1), :] = h
        return h

    hcar_ref[...] = lax.fori_loop(0, tb, step, hcar_ref[...], unroll=8)
    y_ref[...] = hs_ref[...] * gg_ref[...]

    @pl.when(t == pl.num_programs(1) - 1)
    def _fin():
        hlast_ref[0] = hs_ref[t_last:t_last + 1, :]
        cbuf_ref[0] = xs_ref[halo + t_last - (CONV_W - 2):halo + t_last + 1, :]

    xs_ref[0:halo] = xs_ref[tb:tb + halo]


def _lru(xb, gg, buf0, h0, consts, *, nb, t, tb, t_last):
    nt = t // tb
    w = LRU_WIDTH
    row = pl.BlockSpec((tb, w), lambda b, i: (b * nt + i, 0))
    per_b = lambda r: pl.BlockSpec((1, r, w), lambda b, i: (b, 0, 0))
    return pl.pallas_call(
        functools.partial(_lru_kernel, tb=tb, t_last=t_last),
        out_shape=[jax.ShapeDtypeStruct((nb * t, w), F32), jax.ShapeDtypeStruct((nb, 1, w), F32),
                   jax.ShapeDtypeStruct((nb, CONV_W - 1, w), F32)],
        grid=(nb, nt),
        in_specs=[row, row, per_b(SUBLANES), per_b(1)] + [_const_spec(c.shape) for c in consts],
        out_specs=[row, per_b(1), per_b(CONV_W - 1)],
        scratch_shapes=[pltpu.VMEM((tb + 2 * SUBLANES, w), F32), pltpu.VMEM((tb, w), F32), pltpu.VMEM((tb, w), F32),
                        pltpu.VMEM((tb, w), F32), pltpu.VMEM((1, w), F32)],
        compiler_params=_params("parallel", "arbitrary"),
        name="lru",
    )(xb, gg, buf0, h0, *consts)


def _mix_out_kernel(a_ref, b_ref, x_ref, wa_ref, wb_ref, g_ref, beta_ref, o_ref):
    m = _dot(a_ref[...], wa_ref[...]) + _dot(b_ref[...], wb_ref[...])
    o_ref[...] = _layer_norm_rows(DN_ALPHA * x_ref[...] + m, g_ref[...], beta_ref[...])


def _mix_out(a, b, x, wa, wb, g, beta, *, tm=512):
    n, d = x.shape
    row = lambda w: pl.BlockSpec((tm, w), lambda i: (i, 0))
    consts = [wa, wb, g, beta]
    return pl.pallas_call(
        _mix_out_kernel,
        out_shape=jax.ShapeDtypeStruct((n, d), F32),
        grid=(n // tm,),
        in_specs=[row(a.shape[1]), row(b.shape[1]), row(d)] + [_const_spec(c.shape) for c in consts],
        out_specs=row(d),
        compiler_params=_params("parallel"),
        name="mix_out",
    )(a, b, x, *consts)


def _pad_heads(w, heads, width):
    lead = w.shape[:-1]
    w = w.reshape(lead + (heads, width))
    w = jnp.pad(w, [(0, 0)] * len(lead) + [(0, 0), (0, LANES - width)])
    return w.reshape(lead + (heads * LANES,))


def _even_weights(w_in, w_gate, b_gate, gla_norm, conv_w, conv_b, lru_wa, lru_ba, lru_wx, lru_bx, lru_lambda, w_out):
    gk = GLA_HEADS * GLA_DK
    gv = GLA_HEADS * GLA_DV
    o = 0
    cols = {}
    for name, width in (("q", gk), ("k", gk), ("v", gv), ("lr", GLA_RANK), ("r", gv), ("xb", LRU_WIDTH), ("gb", LRU_WIDTH)):
        cols[name] = w_in[:, o:o + width]
        o += width
    wlr = jnp.pad(cols["lr"], ((0, 0), (0, LANES - GLA_RANK)))
    wg = jnp.pad(_pad_heads(w_gate, GLA_HEADS, GLA_DK), ((0, LANES - GLA_RANK), (0, 0)))
    bg = _pad_heads(b_gate[None], GLA_HEADS, GLA_DK)
    in_consts = [_pad_heads(cols["q"], GLA_HEADS, GLA_DK).astype(BF16), _pad_heads(cols["k"], GLA_HEADS, GLA_DK).astype(BF16),
                 cols["v"].astype(BF16), cols["r"].astype(BF16), cols["xb"].astype(BF16), cols["gb"].astype(BF16),
                 wlr.astype(BF16), wg.astype(BF16), bg]
    eye = jnp.eye(LRU_BLOCKS, dtype=F32)
    bd = lambda w: (eye[:, None, :, None] * w[:, :, None, :]).reshape(LRU_WIDTH, LRU_WIDTH).astype(BF16)
    lru_consts = [conv_w, conv_b[None], bd(lru_wa), lru_ba[None], bd(lru_wx), lru_bx[None], lru_lambda[None]]
    return in_consts, gla_norm[None], lru_consts, w_out[:gv].astype(BF16), w_out[gv:].astype(BF16)


def _seq_rows(a, off, nb, t, t_pad):
    a = a[off:off + nb * t].reshape(nb, t, a.shape[1])
    return jnp.pad(a, ((0, 0), (0, t_pad - t), (0, 0))).reshape(nb * t_pad, a.shape[2])


def _unpad_rows(a, nb, t, t_pad):
    return a.reshape(nb, t_pad, a.shape[1])[:, :t].reshape(nb * t, a.shape[1])


def _even_layer(x, groups, weights, g, beta, *, tm):
    in_consts, gn, lru_consts, wo_a, wo_b = weights
    qkl, v, rs, xb, gg = _even_in(x, in_consts, tm=tm)
    ogs, yls, states = [], [], []
    for off, nb, t, tb, s_gla, h_lru, cbuf in groups:
        s0t = jnp.pad(jnp.swapaxes(s_gla, 2, 3), ((0, 0), (0, 0), (0, 0), (0, LANES - GLA_DK)))
        buf0 = jnp.pad(cbuf, ((0, 0), (SUBLANES - (CONV_W - 1), 0), (0, 0)))
        if tb is None:
            tg = GLA_CHUNK
            tl = SUBLANES
            og, st = _gla(_seq_rows(qkl, off, nb, t, tg), _seq_rows(v, off, nb, t, tg), _seq_rows(rs, off, nb, t, tg),
                          s0t, gn, nb=nb, t=tg, tb=tg)
            yl, hl, cb = _lru(_seq_rows(xb, off, nb, t, tl), _seq_rows(gg, off, nb, t, tl), buf0, h_lru[:, None],
                              lru_consts, nb=nb, t=tl, tb=tl, t_last=t - 1)
            og = _unpad_rows(og, nb, t, tg)
            yl = _unpad_rows(yl, nb, t, tl)
        else:
            og, st = _gla(qkl, v, rs, s0t, gn, nb=nb, t=t, tb=GLA_TIME_BLOCK)
            yl, hl, cb = _lru(xb, gg, buf0, h_lru[:, None], lru_consts, nb=nb, t=t, tb=tb, t_last=tb - 1)
        ogs.append(og)
        yls.append(yl)
        states.append((jnp.swapaxes(st[..., :GLA_DK], 2, 3), hl[:, 0], cb))
    n_used = sum(o.shape[0] for o in ogs)
    tail = jnp.zeros((x.shape[0] - n_used, LRU_WIDTH), F32)
    og = jnp.concatenate(ogs + [tail], axis=0)
    yl = jnp.concatenate(yls + [tail], axis=0)
    return _mix_out(og, yl, x, wo_a, wo_b, g, beta, tm=tm), states


RWKV_STREAMS = 5


def _rwkv_in_kernel(*refs, has_vres, tm, long_seqs, short_seqs):
    if has_vres:
        (x_ref, halo_ref, sl_ref, ov_ref, mu_ref, wr_ref, wk_ref, wv_ref, w1_ref, w2_ref, a1_ref, a2_ref, g1_ref, g2_ref,
         w0_ref, a0_ref, vf_ref, v1_ref, v2_ref, v0_ref, p_ref, g_ref, xs_ref) = refs
    else:
        (x_ref, halo_ref, sl_ref, ov_ref, mu_ref, wr_ref, wk_ref, wv_ref, w1_ref, w2_ref, a1_ref, a2_ref, g1_ref, g2_ref,
         w0_ref, a0_ref, p_ref, g_ref, xs_ref) = refs
    d = D_MODEL
    x = x_ref[...]
    r0 = pl.program_id(0) * tm
    xs_ref[0:SUBLANES] = halo_ref[...]
    xs_ref[SUBLANES:SUBLANES + tm] = x
    prev = xs_ref[pl.ds(SUBLANES - 1, tm), :]
    rows = r0 + lax.broadcasted_iota(jnp.int32, (tm, 1), 0)
    off, cnt, length = long_seqs
    b = (jnp.maximum(r0 - off, 0) + (length - 1)) // length
    hit = jnp.logical_and(rows == off + b * length, b < cnt)
    prev = jnp.where(hit, sl_ref[pl.ds(jnp.minimum(b, cnt - 1), 1), :], prev)
    off, cnt, length = short_seqs
    rel = rows - off
    first = jnp.logical_and(jnp.logical_and(rel >= 0, rel < cnt * length), jnp.bitwise_and(rel, length - 1) == 0)
    prev = jnp.where(first, ov_ref[...], prev)
    xx = prev - x
    mix = lambda i: (x + xx * mu_ref[i:i + 1, :]).astype(BF16)
    xr, xw, xk, xv, xa, xg = (mix(i) for i in range(6))
    p_ref[:, 0:d] = _dot(xr, wr_ref[...])
    p_ref[:, d:2 * d] = _dot(xk, wk_ref[...])
    v = _dot(xv, wv_ref[...])
    if has_vres:
        v = v + (vf_ref[...] - v) * _sigmoid(v0_ref[...] + _dot(_dot(xv, v1_ref[...]), v2_ref[...]))
    p_ref[:, 2 * d:3 * d] = v
    p_ref[:, 3 * d:4 * d] = w0_ref[...] + _dot(jnp.tanh(_dot(xw, w1_ref[...])), w2_ref[...])
    p_ref[:, 4 * d:5 * d] = a0_ref[...] + _dot(_dot(xa, a1_ref[...]), a2_ref[...])
    g_ref[...] = _dot(_sigmoid(_dot(xg, g1_ref[...])), g2_ref[...])


def _rwkv_in(x, shift_long, shift_short, consts, vres, *, tm, long_seqs, short_seqs):
    n, d = x.shape
    off, cnt, length = short_seqs
    assert length & (length - 1) == 0 and long_seqs[2] >= tm and tm % SUBLANES == 0
    tile0 = off // tm
    tiles = -(-(off + cnt * length) // tm) - tile0
    ov = jnp.pad(shift_short[:, None], ((0, 0), (0, length - 1), (0, 0))).reshape(cnt * length, d)
    ov = jnp.pad(ov, ((off - tile0 * tm, tiles * tm - (off - tile0 * tm) - cnt * length), (0, 0)))
    sl = jnp.pad(shift_long, ((0, -shift_long.shape[0] % SUBLANES), (0, 0)))
    row = lambda w: pl.BlockSpec((tm, w), lambda i: (i, 0))
    halo = pl.BlockSpec((SUBLANES, d), lambda i: (jnp.maximum(i * (tm // SUBLANES) - 1, 0), 0))
    ov_spec = pl.BlockSpec((tm, d), lambda i: (jnp.clip(i - tile0, 0, tiles - 1), 0))
    args = [x, x, sl, ov] + list(consts)
    specs = [row(d), halo, _const_spec(sl.shape), ov_spec] + [_const_spec(c.shape) for c in consts]
    if vres is not None:
        vf, vconsts = vres
        args += [vf] + list(vconsts)
        specs += [row(d)] + [_const_spec(c.shape) for c in vconsts]
    return pl.pallas_call(
        functools.partial(_rwkv_in_kernel, has_vres=vres is not None, tm=tm, long_seqs=long_seqs, short_seqs=short_seqs),
        out_shape=[jax.ShapeDtypeStruct((n, RWKV_STREAMS * d), F32), jax.ShapeDtypeStruct((n, d), F32)],
        grid=(n // tm,),
        in_specs=specs,
        out_specs=[row(RWKV_STREAMS * d), row(d)],
        scratch_shapes=[pltpu.VMEM((tm + SUBLANES, d), F32)],
        compiler_params=_params("parallel"),
        name="rwkv_in",
    )(*args)


WKV_SEQS = LANES // RWKV_HEADS


def _wkv_kernel(*refs, tt):
    tok_refs = refs[:WKV_SEQS]
    (s0_ref, kk_ref, ka_ref, rk_ref, lg_ref, lb_ref, y_ref, sout_ref,
     s_ref, p_ref, dec_ref, a_ref, b_ref, km_ref, bon_ref) = refs[WKV_SEQS:]
    t = pl.program_id(1)
    hk = RWKV_HEAD

    @pl.when(t == 0)
    def _init():
        s_ref[...] = s0_ref[...]

    def relayout(i, carry):
        for s in range(RWKV_STREAMS):
            rows = jnp.concatenate([tok_refs[q][i, s] for q in range(WKV_SEQS)], axis=0)
            p_ref[i, s] = rows.T
        return carry

    lax.fori_loop(0, tt, relayout, 0)

    r = p_ref[:, 0]
    k = p_ref[:, 1]
    v = p_ref[:, 2]
    dec_ref[...] = jnp.exp(-jnp.exp(-_softplus(-p_ref[:, 3]) - 0.5))
    ag = _sigmoid(p_ref[:, 4])
    kk = k * kk_ref[...][None]
    kk = kk / jnp.maximum(jnp.sqrt(jnp.sum(kk * kk, axis=1, keepdims=True)), 1e-12)
    km = k * (1.0 + (ag - 1.0) * ka_ref[...][None])
    km_ref[...] = km
    a_ref[...] = -kk
    b_ref[...] = kk * ag
    bon_ref[...] = jnp.sum(r * km * rk_ref[...][None], axis=1, keepdims=True) * v
    lg = lg_ref[...]
    lb = lb_ref[...]

    def step(i, carry):
        lanes = 4
        parts = [jnp.zeros((hk, LANES), F32) for _ in range(lanes)]
        for q in range(hk):
            parts[q % lanes] = parts[q % lanes] + s_ref[q] * a_ref[i, q:q + 1, :]
        sa = (parts[0] + parts[1]) + (parts[2] + parts[3])
        vv = p_ref[i, 2]
        parts = [jnp.zeros((hk, LANES), F32) for _ in range(lanes)]
        for q in range(hk):
            sn = s_ref[q] * dec_ref[i, q:q + 1, :] + (sa * b_ref[i, q:q + 1, :] + vv * km_ref[i, q:q + 1, :])
            s_ref[q] = sn
            parts[q % lanes] = parts[q % lanes] + sn * p_ref[i, 0, q:q + 1, :]
        y = (parts[0] + parts[1]) + (parts[2] + parts[3])
        mu = jnp.mean(y, axis=0, keepdims=True)
        dlt = y - mu
        var = jnp.mean(dlt * dlt, axis=0, keepdims=True)
        out = dlt * lax.rsqrt(var + RWKV_GN_EPS) * lg + lb + bon_ref[i]
        y_ref[i] = out.T.reshape(WKV_SEQS, RWKV_HEADS, hk)
        return carry

    lax.fori_loop(0, tt, step, 0)

    @pl.when(t == pl.num_programs(1) - 1)
    def _fin():
        sout_ref[...] = s_ref[...]


def _wkv(p, s0, lane_consts, *, off, nb, t, tt):
    hk = RWKV_HEAD
    l = nb * RWKV_HEADS
    assert off % tt == 0 and t % tt == 0 and nb % WKV_SEQS == 0
    p4 = p.reshape(p.shape[0], RWKV_STREAMS, RWKV_HEADS, hk)
    tok = lambda q: pl.BlockSpec((tt, RWKV_STREAMS, RWKV_HEADS, hk),
                                 lambda g, i: ((off + (g * WKV_SEQS + q) * t) // tt + i, 0, 0, 0))
    lane2 = pl.BlockSpec((hk, LANES), lambda g, i: (0, g))
    st = pl.BlockSpec((hk, hk, LANES), lambda g, i: (0, 0, g))
    seq = lambda: pltpu.VMEM((tt, hk, LANES), F32)
    return pl.pallas_call(
        functools.partial(_wkv_kernel, tt=tt),
        out_shape=[jax.ShapeDtypeStruct((t, nb, RWKV_HEADS, hk), F32), jax.ShapeDtypeStruct((hk, hk, l), F32)],
        grid=(l // LANES, t // tt),
        in_specs=[tok(q) for q in range(WKV_SEQS)] + [st] + [lane2] * 5,
        out_specs=[pl.BlockSpec((tt, WKV_SEQS, RWKV_HEADS, hk), lambda g, i: (i, g, 0, 0)), st],
        scratch_shapes=[pltpu.VMEM((hk, hk, LANES), F32), pltpu.VMEM((tt, RWKV_STREAMS, hk, LANES), F32),
                        seq(), seq(), seq(), seq(), seq()],
        compiler_params=_params("parallel", "arbitrary"),
        name="wkv",
    )(*([p4] * WKV_SEQS), s0, *lane_consts)


def _gate_out_kernel(y_ref, g_ref, x_ref, w_ref, lg_ref, lb_ref, o_ref):
    m = _dot(y_ref[...] * g_ref[...], w_ref[...])
    o_ref[...] = _layer_norm_rows(DN_ALPHA * x_ref[...] + m, lg_ref[...], lb_ref[...])


def _gate_out(y, g, x, w, lg, lb, *, tm):
    n, d = x.shape
    row = pl.BlockSpec((tm, d), lambda i: (i, 0))
    consts = [w, lg, lb]
    return pl.pallas_call(
        _gate_out_kernel,
        out_shape=jax.ShapeDtypeStruct((n, d), F32),
        grid=(n // tm,),
        in_specs=[row, row, row] + [_const_spec(c.shape) for c in consts],
        out_specs=row,
        compiler_params=_params("parallel"),
        name="gate_out",
    )(y, g, x, *consts)


def _pad_cols(w):
    return jnp.pad(w, ((0, 0), (0, LANES - w.shape[1]))).astype(BF16)


def _pad_rows(w):
    return jnp.pad(w, ((0, LANES - w.shape[0]), (0, 0))).astype(BF16)


def _rwkv_weights(mu, w_r, w_k, w_v, w_o, w0, w1, w2, a0, a1, a2, g1, g2, k_k, k_a, r_k, lnx_g, lnx_b, vres):
    consts = [jnp.pad(mu, ((0, SUBLANES - mu.shape[0]), (0, 0))), w_r.astype(BF16), w_k.astype(BF16), w_v.astype(BF16),
              _pad_cols(w1), _pad_rows(w2), _pad_cols(a1), _pad_rows(a2), _pad_cols(g1), _pad_rows(g2), w0[None], a0[None]]
    vconsts = None if vres is None else [_pad_cols(vres[1]), _pad_rows(vres[2]), vres[0][None]]
    per_head = [k_k.reshape(RWKV_HEADS, RWKV_HEAD).T, k_a.reshape(RWKV_HEADS, RWKV_HEAD).T, r_k.T,
                lnx_g.reshape(RWKV_HEADS, RWKV_HEAD).T, lnx_b.reshape(RWKV_HEADS, RWKV_HEAD).T]
    return consts, vconsts, per_head, w_o.astype(BF16)


def _rwkv_layer(x, groups, weights, v_first, g, beta, *, tm):
    consts, vconsts, per_head, w_o = weights
    n, d = x.shape
    (off_l, nb_l, t_l, _, _, shift_l), (off_s, nb_s, t_s, _, _, shift_s) = groups
    n_used = nb_l * t_l + nb_s * t_s
    vres = None if vconsts is None else (v_first, vconsts)
    p, gate = _rwkv_in(x, shift_l, shift_s, consts, vres, tm=tm,
                       long_seqs=(off_l, nb_l, t_l), short_seqs=(off_s, nb_s, t_s))
    ys, states = [], []
    for off, nb, t, tt, s, shift in groups:
        lanes = nb * RWKV_HEADS
        hk2 = RWKV_HEAD * RWKV_HEAD
        s0 = jnp.swapaxes(s, 2, 3).reshape(lanes, hk2).T.reshape(RWKV_HEAD, RWKV_HEAD, lanes)
        lane_consts = [jnp.tile(c, (1, nb)) for c in per_head]
        y, s_out = _wkv(p, s0, lane_consts, off=off, nb=nb, t=t, tt=tt)
        ys.append(jnp.swapaxes(y.reshape(t, nb, d), 0, 1).reshape(nb * t, d))
        s_new = jnp.swapaxes(s_out.reshape(hk2, lanes).T.reshape(nb, RWKV_HEADS, RWKV_HEAD, RWKV_HEAD), 2, 3)
        states.append((s_new, x[off:off + nb * t].reshape(nb, t, d)[:, -1]))
    y = jnp.concatenate(ys + [jnp.zeros((n - n_used, d), F32)], axis=0)
    return _gate_out(y, gate, x, w_o, g, beta, tm=tm), states, p[:, 2 * d:3 * d]


TOKEN_TILE = 512
PEER_TOKEN_TILE = 1024
PROMPT_TIME_BLOCK = 688
WKV_TIME_BLOCK = 16


def kernel(x_prompt, x_sample, state_gla, state_lru_h, state_lru_conv, state_rwkv, state_rwkv_shift, meta_tokens, ln_g, ln_b, ev_w_in, ev_gla_w_gate, ev_gla_b_gate, ev_gla_norm, ev_conv_w, ev_conv_b, ev_lru_wa, ev_lru_ba, ev_lru_wx, ev_lru_bx, ev_lru_lambda, ev_w_out, od_mu, od_w_r, od_w_k, od_w_v, od_w_o, od_w0, od_w1, od_w2, od_a0, od_a1, od_a2, od_v0, od_v1, od_v2, od_g1, od_g2, od_k_k, od_k_a, od_r_k, od_lnx_g, od_lnx_b, peer_w_q, peer_keys, peer_u, peer_v):
    bp, sp, d = x_prompt.shape
    bs, ss, _ = x_sample.shape
    tp = sp + N_META
    n_p, n_s = bp * tp, bs * ss
    n = -(-(n_p + n_s) // PEER_TOKEN_TILE) * PEER_TOKEN_TILE
    assert tp % PROMPT_TIME_BLOCK == 0 and tp % GLA_TIME_BLOCK == 0 and tp % WKV_TIME_BLOCK == 0 and n % TOKEN_TILE == 0

    xp = jnp.concatenate([jnp.broadcast_to(meta_tokens[None], (bp, N_META, d)), x_prompt], axis=1)
    x = jnp.concatenate([xp.reshape(n_p, d), x_sample.reshape(n_s, d), jnp.zeros((n - n_p - n_s, d), F32)], axis=0)

    n_pairs = DEPTH // 2
    zeros = lambda *s: jnp.zeros(s, F32)
    outs = {k: [] for k in ("p_gla", "p_h", "p_conv", "p_rwkv", "p_shift", "s_gla", "s_h", "s_conv", "s_rwkv", "s_shift")}
    v_first = None
    for layer in range(DEPTH):
        j = layer // 2
        g, beta = ln_g[layer, 0][None], ln_b[layer, 0][None]
        if layer % 2 == 0:
            weights = _even_weights(ev_w_in[j], ev_gla_w_gate[j], ev_gla_b_gate[j], ev_gla_norm[j], ev_conv_w[j], ev_conv_b[j],
                                    ev_lru_wa[j], ev_lru_ba[j], ev_lru_wx[j], ev_lru_bx[j], ev_lru_lambda[j], ev_w_out[j])
            groups = [(0, bp, tp, PROMPT_TIME_BLOCK, zeros(bp, GLA_HEADS, GLA_DK, GLA_DV), zeros(bp, LRU_WIDTH),
                       zeros(bp, CONV_W - 1, LRU_WIDTH)),
                      (n_p, bs, ss, None, state_gla[j], state_lru_h[j], state_lru_conv[j])]
            x, states = _even_layer(x, groups, weights, g, beta, tm=TOKEN_TILE)
            for pre, st in zip("ps", states):
                outs[pre + "_gla"].append(st[0])
                outs[pre + "_h"].append(st[1])
                outs[pre + "_conv"].append(st[2])
        else:
            vres = None if j == 0 else (od_v0[j - 1], od_v1[j - 1], od_v2[j - 1])
            weights = _rwkv_weights(od_mu[j], od_w_r[j], od_w_k[j], od_w_v[j], od_w_o[j], od_w0[j], od_w1[j], od_w2[j],
                                    od_a0[j], od_a1[j], od_a2[j], od_g1[j], od_g2[j], od_k_k[j], od_k_a[j], od_r_k[j],
                                    od_lnx_g[j], od_lnx_b[j], vres)
            groups = [(0, bp, tp, WKV_TIME_BLOCK, zeros(bp, RWKV_HEADS, RWKV_HEAD, RWKV_HEAD), zeros(bp, d)),
                      (n_p, bs, ss, ss, state_rwkv[j], state_rwkv_shift[j])]
            x, states, v = _rwkv_layer(x, groups, weights, v_first, g, beta, tm=TOKEN_TILE)
            if v_first is None:
                v_first = v
            for pre, st in zip("ps", states):
                outs[pre + "_rwkv"].append(st[0])
                outs[pre + "_shift"].append(st[1])
        x = _peer_layer(x, peer_w_q[layer].T.astype(BF16), peer_keys[layer, :, 0].astype(BF16),
                        peer_keys[layer, :, 1].astype(BF16), peer_u[layer].astype(BF16), peer_v[layer].astype(BF16),
                        ln_g[layer, 1][None], ln_b[layer, 1][None], tn=PEER_TOKEN_TILE)

    y_prompt = x[:n_p].reshape(bp, tp, d)[:, N_META:]
    y_sample = x[n_p:n_p + n_s].reshape(bs, ss, d)
    st = {k: jnp.stack(v) for k, v in outs.items()}
    return (y_prompt, y_sample, st["p_gla"], st["p_h"], st["p_conv"], st["p_rwkv"], st["p_shift"],
            st["s_gla"], st["s_h"], st["s_conv"], st["s_rwkv"], st["s_shift"])
```

```python
import functools

import jax
import jax.numpy as jnp
from jax import lax
from jax.experimental import pallas as pl
from jax.experimental.pallas import tpu as pltpu

F32 = jnp.float32
BF16 = jnp.bfloat16

D_MODEL = 1024
DEPTH = 4
N_META = 16
GLA_HEADS = 4
GLA_DK = 64
GLA_DV = 128
GLA_RANK = 16
GLA_GATE_NORM = 16.0
GLA_CHUNK = 16
LRU_WIDTH = 512
LRU_BLOCKS = 8
CONV_W = 4
LRU_C = 8.0
RWKV_HEAD = 64
RWKV_HEADS = D_MODEL // RWKV_HEAD
RWKV_GN_EPS = 64e-5
PEER_HEADS = 8
PEER_NKEYS = 128
PEER_DKEY = 256
PEER_HALF = PEER_DKEY // 2
PEER_TOPK = 16
DN_ALPHA = float((2 * DEPTH) ** 0.25)
LN_EPS = 1e-5

LANES = 128
SUBLANES = 8
VMEM_LIMIT = 56 * 1024 * 1024


def _dot(a, b):
    return jnp.dot(a.astype(BF16), b.astype(BF16), preferred_element_type=F32)


def _dot_nt(a, b):
    return lax.dot_general(a.astype(BF16), b.astype(BF16), (((1,), (1,)), ((), ())), preferred_element_type=F32)


def _dot_tn(a, b):
    return lax.dot_general(a.astype(BF16), b.astype(BF16), (((0,), (0,)), ((), ())), preferred_element_type=F32)


def _layer_norm_rows(z, g, b):
    mu = jnp.mean(z, axis=-1, keepdims=True)
    d = z - mu
    var = jnp.mean(d * d, axis=-1, keepdims=True)
    return d * lax.rsqrt(var + LN_EPS) * g + b


def _sigmoid(x):
    return 1.0 / (1.0 + jnp.exp(-x))


def _softplus(x):
    return jnp.maximum(x, 0.0) + jnp.log1p(jnp.exp(-jnp.abs(x)))


def _gelu_tanh(x):
    c0 = 0.7978845608028654
    c1 = c0 * 0.044715
    one = jnp.asarray(1.0, x.dtype)
    return (0.5 * x) * (one + jnp.tanh(x * (c0 + c1 * (x * x))))


def _silu(x):
    return x * _sigmoid(x)


def _params(*sem):
    return pltpu.CompilerParams(dimension_semantics=sem, vmem_limit_bytes=VMEM_LIMIT)


def _const_spec(shape):
    nd = len(shape)
    return pl.BlockSpec(shape, lambda *_: (0,) * nd)


PEER_CAND_ROWS = 80
PEER_PACK = 2 * SUBLANES
PEER_SELECT_LANES = 2 * LANES


def _peer_cand_index():
    rows = [r1 * PEER_TOPK for r1 in range(16)]
    for j in range(1, 8):
        rows += [r1 * PEER_TOPK + j for r1 in range(8)]
    rows += list(range(8, 16))
    return jnp.broadcast_to(jnp.asarray(rows, F32)[:, None], (PEER_CAND_ROWS, PEER_SELECT_LANES))


PEER_CODE_STEP = 2.0 ** 116
PEER_CODED_BELOW = -(2.0 ** 119)


def _rank_code(r):
    return -PEER_CODE_STEP * (PEER_TOPK + r)


def _decode_rank(s):
    return jnp.where(s < PEER_CODED_BELOW, s * (-1.0 / PEER_CODE_STEP) - PEER_TOPK, float(PEER_TOPK))


def _count_coded(s):
    return jnp.sum(jnp.where(s < PEER_CODED_BELOW, 1.0, 0.0), axis=0, keepdims=True)


def _top16(scores, key_iota):
    row16 = lax.broadcasted_iota(jnp.int32, (PEER_TOPK, scores[0].shape[1]), 0)
    vals = [jnp.zeros((PEER_TOPK, s.shape[1]), F32) for s in scores]
    scores = list(scores)
    for r in range(PEER_TOPK):
        tops = [jnp.max(s, axis=0, keepdims=True) for s in scores]
        hits = [s == m for s, m in zip(scores, tops)]
        if key_iota is not None:
            firsts = [jnp.min(jnp.where(hit, key_iota, float(PEER_NKEYS)), axis=0, keepdims=True) for hit in hits]
            hits = [key_iota == first for first in firsts]
        scores = [jnp.where(hit, _rank_code(r), s) for hit, s in zip(hits, scores)]
        vals = [jnp.where(row16 == r, m, v) for m, v in zip(tops, vals)]
    return list(zip(vals, scores))


def _peer_candidates(a, b):
    blocks = [a + b[0:1]]
    for j in range(1, 8):
        blocks.append(a[0:8] + b[j:j + 1])
    blocks.append(a[0:1] + b[8:16])
    return jnp.concatenate(blocks, axis=0)


def _select16(c, cidx):
    for _ in range(PEER_TOPK):
        m = jnp.max(c, axis=0, keepdims=True)
        hit = c == m
        if cidx is not None:
            first = jnp.min(jnp.where(hit, cidx, 1e9), axis=0, keepdims=True)
            hit = cidx == first
        c = jnp.where(hit, _rank_code(0), c)
    return c


def _peer_gate_tiles(s1, s2, a, b, s1c, s2c, c, cc):
    sel = jnp.where(cc < PEER_CODED_BELOW, 1.0, 0.0)
    z = jnp.sum(sel * jnp.exp(c - c[0:1]), axis=0, keepdims=True)
    j_lo = sel[0:8]
    for j in range(1, 8):
        j_lo = j_lo + sel[8 + 8 * j:16 + 8 * j]
    extra = jnp.sum(sel[72:80], axis=0, keepdims=True)
    row8 = lax.broadcasted_iota(jnp.int32, j_lo.shape, 0)
    j_lo = j_lo + jnp.where(row8 == 0, extra, 0.0)
    jt = jnp.concatenate([j_lo, sel[8:16]], axis=0)
    r1 = _decode_rank(s1c)
    h1 = jnp.zeros_like(r1)
    for r in range(PEER_TOPK):
        h1 = jnp.where(r1 == float(r), jt[r:r + 1], h1)
    return h1, jnp.exp(s1 - a[0:1]) / z, _decode_rank(s2c), jnp.exp(s2 - b[0:1])


def _peer_kernel(x_ref, wqt_ref, k1_ref, k2_ref, cidx_ref, u_ref, vt_ref, g_ref, b_ref, o_ref,
                 xt_ref, acc_ref, h1_ref, c1_ref, r2_ref, e2_ref, s_ref, hd0_ref, hd1_ref, coef_ref, *, tn, te):
    j = pl.program_id(1)
    nj = pl.num_programs(1)

    @pl.when(j == 0)
    def _select():
        xt_ref[...] = x_ref[...].T.astype(BF16)
        acc_ref[...] = jnp.zeros_like(acc_ref)
        hd1_ref[...] = jnp.zeros_like(hd1_ref)
        coef_ref[0:te] = jnp.zeros((te, tn), BF16)
        key_iota = lax.broadcasted_iota(jnp.int32, (PEER_NKEYS, PEER_SELECT_LANES), 0).astype(F32)

        def head(h, carry):
            q = _dot(wqt_ref[pl.ds(pl.multiple_of(h * PEER_DKEY, PEER_DKEY), PEER_DKEY), :], xt_ref[...])
            mu = jnp.mean(q, axis=0, keepdims=True)
            d = q - mu
            qn = d * lax.rsqrt(jnp.mean(d * d, axis=0, keepdims=True) + LN_EPS)
            s_ref[0] = _dot(k1_ref[h], qn[0:PEER_HALF])
            s_ref[1] = _dot(k2_ref[h], qn[PEER_HALF:PEER_DKEY])

            def chunk(ci, carry2):
                ls = pl.ds(pl.multiple_of(ci * PEER_SELECT_LANES, PEER_SELECT_LANES), PEER_SELECT_LANES)
                s1 = s_ref[0, :, ls]
                s2 = s_ref[1, :, ls]

                def put(tiles):
                    h1_ref[h, :, ls] = tiles[0]
                    c1_ref[h, :, ls] = tiles[1]
                    r2_ref[h, :, ls] = tiles[2].astype(BF16)
                    e2_ref[h, :, ls] = tiles[3].astype(BF16)

                (a, s1c), (b, s2c) = _top16([s1, s2], None)
                c = _peer_candidates(a, b)
                cc = _select16(c, None)
                put(_peer_gate_tiles(s1, s2, a, b, s1c, s2c, c, cc))
                miscount = (jnp.abs(_count_coded(s1c) - PEER_TOPK) + jnp.abs(_count_coded(s2c) - PEER_TOPK)
                            + jnp.abs(_count_coded(cc) - PEER_TOPK))

                @pl.when(jnp.max(miscount) > 0.0)
                def _with_ties():
                    (a, s1c), (b, s2c) = _top16([s1, s2], key_iota)
                    c = _peer_candidates(a, b)
                    cc = _select16(c, cidx_ref[...])
                    put(_peer_gate_tiles(s1, s2, a, b, s1c, s2c, c, cc))

                return carry2

            lax.fori_loop(0, tn // PEER_SELECT_LANES, chunk, 0)
            return carry

        lax.fori_loop(0, PEER_HEADS, head, 0)

    groups = PEER_NKEYS // PEER_PACK
    last_key = PEER_NKEYS - 1
    zero_pack = jnp.zeros((PEER_PACK, LANES), F32)

    def gate_rows(tile):
        rows = []
        for ii in range(te // PEER_NKEYS):
            i1 = jnp.clip(tile * (te // PEER_NKEYS) + ii, 0, last_key)
            rows.append(([h1_ref[h, pl.ds(i1, 1), :] for h in range(PEER_HEADS)],
                         [c1_ref[h, pl.ds(i1, 1), :] for h in range(PEER_HEADS)]))
        return rows

    def gate_stage(rows, hd_ref, coef_rows, lane_groups):
        nk = len(rows)
        for c in lane_groups:
            ls = slice(c * LANES, (c + 1) * LANES)
            gates = [[jnp.zeros((PEER_PACK, LANES), BF16) for _ in range(groups)] for _ in range(nk)]
            for h in range(PEER_HEADS):
                bounds = [(rows[ii][0][h][:, ls] + zero_pack).astype(BF16) for ii in range(nk)]
                scales = [(rows[ii][1][h][:, ls] + zero_pack).astype(BF16) for ii in range(nk)]
                for gi in range(groups):
                    krows = slice(gi * PEER_PACK, (gi + 1) * PEER_PACK)
                    rank = r2_ref[h, krows, ls]
                    fac = e2_ref[h, krows, ls]
                    for ii in range(nk):
                        gates[ii][gi] = gates[ii][gi] + jnp.where(rank < bounds[ii], fac * scales[ii], jnp.zeros((), BF16))
            for ii in range(nk):
                for gi in range(groups):
                    erows = slice(ii * PEER_NKEYS + gi * PEER_PACK, ii * PEER_NKEYS + (gi + 1) * PEER_PACK)
                    crows = slice(coef_rows + erows.start, coef_rows + erows.stop)
                    coef_ref[crows, ls] = gates[ii][gi] * _gelu_tanh(hd_ref[erows, ls].astype(BF16))

    piece = 2 * LANES
    per = piece // LANES
    rows_odd = gate_rows(2 * j - 1)
    rows_even = gate_rows(2 * j)
    pieces = tn // piece
    lanes_of = lambda c: slice(c * piece, (c + 1) * piece)

    def down_project(c):
        acc_ref[:, lanes_of(c)] += jnp.dot(vt_ref[0], coef_ref[:, lanes_of(c)], preferred_element_type=F32)

    for c in range(pieces):
        gate_stage(rows_odd, hd1_ref, te, range(c * per, (c + 1) * per))
        hd0_ref[:, lanes_of(c)] = _dot(u_ref[0:te, :], xt_ref[:, lanes_of(c)])
        hd1_ref[:, lanes_of(c)] = _dot(u_ref[te:2 * te, :], xt_ref[:, lanes_of(c)])
        if c > 0:
            down_project(c - 1)
    down_project(pieces - 1)
    for c in range(pieces):
        gate_stage(rows_even, hd0_ref, 0, range(c * per, (c + 1) * per))

    @pl.when(j == nj - 1)
    def _finish():
        y = acc_ref[...].T
        o_ref[...] = _layer_norm_rows(DN_ALPHA * x_ref[...] + y, g_ref[...], b_ref[...])


def _peer_layer(x, wqt, k1, k2, u, v, g, b, *, tn=1024, te=256):
    n, d = x.shape
    ne = u.shape[0]
    assert ne % (2 * te) == 0 and n % tn == 0
    pairs = ne // (2 * te)
    steps = pairs + 1
    vt = jnp.swapaxes(v.reshape(pairs, 2 * te, d), 1, 2)
    kern = functools.partial(_peer_kernel, tn=tn, te=te)
    one = pl.Buffered(1)
    return pl.pallas_call(
        kern,
        out_shape=jax.ShapeDtypeStruct((n, d), F32),
        grid=(n // tn, steps),
        in_specs=[
            pl.BlockSpec((tn, d), lambda i, j: (i, 0), pipeline_mode=one),
            pl.BlockSpec(wqt.shape, lambda i, j: (0, 0), pipeline_mode=one),
            pl.BlockSpec(k1.shape, lambda i, j: (0, 0, 0), pipeline_mode=one),
            pl.BlockSpec(k2.shape, lambda i, j: (0, 0, 0), pipeline_mode=one),
            pl.BlockSpec((PEER_CAND_ROWS, PEER_SELECT_LANES), lambda i, j: (0, 0), pipeline_mode=one),
            pl.BlockSpec((2 * te, d), lambda i, j: (jnp.minimum(j, pairs - 1), 0)),
            pl.BlockSpec((1, d, 2 * te), lambda i, j: (jnp.maximum(j - 1, 0), 0, 0)),
            pl.BlockSpec((1, d), lambda i, j: (0, 0), pipeline_mode=one),
            pl.BlockSpec((1, d), lambda i, j: (0, 0), pipeline_mode=one),
        ],
        out_specs=pl.BlockSpec((tn, d), lambda i, j: (i, 0), pipeline_mode=one),
        scratch_shapes=[
            pltpu.VMEM((d, tn), BF16),
            pltpu.VMEM((d, tn), F32),
            pltpu.VMEM((PEER_HEADS, PEER_NKEYS, tn), F32),
            pltpu.VMEM((PEER_HEADS, PEER_NKEYS, tn), F32),
            pltpu.VMEM((PEER_HEADS, PEER_NKEYS, tn), BF16),
            pltpu.VMEM((PEER_HEADS, PEER_NKEYS, tn), BF16),
            pltpu.VMEM((2, PEER_NKEYS, tn), F32),
            pltpu.VMEM((te, tn), F32),
            pltpu.VMEM((te, tn), F32),
            pltpu.VMEM((2 * te, tn), BF16),
        ],
        compiler_params=_params("parallel", "arbitrary"),
        name="peer",
    )(x, wqt, k1, k2, _peer_cand_index(), u, vt, g, b)


GLA_PAD = GLA_HEADS * LANES


def _even_in_kernel(x_ref, wq_ref, wk_ref, wv_ref, wr_ref, wxb_ref, wgb_ref, wlr_ref, wg_ref, bg_ref,
                    qkl_ref, v_ref, rs_ref, xb_ref, gg_ref):
    xb16 = x_ref[...].astype(BF16)
    qkl_ref[:, 0:GLA_PAD] = _dot(xb16, wq_ref[...]) * (GLA_DK ** -0.5)
    qkl_ref[:, GLA_PAD:2 * GLA_PAD] = _dot(xb16, wk_ref[...])
    glr = _dot(xb16, wlr_ref[...])
    z = _dot(glr, wg_ref[...]) + bg_ref[...]
    qkl_ref[:, 2 * GLA_PAD:3 * GLA_PAD] = -_softplus(-z) * (1.0 / GLA_GATE_NORM)
    v_ref[...] = _dot(xb16, wv_ref[...])
    rs_ref[...] = _silu(_dot(xb16, wr_ref[...]))
    xb_ref[...] = _dot(xb16, wxb_ref[...])
    gg_ref[...] = _gelu_tanh(_dot(xb16, wgb_ref[...]))


def _even_in(x, consts, *, tm=512):
    n, d = x.shape
    row = lambda w: pl.BlockSpec((tm, w), lambda i: (i, 0))
    return pl.pallas_call(
        _even_in_kernel,
        out_shape=[jax.ShapeDtypeStruct((n, 3 * GLA_PAD), F32)] + [jax.ShapeDtypeStruct((n, LRU_WIDTH), F32)] * 4,
        grid=(n // tm,),
        in_specs=[row(d)] + [_const_spec(c.shape) for c in consts],
        out_specs=[row(3 * GLA_PAD)] + [row(LRU_WIDTH)] * 4,
        compiler_params=_params("parallel"),
        name="even_in",
    )(x, *consts)


GLA_SEQS_PER_STEP = 8
GLA_TIME_BLOCK = 3 * GLA_CHUNK


def _gla_kernel(*refs, tb, ns):
    qkl_refs, v_refs, rs_refs, s0_refs = (refs[i * ns:(i + 1) * ns] for i in range(4))
    tri_ref, gn_ref = refs[4 * ns:4 * ns + 2]
    o_refs = refs[4 * ns + 2:5 * ns + 2]
    sout_refs = refs[5 * ns + 2:6 * ns + 2]
    st_ref = refs[6 * ns + 2]
    t = pl.program_id(1)

    @pl.when(t == 0)
    def _init():
        for p in range(ns):
            st_ref[p] = s0_refs[p][0]

    tri = tri_ref[...]
    causal = tri > 0
    gn = gn_ref[...]

    def chunk(c, carry):
        rows = pl.ds(pl.multiple_of(c * GLA_CHUNK, GLA_CHUNK), GLA_CHUNK)
        seqs = range(ns)
        pairs = [(p, h) for p in seqs for h in range(GLA_HEADS)]
        sl = lambda h: slice(h * LANES, (h + 1) * LANES)
        las = [qkl_refs[p][rows, 2 * GLA_PAD:3 * GLA_PAD] for p in seqs]
        his = [la.astype(BF16) for la in las]
        r1s = [la - hi.astype(F32) for la, hi in zip(las, his)]
        mids = [r1.astype(BF16) for r1 in r1s]
        los = [(r1 - mid.astype(F32)).astype(BF16) for r1, mid in zip(r1s, mids)]
        parts = [[jnp.dot(tri, part, preferred_element_type=F32) for part in (hi, mid, lo)]
                 for hi, mid, lo in zip(his, mids, los)]
        cums = [a + b + c3 for a, b, c3 in parts]
        lasts = [cum[GLA_CHUNK - 1:GLA_CHUNK] for cum in cums]
        qes = [qkl_refs[p][rows, 0:GLA_PAD] * jnp.exp(cums[p]) for p in seqs]
        ks = [qkl_refs[p][rows, GLA_PAD:2 * GLA_PAD] for p in seqs]
        kes = [ks[p] * jnp.exp(-cums[p]) for p in seqs]
        kls = [ks[p] * jnp.exp(lasts[p] - cums[p]) for p in seqs]
        decs = [jnp.exp(last) for last in lasts]
        vs = [v_refs[p][rows, :] for p in seqs]
        sts = [st_ref[p, h] for p, h in pairs]
        atts = [_dot_nt(qes[p][:, sl(h)], kes[p][:, sl(h)]) for p, h in pairs]
        inters = [_dot_nt(qes[p][:, sl(h)], st) for (p, h), st in zip(pairs, sts)]
        upds = [_dot_tn(vs[p][:, sl(h)], kls[p][:, sl(h)]) for p, h in pairs]
        intras = [_dot(jnp.where(causal, att, 0.0), vs[p][:, sl(h)]) for (p, h), att in zip(pairs, atts)]
        for i, (p, h) in enumerate(pairs):
            st_ref[p, h] = sts[i] * decs[p][:, sl(h)] + upds[i]
            o = intras[i] + inters[i]
            on = o * lax.rsqrt(jnp.mean(o * o, axis=-1, keepdims=True) + LN_EPS) * gn
            o_refs[p][rows, sl(h)] = on * rs_refs[p][rows, sl(h)]
        return carry

    lax.fori_loop(0, tb // GLA_CHUNK, chunk, 0)

    @pl.when(t == pl.num_programs(1) - 1)
    def _fin():
        for p in range(ns):
            sout_refs[p][0] = st_ref[p]


def _gla(qkl, v, rs, s0t, gn, *, nb, t, tb):
    nt = t // tb
    ns = GLA_SEQS_PER_STEP
    assert nb % ns == 0
    per = nb // ns
    w = GLA_HEADS * GLA_DV
    tri = jnp.tril(jnp.ones((GLA_CHUNK, GLA_CHUNK), BF16))
    row_in = lambda width, p: pl.BlockSpec((tb, width), lambda g, i: ((p * per + g) * nt + i, 0))
    st_in = lambda p: pl.BlockSpec((1, GLA_HEADS, GLA_DV, LANES), lambda g, i: (p * per + g, 0, 0, 0))
    row_out = pl.BlockSpec((tb, w), lambda g, i: (g * nt + i, 0))
    st_out = pl.BlockSpec((1, GLA_HEADS, GLA_DV, LANES), lambda g, i: (g, 0, 0, 0))
    slots = range(ns)
    outs = pl.pallas_call(
        functools.partial(_gla_kernel, tb=tb, ns=ns),
        out_shape=[jax.ShapeDtypeStruct((per * t, w), F32)] * ns
        + [jax.ShapeDtypeStruct((per,) + s0t.shape[1:], F32)] * ns,
        grid=(per, nt),
        in_specs=[row_in(3 * GLA_PAD, p) for p in slots] + [row_in(w, p) for p in slots] + [row_in(w, p) for p in slots]
        + [st_in(p) for p in slots] + [_const_spec(tri.shape), _const_spec(gn.shape)],
        out_specs=[row_out] * ns + [st_out] * ns,
        scratch_shapes=[pltpu.VMEM((ns, GLA_HEADS, GLA_DV, LANES), F32)],
        compiler_params=_params("parallel", "arbitrary"),
        name="gla",
    )(*([qkl] * ns + [v] * ns + [rs] * ns + [s0t] * ns + [tri, gn]))
    return jnp.concatenate(outs[:ns], axis=0), jnp.concatenate(outs[ns:], axis=0)


def _lru_kernel(xb_ref, gg_ref, buf0_ref, h0_ref, cw_ref, cb_ref, wa_ref, ba_ref, wx_ref, bx_ref, lam_ref,
                y_ref, hlast_ref, cbuf_ref, xs_ref, a_ref, b_ref, hs_ref, hcar_ref, *, tb, t_last):
    t = pl.program_id(1)
    halo = SUBLANES

    @pl.when(t == 0)
    def _init():
        xs_ref[0:halo] = buf0_ref[0]
        hcar_ref[...] = h0_ref[0]

    xs_ref[halo:halo + tb] = xb_ref[...]
    xc = cb_ref[...] + sum(xs_ref[pl.ds(halo - (CONV_W - 1) + i, tb), :] * cw_ref[i:i + 1, :] for i in range(CONV_W))
    ga = _sigmoid(_dot(xc, wa_ref[...]) + ba_ref[...])
    gx = _sigmoid(_dot(xc, wx_ref[...]) + bx_ref[...])
    log_at = ga * (-LRU_C * _softplus(-lam_ref[...]))
    a = jnp.exp(log_at)
    a_ref[...] = a
    b_ref[...] = jnp.sqrt(-jnp.tanh(log_at) * (a * a + 1.0)) * (gx * xc)

    def step(i, h):
        h = a_ref[pl.ds(i, 1), :] * h + b_ref[pl.ds(i, 1), :]
        hs_ref[pl.ds(i, 1), :] = h
        return h

    hcar_ref[...] = lax.fori_loop(0, tb, step, hcar_ref[...], unroll=8)
    y_ref[...] = hs_ref[...] * gg_ref[...]

    @pl.when(t == pl.num_programs(1) - 1)
    def _fin():
        hlast_ref[0] = hs_ref[t_last:t_last + 1, :]
        cbuf_ref[0] = xs_ref[halo + t_last - (CONV_W - 2):halo + t_last + 1, :]

    xs_ref[0:halo] = xs_ref[tb:tb + halo]


def _lru(xb, gg, buf0, h0, consts, *, nb, t, tb, t_last):
    nt = t // tb
    w = LRU_WIDTH
    row = pl.BlockSpec((tb, w), lambda b, i: (b * nt + i, 0))
    per_b = lambda r: pl.BlockSpec((1, r, w), lambda b, i: (b, 0, 0))
    return pl.pallas_call(
        functools.partial(_lru_kernel, tb=tb, t_last=t_last),
        out_shape=[jax.ShapeDtypeStruct((nb * t, w), F32), jax.ShapeDtypeStruct((nb, 1, w), F32),
                   jax.ShapeDtypeStruct((nb, CONV_W - 1, w), F32)],
        grid=(nb, nt),
        in_specs=[row, row, per_b(SUBLANES), per_b(1)] + [_const_spec(c.shape) for c in consts],
        out_specs=[row, per_b(1), per_b(CONV_W - 1)],
        scratch_shapes=[pltpu.VMEM((tb + 2 * SUBLANES, w), F32), pltpu.VMEM((tb, w), F32), pltpu.VMEM((tb, w), F32),
                        pltpu.VMEM((tb, w), F32), pltpu.VMEM((1, w), F32)],
        compiler_params=_params("parallel", "arbitrary"),
        name="lru",
    )(xb, gg, buf0, h0, *consts)


def _mix_out_kernel(a_ref, b_ref, x_ref, wa_ref, wb_ref, g_ref, beta_ref, o_ref):
    m = _dot(a_ref[...], wa_ref[...]) + _dot(b_ref[...], wb_ref[...])
    o_ref[...] = _layer_norm_rows(DN_ALPHA * x_ref[...] + m, g_ref[...], beta_ref[...])


def _mix_out(a, b, x, wa, wb, g, beta, *, tm=512):
    n, d = x.shape
    row = lambda w: pl.BlockSpec((tm, w), lambda i: (i, 0))
    consts = [wa, wb, g, beta]
    return pl.pallas_call(
        _mix_out_kernel,
        out_shape=jax.ShapeDtypeStruct((n, d), F32),
        grid=(n // tm,),
        in_specs=[row(a.shape[1]), row(b.shape[1]), row(d)] + [_const_spec(c.shape) for c in consts],
        out_specs=row(d),
        compiler_params=_params("parallel"),
        name="mix_out",
    )(a, b, x, *consts)


def _pad_heads(w, heads, width):
    lead = w.shape[:-1]
    w = w.reshape(lead + (heads, width))
    w = jnp.pad(w, [(0, 0)] * len(lead) + [(0, 0), (0, LANES - width)])
    return w.reshape(lead + (heads * LANES,))


def _even_weights(w_in, w_gate, b_gate, gla_norm, conv_w, conv_b, lru_wa, lru_ba, lru_wx, lru_bx, lru_lambda, w_out):
    gk = GLA_HEADS * GLA_DK
    gv = GLA_HEADS * GLA_DV
    o = 0
    cols = {}
    for name, width in (("q", gk), ("k", gk), ("v", gv), ("lr", GLA_RANK), ("r", gv), ("xb", LRU_WIDTH), ("gb", LRU_WIDTH)):
        cols[name] = w_in[:, o:o + width]
        o += width
    wlr = jnp.pad(cols["lr"], ((0, 0), (0, LANES - GLA_RANK)))
    wg = jnp.pad(_pad_heads(w_gate, GLA_HEADS, GLA_DK), ((0, LANES - GLA_RANK), (0, 0)))
    bg = _pad_heads(b_gate[None], GLA_HEADS, GLA_DK)
    in_consts = [_pad_heads(cols["q"], GLA_HEADS, GLA_DK).astype(BF16), _pad_heads(cols["k"], GLA_HEADS, GLA_DK).astype(BF16),
                 cols["v"].astype(BF16), cols["r"].astype(BF16), cols["xb"].astype(BF16), cols["gb"].astype(BF16),
                 wlr.astype(BF16), wg.astype(BF16), bg]
    eye = jnp.eye(LRU_BLOCKS, dtype=F32)
    bd = lambda w: (eye[:, None, :, None] * w[:, :, None, :]).reshape(LRU_WIDTH, LRU_WIDTH).astype(BF16)
    lru_consts = [conv_w, conv_b[None], bd(lru_wa), lru_ba[None], bd(lru_wx), lru_bx[None], lru_lambda[None]]
    return in_consts, gla_norm[None], lru_consts, w_out[:gv].astype(BF16), w_out[gv:].astype(BF16)


def _seq_rows(a, off, nb, t, t_pad):
    a = a[off:off + nb * t].reshape(nb, t, a.shape[1])
    return jnp.pad(a, ((0, 0), (0, t_pad - t), (0, 0))).reshape(nb * t_pad, a.shape[2])


def _unpad_rows(a, nb, t, t_pad):
    return a.reshape(nb, t_pad, a.shape[1])[:, :t].reshape(nb * t, a.shape[1])


def _even_layer(x, groups, weights, g, beta, *, tm):
    in_consts, gn, lru_consts, wo_a, wo_b = weights
    qkl, v, rs, xb, gg = _even_in(x, in_consts, tm=tm)
    ogs, yls, states = [], [], []
    for off, nb, t, tb, s_gla, h_lru, cbuf in groups:
        s0t = jnp.pad(jnp.swapaxes(s_gla, 2, 3), ((0, 0), (0, 0), (0, 0), (0, LANES - GLA_DK)))
        buf0 = jnp.pad(cbuf, ((0, 0), (SUBLANES - (CONV_W - 1), 0), (0, 0)))
        if tb is None:
            tg = GLA_CHUNK
            tl = SUBLANES
            og, st = _gla(_seq_rows(qkl, off, nb, t, tg), _seq_rows(v, off, nb, t, tg), _seq_rows(rs, off, nb, t, tg),
                          s0t, gn, nb=nb, t=tg, tb=tg)
            yl, hl, cb = _lru(_seq_rows(xb, off, nb, t, tl), _seq_rows(gg, off, nb, t, tl), buf0, h_lru[:, None],
                              lru_consts, nb=nb, t=tl, tb=tl, t_last=t - 1)
            og = _unpad_rows(og, nb, t, tg)
            yl = _unpad_rows(yl, nb, t, tl)
        else:
            og, st = _gla(qkl, v, rs, s0t, gn, nb=nb, t=t, tb=GLA_TIME_BLOCK)
            yl, hl, cb = _lru(xb, gg, buf0, h_lru[:, None], lru_consts, nb=nb, t=t, tb=tb, t_last=tb - 1)
        ogs.append(og)
        yls.append(yl)
        states.append((jnp.swapaxes(st[..., :GLA_DK], 2, 3), hl[:, 0], cb))
    n_used = sum(o.shape[0] for o in ogs)
    tail = jnp.zeros((x.shape[0] - n_used, LRU_WIDTH), F32)
    og = jnp.concatenate(ogs + [tail], axis=0)
    yl = jnp.concatenate(yls + [tail], axis=0)
    return _mix_out(og, yl, x, wo_a, wo_b, g, beta, tm=tm), states


RWKV_STREAMS = 5


def _rwkv_in_kernel(*refs, has_vres, tm, long_seqs, short_seqs):
    if has_vres:
        (x_ref, halo_ref, sl_ref, ov_ref, mu_ref, wr_ref, wk_ref, wv_ref, w1_ref, w2_ref, a1_ref, a2_ref, g1_ref, g2_ref,
         w0_ref, a0_ref, vf_ref, v1_ref, v2_ref, v0_ref, p_ref, g_ref, xs_ref) = refs
    else:
        (x_ref, halo_ref, sl_ref, ov_ref, mu_ref, wr_ref, wk_ref, wv_ref, w1_ref, w2_ref, a1_ref, a2_ref, g1_ref, g2_ref,
         w0_ref, a0_ref, p_ref, g_ref, xs_ref) = refs
    d = D_MODEL
    x = x_ref[...]
    r0 = pl.program_id(0) * tm
    xs_ref[0:SUBLANES] = halo_ref[...]
    xs_ref[SUBLANES:SUBLANES + tm] = x
    prev = xs_ref[pl.ds(SUBLANES - 1, tm), :]
    rows = r0 + lax.broadcasted_iota(jnp.int32, (tm, 1), 0)
    off, cnt, length = long_seqs
    b = (jnp.maximum(r0 - off, 0) + (length - 1)) // length
    hit = jnp.logical_and(rows == off + b * length, b < cnt)
    prev = jnp.where(hit, sl_ref[pl.ds(jnp.minimum(b, cnt - 1), 1), :], prev)
    off, cnt, length = short_seqs
    rel = rows - off
    first = jnp.logical_and(jnp.logical_and(rel >= 0, rel < cnt * length), jnp.bitwise_and(rel, length - 1) == 0)
    prev = jnp.where(first, ov_ref[...], prev)
    xx = prev - x
    mix = lambda i: (x + xx * mu_ref[i:i + 1, :]).astype(BF16)
    xr, xw, xk, xv, xa, xg = (mix(i) for i in range(6))
    p_ref[:, 0:d] = _dot(xr, wr_ref[...])
    p_ref[:, d:2 * d] = _dot(xk, wk_ref[...])
    v = _dot(xv, wv_ref[...])
    if has_vres:
        v = v + (vf_ref[...] - v) * _sigmoid(v0_ref[...] + _dot(_dot(xv, v1_ref[...]), v2_ref[...]))
    p_ref[:, 2 * d:3 * d] = v
    p_ref[:, 3 * d:4 * d] = w0_ref[...] + _dot(jnp.tanh(_dot(xw, w1_ref[...])), w2_ref[...])
    p_ref[:, 4 * d:5 * d] = a0_ref[...] + _dot(_dot(xa, a1_ref[...]), a2_ref[...])
    g_ref[...] = _dot(_sigmoid(_dot(xg, g1_ref[...])), g2_ref[...])


def _rwkv_in(x, shift_long, shift_short, consts, vres, *, tm, long_seqs, short_seqs):
    n, d = x.shape
    off, cnt, length = short_seqs
    assert length & (length - 1) == 0 and long_seqs[2] >= tm and tm % SUBLANES == 0
    tile0 = off // tm
    tiles = -(-(off + cnt * length) // tm) - tile0
    ov = jnp.pad(shift_short[:, None], ((0, 0), (0, length - 1), (0, 0))).reshape(cnt * length, d)
    ov = jnp.pad(ov, ((off - tile0 * tm, tiles * tm - (off - tile0 * tm) - cnt * length), (0, 0)))
    sl = jnp.pad(shift_long, ((0, -shift_long.shape[0] % SUBLANES), (0, 0)))
    row = lambda w: pl.BlockSpec((tm, w), lambda i: (i, 0))
    halo = pl.BlockSpec((SUBLANES, d), lambda i: (jnp.maximum(i * (tm // SUBLANES) - 1, 0), 0))
    ov_spec = pl.BlockSpec((tm, d), lambda i: (jnp.clip(i - tile0, 0, tiles - 1), 0))
    args = [x, x, sl, ov] + list(consts)
    specs = [row(d), halo, _const_spec(sl.shape), ov_spec] + [_const_spec(c.shape) for c in consts]
    if vres is not None:
        vf, vconsts = vres
        args += [vf] + list(vconsts)
        specs += [row(d)] + [_const_spec(c.shape) for c in vconsts]
    return pl.pallas_call(
        functools.partial(_rwkv_in_kernel, has_vres=vres is not None, tm=tm, long_seqs=long_seqs, short_seqs=short_seqs),
        out_shape=[jax.ShapeDtypeStruct((n, RWKV_STREAMS * d), F32), jax.ShapeDtypeStruct((n, d), F32)],
        grid=(n // tm,),
        in_specs=specs,
        out_specs=[row(RWKV_STREAMS * d), row(d)],
        scratch_shapes=[pltpu.VMEM((tm + SUBLANES, d), F32)],
        compiler_params=_params("parallel"),
        name="rwkv_in",
    )(*args)


WKV_SEQS = LANES // RWKV_HEADS


def _wkv_kernel(*refs, tt):
    tok_refs = refs[:WKV_SEQS]
    s0_ref, kk_ref, ka_ref, rk_ref, lg_ref, lb_ref = refs[WKV_SEQS:WKV_SEQS + 6]
    y_refs = refs[WKV_SEQS + 6:2 * WKV_SEQS + 6]
    sout_ref, s_ref, p_ref, dec_ref, a_ref, b_ref, km_ref, bon_ref = refs[2 * WKV_SEQS + 6:]
    t = pl.program_id(1)
    hk = RWKV_HEAD

    @pl.when(t == 0)
    def _init():
        s_ref[...] = s0_ref[...]

    def relayout(i, carry):
        for s in range(RWKV_STREAMS):
            rows = jnp.concatenate([tok_refs[q][i, s] for q in range(WKV_SEQS)], axis=0)
            p_ref[i, s] = rows.T
        return carry

    lax.fori_loop(0, tt, relayout, 0)

    r = p_ref[:, 0]
    k = p_ref[:, 1]
    v = p_ref[:, 2]
    dec_ref[...] = jnp.exp(-jnp.exp(-_softplus(-p_ref[:, 3]) - 0.5))
    ag = _sigmoid(p_ref[:, 4])
    kk = k * kk_ref[...][None]
    kk = kk / jnp.maximum(jnp.sqrt(jnp.sum(kk * kk, axis=1, keepdims=True)), 1e-12)
    km = k * (1.0 + (ag - 1.0) * ka_ref[...][None])
    km_ref[...] = km
    a_ref[...] = -kk
    b_ref[...] = kk * ag
    bon_ref[...] = jnp.sum(r * km * rk_ref[...][None], axis=1, keepdims=True) * v
    lg = lg_ref[...]
    lb = lb_ref[...]

    def step(i, carry):
        lanes = 4
        parts = [jnp.zeros((hk, LANES), F32) for _ in range(lanes)]
        for q in range(hk):
            parts[q % lanes] = parts[q % lanes] + s_ref[q] * a_ref[i, q:q + 1, :]
        sa = (parts[0] + parts[1]) + (parts[2] + parts[3])
        vv = p_ref[i, 2]
        parts = [jnp.zeros((hk, LANES), F32) for _ in range(lanes)]
        for q in range(hk):
            sn = s_ref[q] * dec_ref[i, q:q + 1, :] + (sa * b_ref[i, q:q + 1, :] + vv * km_ref[i, q:q + 1, :])
            s_ref[q] = sn
            parts[q % lanes] = parts[q % lanes] + sn * p_ref[i, 0, q:q + 1, :]
        y = (parts[0] + parts[1]) + (parts[2] + parts[3])
        mu = jnp.mean(y, axis=0, keepdims=True)
        dlt = y - mu
        var = jnp.mean(dlt * dlt, axis=0, keepdims=True)
        out = dlt * lax.rsqrt(var + RWKV_GN_EPS) * lg + lb + bon_ref[i]
        rows = out.T
        for q in range(WKV_SEQS):
            y_refs[q][i] = rows[q * RWKV_HEADS:(q + 1) * RWKV_HEADS]
        return carry

    lax.fori_loop(0, tt, step, 0)

    @pl.when(t == pl.num_programs(1) - 1)
    def _fin():
        sout_ref[...] = s_ref[...]


def _wkv(p, s0, lane_consts, *, off, nb, t, tt):
    hk = RWKV_HEAD
    l = nb * RWKV_HEADS
    assert off % tt == 0 and t % tt == 0 and nb % WKV_SEQS == 0
    p4 = p.reshape(p.shape[0], RWKV_STREAMS, RWKV_HEADS, hk)
    tok = lambda q: pl.BlockSpec((tt, RWKV_STREAMS, RWKV_HEADS, hk),
                                 lambda g, i: ((off + (g * WKV_SEQS + q) * t) // tt + i, 0, 0, 0))
    lane2 = pl.BlockSpec((hk, LANES), lambda g, i: (0, g))
    st = pl.BlockSpec((hk, hk, LANES), lambda g, i: (0, 0, g))
    seq = lambda: pltpu.VMEM((tt, hk, LANES), F32)
    groups, nt = nb // WKV_SEQS, t // tt
    y_spec = pl.BlockSpec((tt, RWKV_HEADS, hk), lambda g, i: (g * nt + i, 0, 0))
    outs = pl.pallas_call(
        functools.partial(_wkv_kernel, tt=tt),
        out_shape=[jax.ShapeDtypeStruct((groups * t, RWKV_HEADS, hk), F32)] * WKV_SEQS
        + [jax.ShapeDtypeStruct((hk, hk, l), F32)],
        grid=(groups, nt),
        in_specs=[tok(q) for q in range(WKV_SEQS)] + [st] + [lane2] * 5,
        out_specs=[y_spec] * WKV_SEQS + [st],
        scratch_shapes=[pltpu.VMEM((hk, hk, LANES), F32), pltpu.VMEM((tt, RWKV_STREAMS, hk, LANES), F32),
                        seq(), seq(), seq(), seq(), seq()],
        compiler_params=_params("parallel", "arbitrary"),
        name="wkv",
    )(*([p4] * WKV_SEQS), s0, *lane_consts)
    y = jnp.stack(outs[:WKV_SEQS]).reshape(WKV_SEQS, groups, t, RWKV_HEADS * hk)
    return jnp.swapaxes(y, 0, 1).reshape(nb * t, RWKV_HEADS * hk), outs[WKV_SEQS]


def _gate_out_kernel(y_ref, g_ref, x_ref, w_ref, lg_ref, lb_ref, o_ref):
    m = _dot(y_ref[...] * g_ref[...], w_ref[...])
    o_ref[...] = _layer_norm_rows(DN_ALPHA * x_ref[...] + m, lg_ref[...], lb_ref[...])


def _gate_out(y, g, x, w, lg, lb, *, tm):
    n, d = x.shape
    row = pl.BlockSpec((tm, d), lambda i: (i, 0))
    consts = [w, lg, lb]
    return pl.pallas_call(
        _gate_out_kernel,
        out_shape=jax.ShapeDtypeStruct((n, d), F32),
        grid=(n // tm,),
        in_specs=[row, row, row] + [_const_spec(c.shape) for c in consts],
        out_specs=row,
        compiler_params=_params("parallel"),
        name="gate_out",
    )(y, g, x, *consts)


def _pad_cols(w):
    return jnp.pad(w, ((0, 0), (0, LANES - w.shape[1]))).astype(BF16)


def _pad_rows(w):
    return jnp.pad(w, ((0, LANES - w.shape[0]), (0, 0))).astype(BF16)


def _rwkv_weights(mu, w_r, w_k, w_v, w_o, w0, w1, w2, a0, a1, a2, g1, g2, k_k, k_a, r_k, lnx_g, lnx_b, vres):
    consts = [jnp.pad(mu, ((0, SUBLANES - mu.shape[0]), (0, 0))), w_r.astype(BF16), w_k.astype(BF16), w_v.astype(BF16),
              _pad_cols(w1), _pad_rows(w2), _pad_cols(a1), _pad_rows(a2), _pad_cols(g1), _pad_rows(g2), w0[None], a0[None]]
    vconsts = None if vres is None else [_pad_cols(vres[1]), _pad_rows(vres[2]), vres[0][None]]
    per_head = [k_k.reshape(RWKV_HEADS, RWKV_HEAD).T, k_a.reshape(RWKV_HEADS, RWKV_HEAD).T, r_k.T,
                lnx_g.reshape(RWKV_HEADS, RWKV_HEAD).T, lnx_b.reshape(RWKV_HEADS, RWKV_HEAD).T]
    return consts, vconsts, per_head, w_o.astype(BF16)


def _rwkv_layer(x, groups, weights, v_first, g, beta, *, tm):
    consts, vconsts, per_head, w_o = weights
    n, d = x.shape
    (off_l, nb_l, t_l, _, _, shift_l), (off_s, nb_s, t_s, _, _, shift_s) = groups
    n_used = nb_l * t_l + nb_s * t_s
    vres = None if vconsts is None else (v_first, vconsts)
    p, gate = _rwkv_in(x, shift_l, shift_s, consts, vres, tm=tm,
                       long_seqs=(off_l, nb_l, t_l), short_seqs=(off_s, nb_s, t_s))
    ys, states = [], []
    for off, nb, t, tt, s, shift in groups:
        lanes = nb * RWKV_HEADS
        hk2 = RWKV_HEAD * RWKV_HEAD
        s0 = jnp.swapaxes(s, 2, 3).reshape(lanes, hk2).T.reshape(RWKV_HEAD, RWKV_HEAD, lanes)
        lane_consts = [jnp.tile(c, (1, nb)) for c in per_head]
        y, s_out = _wkv(p, s0, lane_consts, off=off, nb=nb, t=t, tt=tt)
        ys.append(y)
        s_new = jnp.swapaxes(s_out.reshape(hk2, lanes).T.reshape(nb, RWKV_HEADS, RWKV_HEAD, RWKV_HEAD), 2, 3)
        states.append((s_new, x[off:off + nb * t].reshape(nb, t, d)[:, -1]))
    y = jnp.concatenate(ys + [jnp.zeros((n - n_used, d), F32)], axis=0)
    return _gate_out(y, gate, x, w_o, g, beta, tm=tm), states, p[:, 2 * d:3 * d]


TOKEN_TILE = 512
PEER_TOKEN_TILE = 1024
PROMPT_TIME_BLOCK = 688
WKV_TIME_BLOCK = 16


def kernel(x_prompt, x_sample, state_gla, state_lru_h, state_lru_conv, state_rwkv, state_rwkv_shift, meta_tokens, ln_g, ln_b, ev_w_in, ev_gla_w_gate, ev_gla_b_gate, ev_gla_norm, ev_conv_w, ev_conv_b, ev_lru_wa, ev_lru_ba, ev_lru_wx, ev_lru_bx, ev_lru_lambda, ev_w_out, od_mu, od_w_r, od_w_k, od_w_v, od_w_o, od_w0, od_w1, od_w2, od_a0, od_a1, od_a2, od_v0, od_v1, od_v2, od_g1, od_g2, od_k_k, od_k_a, od_r_k, od_lnx_g, od_lnx_b, peer_w_q, peer_keys, peer_u, peer_v):
    bp, sp, d = x_prompt.shape
    bs, ss, _ = x_sample.shape
    tp = sp + N_META
    n_p, n_s = bp * tp, bs * ss
    n = -(-(n_p + n_s) // PEER_TOKEN_TILE) * PEER_TOKEN_TILE
    assert tp % PROMPT_TIME_BLOCK == 0 and tp % GLA_TIME_BLOCK == 0 and tp % WKV_TIME_BLOCK == 0 and n % TOKEN_TILE == 0

    xp = jnp.concatenate([jnp.broadcast_to(meta_tokens[None], (bp, N_META, d)), x_prompt], axis=1)
    x = jnp.concatenate([xp.reshape(n_p, d), x_sample.reshape(n_s, d), jnp.zeros((n - n_p - n_s, d), F32)], axis=0)

    n_pairs = DEPTH // 2
    zeros = lambda *s: jnp.zeros(s, F32)
    outs = {k: [] for k in ("p_gla", "p_h", "p_conv", "p_rwkv", "p_shift", "s_gla", "s_h", "s_conv", "s_rwkv", "s_shift")}
    v_first = None
    for layer in range(DEPTH):
        j = layer // 2
        g, beta = ln_g[layer, 0][None], ln_b[layer, 0][None]
        if layer % 2 == 0:
            weights = _even_weights(ev_w_in[j], ev_gla_w_gate[j], ev_gla_b_gate[j], ev_gla_norm[j], ev_conv_w[j], ev_conv_b[j],
                                    ev_lru_wa[j], ev_lru_ba[j], ev_lru_wx[j], ev_lru_bx[j], ev_lru_lambda[j], ev_w_out[j])
            groups = [(0, bp, tp, PROMPT_TIME_BLOCK, zeros(bp, GLA_HEADS, GLA_DK, GLA_DV), zeros(bp, LRU_WIDTH),
                       zeros(bp, CONV_W - 1, LRU_WIDTH)),
                      (n_p, bs, ss, None, state_gla[j], state_lru_h[j], state_lru_conv[j])]
            x, states = _even_layer(x, groups, weights, g, beta, tm=TOKEN_TILE)
            for pre, st in zip("ps", states):
                outs[pre + "_gla"].append(st[0])
                outs[pre + "_h"].append(st[1])
                outs[pre + "_conv"].append(st[2])
        else:
            vres = None if j == 0 else (od_v0[j - 1], od_v1[j - 1], od_v2[j - 1])
            weights = _rwkv_weights(od_mu[j], od_w_r[j], od_w_k[j], od_w_v[j], od_w_o[j], od_w0[j], od_w1[j], od_w2[j],
                                    od_a0[j], od_a1[j], od_a2[j], od_g1[j], od_g2[j], od_k_k[j], od_k_a[j], od_r_k[j],
                                    od_lnx_g[j], od_lnx_b[j], vres)
            groups = [(0, bp, tp, WKV_TIME_BLOCK, zeros(bp, RWKV_HEADS, RWKV_HEAD, RWKV_HEAD), zeros(bp, d)),
                      (n_p, bs, ss, ss, state_rwkv[j], state_rwkv_shift[j])]
            x, states, v = _rwkv_layer(x, groups, weights, v_first, g, beta, tm=TOKEN_TILE)
            if v_first is None:
                v_first = v
            for pre, st in zip("ps", states):
                outs[pre + "_rwkv"].append(st[0])
                outs[pre + "_shift"].append(st[1])
        x = _peer_layer(x, peer_w_q[layer].T.astype(BF16), peer_keys[layer, :, 0].astype(BF16),
                        peer_keys[layer, :, 1].astype(BF16), peer_u[layer].astype(BF16), peer_v[layer].astype(BF16),
                        ln_g[layer, 1][None], ln_b[layer, 1][None], tn=PEER_TOKEN_TILE)

    y_prompt = x[:n_p].reshape(bp, tp, d)[:, N_META:]
    y_sample = x[n_p:n_p + n_s].reshape(bs, ss, d)
    st = {k: jnp.stack(v) for k, v in outs.items()}
    return (y_prompt, y_sample, st["p_gla"], st["p_h"], st["p_conv"], st["p_rwkv"], st["p_shift"],
            st["s_gla"], st["s_h"], st["s_conv"], st["s_rwkv"], st["s_shift"])
```

```python
import functools

import jax
import jax.numpy as jnp
from jax import lax
from jax.experimental import pallas as pl
from jax.experimental.pallas import tpu as pltpu

F32 = jnp.float32
BF16 = jnp.bfloat16

D_MODEL = 1024
DEPTH = 4
N_META = 16
GLA_HEADS = 4
GLA_DK = 64
GLA_DV = 128
GLA_RANK = 16
GLA_GATE_NORM = 16.0
GLA_CHUNK = 16
LRU_WIDTH = 512
LRU_BLOCKS = 8
CONV_W = 4
LRU_C = 8.0
RWKV_HEAD = 64
RWKV_HEADS = D_MODEL // RWKV_HEAD
RWKV_GN_EPS = 64e-5
PEER_HEADS = 8
PEER_NKEYS = 128
PEER_DKEY = 256
PEER_HALF = PEER_DKEY // 2
PEER_TOPK = 16
DN_ALPHA = float((2 * DEPTH) ** 0.25)
LN_EPS = 1e-5

LANES = 128
SUBLANES = 8
VMEM_LIMIT = 56 * 1024 * 1024


def _dot(a, b):
    return jnp.dot(a.astype(BF16), b.astype(BF16), preferred_element_type=F32)


def _dot_nt(a, b):
    return lax.dot_general(a.astype(BF16), b.astype(BF16), (((1,), (1,)), ((), ())), preferred_element_type=F32)


def _dot_tn(a, b):
    return lax.dot_general(a.astype(BF16), b.astype(BF16), (((0,), (0,)), ((), ())), preferred_element_type=F32)


def _layer_norm_rows(z, g, b):
    mu = jnp.mean(z, axis=-1, keepdims=True)
    d = z - mu
    var = jnp.mean(d * d, axis=-1, keepdims=True)
    return d * lax.rsqrt(var + LN_EPS) * g + b


def _sigmoid(x):
    return 1.0 / (1.0 + jnp.exp(-x))


def _softplus(x):
    return jnp.maximum(x, 0.0) + jnp.log1p(jnp.exp(-jnp.abs(x)))


def _gelu_tanh(x):
    c0 = 0.7978845608028654
    c1 = c0 * 0.044715
    one = jnp.asarray(1.0, x.dtype)
    return (0.5 * x) * (one + jnp.tanh(x * (c0 + c1 * (x * x))))


def _silu(x):
    return x * _sigmoid(x)


def _params(*sem):
    return pltpu.CompilerParams(dimension_semantics=sem, vmem_limit_bytes=VMEM_LIMIT)


def _const_spec(shape):
    nd = len(shape)
    return pl.BlockSpec(shape, lambda *_: (0,) * nd)


PEER_CAND_ROWS = 80
PEER_PACK = 2 * SUBLANES
PEER_SELECT_LANES = 4 * LANES


def _peer_cand_index():
    rows = [r1 * PEER_TOPK for r1 in range(16)]
    for j in range(1, 8):
        rows += [r1 * PEER_TOPK + j for r1 in range(8)]
    rows += list(range(8, 16))
    return jnp.broadcast_to(jnp.asarray(rows, F32)[:, None], (PEER_CAND_ROWS, PEER_SELECT_LANES))


PEER_CODE_STEP = 2.0 ** 116
PEER_CODED_BELOW = -(2.0 ** 119)


def _rank_code(r):
    return -PEER_CODE_STEP * (PEER_TOPK + r)


def _decode_rank(s):
    return jnp.where(s < PEER_CODED_BELOW, s * (-1.0 / PEER_CODE_STEP) - PEER_TOPK, float(PEER_TOPK))


def _count_coded(s):
    return jnp.sum(jnp.where(s < PEER_CODED_BELOW, 1.0, 0.0), axis=0, keepdims=True)


def _top16(scores, key_iota):
    row16 = lax.broadcasted_iota(jnp.int32, (PEER_TOPK, scores[0].shape[1]), 0)
    vals = [jnp.zeros((PEER_TOPK, s.shape[1]), F32) for s in scores]
    scores = list(scores)
    for r in range(PEER_TOPK):
        tops = [jnp.max(s, axis=0, keepdims=True) for s in scores]
        hits = [s == m for s, m in zip(scores, tops)]
        if key_iota is not None:
            firsts = [jnp.min(jnp.where(hit, key_iota, float(PEER_NKEYS)), axis=0, keepdims=True) for hit in hits]
            hits = [key_iota == first for first in firsts]
        scores = [jnp.where(hit, _rank_code(r), s) for hit, s in zip(hits, scores)]
        vals = [jnp.where(row16 == r, m, v) for m, v in zip(tops, vals)]
    return list(zip(vals, scores))


def _peer_candidates(a, b):
    blocks = [a + b[0:1]]
    for j in range(1, 8):
        blocks.append(a[0:8] + b[j:j + 1])
    blocks.append(a[0:1] + b[8:16])
    return jnp.concatenate(blocks, axis=0)


def _select16(c, cidx):
    for _ in range(PEER_TOPK):
        m = jnp.max(c, axis=0, keepdims=True)
        hit = c == m
        if cidx is not None:
            first = jnp.min(jnp.where(hit, cidx, 1e9), axis=0, keepdims=True)
            hit = cidx == first
        c = jnp.where(hit, _rank_code(0), c)
    return c


def _peer_gate_tiles(s1, s2, a, b, s1c, s2c, c, cc):
    sel = jnp.where(cc < PEER_CODED_BELOW, 1.0, 0.0)
    z = jnp.sum(sel * jnp.exp(c - c[0:1]), axis=0, keepdims=True)
    j_lo = sel[0:8]
    for j in range(1, 8):
        j_lo = j_lo + sel[8 + 8 * j:16 + 8 * j]
    extra = jnp.sum(sel[72:80], axis=0, keepdims=True)
    row8 = lax.broadcasted_iota(jnp.int32, j_lo.shape, 0)
    j_lo = j_lo + jnp.where(row8 == 0, extra, 0.0)
    jt = jnp.concatenate([j_lo, sel[8:16]], axis=0)
    r1 = _decode_rank(s1c)
    h1 = jnp.zeros_like(r1)
    for r in range(PEER_TOPK):
        h1 = jnp.where(r1 == float(r), jt[r:r + 1], h1)
    return h1, jnp.exp(s1 - a[0:1]) / z, _decode_rank(s2c), jnp.exp(s2 - b[0:1])


def _peer_kernel(x_ref, wqt_ref, k1_ref, k2_ref, cidx_ref, u_ref, vt_ref, g_ref, b_ref, o_ref,
                 xt_ref, acc_ref, h1_ref, c1_ref, r2_ref, e2_ref, s_ref, hd0_ref, hd1_ref, coef_ref, *, tn, te):
    j = pl.program_id(1)
    nj = pl.num_programs(1)

    @pl.when(j == 0)
    def _select():
        xt_ref[...] = x_ref[...].T.astype(BF16)
        acc_ref[...] = jnp.zeros_like(acc_ref)
        hd1_ref[...] = jnp.zeros_like(hd1_ref)
        coef_ref[0:te] = jnp.zeros((te, tn), BF16)
        key_iota = lax.broadcasted_iota(jnp.int32, (PEER_NKEYS, PEER_SELECT_LANES), 0).astype(F32)

        def head(h, carry):
            q = _dot(wqt_ref[pl.ds(pl.multiple_of(h * PEER_DKEY, PEER_DKEY), PEER_DKEY), :], xt_ref[...])
            mu = jnp.mean(q, axis=0, keepdims=True)
            d = q - mu
            qn = d * lax.rsqrt(jnp.mean(d * d, axis=0, keepdims=True) + LN_EPS)
            s_ref[0] = _dot(k1_ref[h], qn[0:PEER_HALF])
            s_ref[1] = _dot(k2_ref[h], qn[PEER_HALF:PEER_DKEY])

            def chunk(ci, carry2):
                ls = pl.ds(pl.multiple_of(ci * PEER_SELECT_LANES, PEER_SELECT_LANES), PEER_SELECT_LANES)
                s1 = s_ref[0, :, ls]
                s2 = s_ref[1, :, ls]

                def put(tiles):
                    h1_ref[h, :, ls] = tiles[0]
                    c1_ref[h, :, ls] = tiles[1]
                    r2_ref[h, :, ls] = tiles[2].astype(BF16)
                    e2_ref[h, :, ls] = tiles[3].astype(BF16)

                (a, s1c), (b, s2c) = _top16([s1, s2], None)
                c = _peer_candidates(a, b)
                cc = _select16(c, None)
                put(_peer_gate_tiles(s1, s2, a, b, s1c, s2c, c, cc))
                miscount = (jnp.abs(_count_coded(s1c) - PEER_TOPK) + jnp.abs(_count_coded(s2c) - PEER_TOPK)
                            + jnp.abs(_count_coded(cc) - PEER_TOPK))

                @pl.when(jnp.max(miscount) > 0.0)
                def _with_ties():
                    (a, s1c), (b, s2c) = _top16([s1, s2], key_iota)
                    c = _peer_candidates(a, b)
                    cc = _select16(c, cidx_ref[...])
                    put(_peer_gate_tiles(s1, s2, a, b, s1c, s2c, c, cc))

                return carry2

            lax.fori_loop(0, tn // PEER_SELECT_LANES, chunk, 0)
            return carry

        lax.fori_loop(0, PEER_HEADS, head, 0)

    groups = PEER_NKEYS // PEER_PACK
    last_key = PEER_NKEYS - 1
    zero_pack = jnp.zeros((PEER_PACK, LANES), F32)

    def gate_rows(tile):
        rows = []
        for ii in range(te // PEER_NKEYS):
            i1 = jnp.clip(tile * (te // PEER_NKEYS) + ii, 0, last_key)
            rows.append(([h1_ref[h, pl.ds(i1, 1), :] for h in range(PEER_HEADS)],
                         [c1_ref[h, pl.ds(i1, 1), :] for h in range(PEER_HEADS)]))
        return rows

    def gate_stage(rows, hd_ref, coef_rows, lane_groups):
        nk = len(rows)
        for c in lane_groups:
            ls = slice(c * LANES, (c + 1) * LANES)
            gates = [[jnp.zeros((PEER_PACK, LANES), BF16) for _ in range(groups)] for _ in range(nk)]
            for h in range(PEER_HEADS):
                bounds = [(rows[ii][0][h][:, ls] + zero_pack).astype(BF16) for ii in range(nk)]
                scales = [(rows[ii][1][h][:, ls] + zero_pack).astype(BF16) for ii in range(nk)]
                for gi in range(groups):
                    krows = slice(gi * PEER_PACK, (gi + 1) * PEER_PACK)
                    rank = r2_ref[h, krows, ls]
                    fac = e2_ref[h, krows, ls]
                    for ii in range(nk):
                        gates[ii][gi] = gates[ii][gi] + jnp.where(rank < bounds[ii], fac * scales[ii], jnp.zeros((), BF16))
            for ii in range(nk):
                for gi in range(groups):
                    erows = slice(ii * PEER_NKEYS + gi * PEER_PACK, ii * PEER_NKEYS + (gi + 1) * PEER_PACK)
                    crows = slice(coef_rows + erows.start, coef_rows + erows.stop)
                    coef_ref[crows, ls] = gates[ii][gi] * _gelu_tanh(hd_ref[erows, ls].astype(BF16))

    piece = 2 * LANES
    per = piece // LANES
    rows_odd = gate_rows(2 * j - 1)
    rows_even = gate_rows(2 * j)
    pieces = tn // piece
    lanes_of = lambda c: slice(c * piece, (c + 1) * piece)

    def down_project(c):
        acc_ref[:, lanes_of(c)] += jnp.dot(vt_ref[0], coef_ref[:, lanes_of(c)], preferred_element_type=F32)

    for c in range(pieces):
        gate_stage(rows_odd, hd1_ref, te, range(c * per, (c + 1) * per))
        hd0_ref[:, lanes_of(c)] = _dot(u_ref[0:te, :], xt_ref[:, lanes_of(c)])
        hd1_ref[:, lanes_of(c)] = _dot(u_ref[te:2 * te, :], xt_ref[:, lanes_of(c)])
        if c > 0:
            down_project(c - 1)
    down_project(pieces - 1)
    for c in range(pieces):
        gate_stage(rows_even, hd0_ref, 0, range(c * per, (c + 1) * per))

    @pl.when(j == nj - 1)
    def _finish():
        y = acc_ref[...].T
        o_ref[...] = _layer_norm_rows(DN_ALPHA * x_ref[...] + y, g_ref[...], b_ref[...])


def _peer_layer(x, wqt, k1, k2, u, v, g, b, *, tn=1024, te=256):
    n, d = x.shape
    ne = u.shape[0]
    assert ne % (2 * te) == 0 and n % tn == 0
    pairs = ne // (2 * te)
    steps = pairs + 1
    vt = jnp.swapaxes(v.reshape(pairs, 2 * te, d), 1, 2)
    kern = functools.partial(_peer_kernel, tn=tn, te=te)
    one = pl.Buffered(1)
    return pl.pallas_call(
        kern,
        out_shape=jax.ShapeDtypeStruct((n, d), F32),
        grid=(n // tn, steps),
        in_specs=[
            pl.BlockSpec((tn, d), lambda i, j: (i, 0), pipeline_mode=one),
            pl.BlockSpec(wqt.shape, lambda i, j: (0, 0), pipeline_mode=one),
            pl.BlockSpec(k1.shape, lambda i, j: (0, 0, 0), pipeline_mode=one),
            pl.BlockSpec(k2.shape, lambda i, j: (0, 0, 0), pipeline_mode=one),
            pl.BlockSpec((PEER_CAND_ROWS, PEER_SELECT_LANES), lambda i, j: (0, 0), pipeline_mode=one),
            pl.BlockSpec((2 * te, d), lambda i, j: (jnp.minimum(j, pairs - 1), 0)),
            pl.BlockSpec((1, d, 2 * te), lambda i, j: (jnp.maximum(j - 1, 0), 0, 0)),
            pl.BlockSpec((1, d), lambda i, j: (0, 0), pipeline_mode=one),
            pl.BlockSpec((1, d), lambda i, j: (0, 0), pipeline_mode=one),
        ],
        out_specs=pl.BlockSpec((tn, d), lambda i, j: (i, 0), pipeline_mode=one),
        scratch_shapes=[
            pltpu.VMEM((d, tn), BF16),
            pltpu.VMEM((d, tn), F32),
            pltpu.VMEM((PEER_HEADS, PEER_NKEYS, tn), F32),
            pltpu.VMEM((PEER_HEADS, PEER_NKEYS, tn), F32),
            pltpu.VMEM((PEER_HEADS, PEER_NKEYS, tn), BF16),
            pltpu.VMEM((PEER_HEADS, PEER_NKEYS, tn), BF16),
            pltpu.VMEM((2, PEER_NKEYS, tn), F32),
            pltpu.VMEM((te, tn), F32),
            pltpu.VMEM((te, tn), F32),
            pltpu.VMEM((2 * te, tn), BF16),
        ],
        compiler_params=_params("parallel", "arbitrary"),
        name="peer",
    )(x, wqt, k1, k2, _peer_cand_index(), u, vt, g, b)


GLA_PAD = GLA_HEADS * LANES


def _even_in_kernel(x_ref, wq_ref, wk_ref, wv_ref, wr_ref, wxb_ref, wgb_ref, wlr_ref, wg_ref, bg_ref,
                    qkl_ref, v_ref, rs_ref, xb_ref, gg_ref):
    xb16 = x_ref[...].astype(BF16)
    qkl_ref[:, 0:GLA_PAD] = _dot(xb16, wq_ref[...]) * (GLA_DK ** -0.5)
    qkl_ref[:, GLA_PAD:2 * GLA_PAD] = _dot(xb16, wk_ref[...])
    glr = _dot(xb16, wlr_ref[...])
    z = _dot(glr, wg_ref[...]) + bg_ref[...]
    qkl_ref[:, 2 * GLA_PAD:3 * GLA_PAD] = -_softplus(-z) * (1.0 / GLA_GATE_NORM)
    v_ref[...] = _dot(xb16, wv_ref[...])
    rs_ref[...] = _silu(_dot(xb16, wr_ref[...]))
    xb_ref[...] = _dot(xb16, wxb_ref[...])
    gg_ref[...] = _gelu_tanh(_dot(xb16, wgb_ref[...]))


def _even_in(x, consts, *, tm=512):
    n, d = x.shape
    row = lambda w: pl.BlockSpec((tm, w), lambda i: (i, 0))
    return pl.pallas_call(
        _even_in_kernel,
        out_shape=[jax.ShapeDtypeStruct((n, 3 * GLA_PAD), F32)] + [jax.ShapeDtypeStruct((n, LRU_WIDTH), F32)] * 4,
        grid=(n // tm,),
        in_specs=[row(d)] + [_const_spec(c.shape) for c in consts],
        out_specs=[row(3 * GLA_PAD)] + [row(LRU_WIDTH)] * 4,
        compiler_params=_params("parallel"),
        name="even_in",
    )(x, *consts)


GLA_SEQS_PER_STEP = 8
GLA_TIME_BLOCK = 3 * GLA_CHUNK


def _gla_kernel(*refs, tb, ns):
    qkl_refs, v_refs, rs_refs, s0_refs = (refs[i * ns:(i + 1) * ns] for i in range(4))
    tri_ref, gn_ref = refs[4 * ns:4 * ns + 2]
    o_refs = refs[4 * ns + 2:5 * ns + 2]
    sout_refs = refs[5 * ns + 2:6 * ns + 2]
    st_ref = refs[6 * ns + 2]
    t = pl.program_id(1)

    @pl.when(t == 0)
    def _init():
        for p in range(ns):
            st_ref[p] = s0_refs[p][0]

    tri = tri_ref[...]
    causal = tri > 0
    gn = gn_ref[...]

    def chunk(c, carry):
        rows = pl.ds(pl.multiple_of(c * GLA_CHUNK, GLA_CHUNK), GLA_CHUNK)
        seqs = range(ns)
        pairs = [(p, h) for p in seqs for h in range(GLA_HEADS)]
        sl = lambda h: slice(h * LANES, (h + 1) * LANES)
        las = [qkl_refs[p][rows, 2 * GLA_PAD:3 * GLA_PAD] for p in seqs]
        his = [la.astype(BF16) for la in las]
        r1s = [la - hi.astype(F32) for la, hi in zip(las, his)]
        mids = [r1.astype(BF16) for r1 in r1s]
        los = [(r1 - mid.astype(F32)).astype(BF16) for r1, mid in zip(r1s, mids)]
        parts = [[jnp.dot(tri, part, preferred_element_type=F32) for part in (hi, mid, lo)]
                 for hi, mid, lo in zip(his, mids, los)]
        cums = [a + b + c3 for a, b, c3 in parts]
        lasts = [cum[GLA_CHUNK - 1:GLA_CHUNK] for cum in cums]
        qes = [qkl_refs[p][rows, 0:GLA_PAD] * jnp.exp(cums[p]) for p in seqs]
        ks = [qkl_refs[p][rows, GLA_PAD:2 * GLA_PAD] for p in seqs]
        kes = [ks[p] * jnp.exp(-cums[p]) for p in seqs]
        kls = [ks[p] * jnp.exp(lasts[p] - cums[p]) for p in seqs]
        decs = [jnp.exp(last) for last in lasts]
        vs = [v_refs[p][rows, :] for p in seqs]
        sts = [st_ref[p, h] for p, h in pairs]
        atts = [_dot_nt(qes[p][:, sl(h)], kes[p][:, sl(h)]) for p, h in pairs]
        inters = [_dot_nt(qes[p][:, sl(h)], st) for (p, h), st in zip(pairs, sts)]
        upds = [_dot_tn(vs[p][:, sl(h)], kls[p][:, sl(h)]) for p, h in pairs]
        intras = [_dot(jnp.where(causal, att, 0.0), vs[p][:, sl(h)]) for (p, h), att in zip(pairs, atts)]
        for i, (p, h) in enumerate(pairs):
            st_ref[p, h] = sts[i] * decs[p][:, sl(h)] + upds[i]
            o = intras[i] + inters[i]
            on = o * lax.rsqrt(jnp.mean(o * o, axis=-1, keepdims=True) + LN_EPS) * gn
            o_refs[p][rows, sl(h)] = on * rs_refs[p][rows, sl(h)]
        return carry

    lax.fori_loop(0, tb // GLA_CHUNK, chunk, 0)

    @pl.when(t == pl.num_programs(1) - 1)
    def _fin():
        for p in range(ns):
            sout_refs[p][0] = st_ref[p]


def _gla(qkl, v, rs, s0t, gn, *, nb, t, tb):
    nt = t // tb
    ns = GLA_SEQS_PER_STEP
    assert nb % ns == 0
    per = nb // ns
    w = GLA_HEADS * GLA_DV
    tri = jnp.tril(jnp.ones((GLA_CHUNK, GLA_CHUNK), BF16))
    row_in = lambda width, p: pl.BlockSpec((tb, width), lambda g, i: ((p * per + g) * nt + i, 0))
    st_in = lambda p: pl.BlockSpec((1, GLA_HEADS, GLA_DV, LANES), lambda g, i: (p * per + g, 0, 0, 0))
    row_out = pl.BlockSpec((tb, w), lambda g, i: (g * nt + i, 0))
    st_out = pl.BlockSpec((1, GLA_HEADS, GLA_DV, LANES), lambda g, i: (g, 0, 0, 0))
    slots = range(ns)
    outs = pl.pallas_call(
        functools.partial(_gla_kernel, tb=tb, ns=ns),
        out_shape=[jax.ShapeDtypeStruct((per * t, w), F32)] * ns
        + [jax.ShapeDtypeStruct((per,) + s0t.shape[1:], F32)] * ns,
        grid=(per, nt),
        in_specs=[row_in(3 * GLA_PAD, p) for p in slots] + [row_in(w, p) for p in slots] + [row_in(w, p) for p in slots]
        + [st_in(p) for p in slots] + [_const_spec(tri.shape), _const_spec(gn.shape)],
        out_specs=[row_out] * ns + [st_out] * ns,
        scratch_shapes=[pltpu.VMEM((ns, GLA_HEADS, GLA_DV, LANES), F32)],
        compiler_params=_params("parallel", "arbitrary"),
        name="gla",
    )(*([qkl] * ns + [v] * ns + [rs] * ns + [s0t] * ns + [tri, gn]))
    return jnp.concatenate(outs[:ns], axis=0), jnp.concatenate(outs[ns:], axis=0)


def _lru_kernel(xb_ref, gg_ref, buf0_ref, h0_ref, cw_ref, cb_ref, wa_ref, ba_ref, wx_ref, bx_ref, lam_ref,
                y_ref, hlast_ref, cbuf_ref, xs_ref, a_ref, b_ref, hs_ref, hcar_ref, *, tb, t_last):
    t = pl.program_id(1)
    halo = SUBLANES

    @pl.when(t == 0)
    def _init():
        xs_ref[0:halo] = buf0_ref[0]
        hcar_ref[...] = h0_ref[0]

    xs_ref[halo:halo + tb] = xb_ref[...]
    xc = cb_ref[...] + sum(xs_ref[pl.ds(halo - (CONV_W - 1) + i, tb), :] * cw_ref[i:i + 1, :] for i in range(CONV_W))
    ga = _sigmoid(_dot(xc, wa_ref[...]) + ba_ref[...])
    gx = _sigmoid(_dot(xc, wx_ref[...]) + bx_ref[...])
    log_at = ga * (-LRU_C * _softplus(-lam_ref[...]))
    a = jnp.exp(log_at)
    a_ref[...] = a
    b_ref[...] = jnp.sqrt(-jnp.tanh(log_at) * (a * a + 1.0)) * (gx * xc)

    def step(i, h):
        h = a_ref[pl.ds(i, 1), :] * h + b_ref[pl.ds(i, 1), :]
        hs_ref[pl.ds(i, 1), :] = h
        return h

    hcar_ref[...] = lax.fori_loop(0, tb, step, hcar_ref[...], unroll=8)
    y_ref[...] = hs_ref[...] * gg_ref[...]

    @pl.when(t == pl.num_programs(1) - 1)
    def _fin():
        hlast_ref[0] = hs_ref[t_last:t_last + 1, :]
        cbuf_ref[0] = xs_ref[halo + t_last - (CONV_W - 2):halo + t_last + 1, :]

    xs_ref[0:halo] = xs_ref[tb:tb + halo]


def _lru(xb, gg, buf0, h0, consts, *, nb, t, tb, t_last):
    nt = t // tb
    w = LRU_WIDTH
    row = pl.BlockSpec((tb, w), lambda b, i: (b * nt + i, 0))
    per_b = lambda r: pl.BlockSpec((1, r, w), lambda b, i: (b, 0, 0))
    return pl.pallas_call(
        functools.partial(_lru_kernel, tb=tb, t_last=t_last),
        out_shape=[jax.ShapeDtypeStruct((nb * t, w), F32), jax.ShapeDtypeStruct((nb, 1, w), F32),
                   jax.ShapeDtypeStruct((nb, CONV_W - 1, w), F32)],
        grid=(nb, nt),
        in_specs=[row, row, per_b(SUBLANES), per_b(1)] + [_const_spec(c.shape) for c in consts],
        out_specs=[row, per_b(1), per_b(CONV_W - 1)],
        scratch_shapes=[pltpu.VMEM((tb + 2 * SUBLANES, w), F32), pltpu.VMEM((tb, w), F32), pltpu.VMEM((tb, w), F32),
                        pltpu.VMEM((tb, w), F32), pltpu.VMEM((1, w), F32)],
        compiler_params=_params("parallel", "arbitrary"),
        name="lru",
    )(xb, gg, buf0, h0, *consts)


def _mix_out_kernel(a_ref, b_ref, x_ref, wa_ref, wb_ref, g_ref, beta_ref, o_ref):
    m = _dot(a_ref[...], wa_ref[...]) + _dot(b_ref[...], wb_ref[...])
    o_ref[...] = _layer_norm_rows(DN_ALPHA * x_ref[...] + m, g_ref[...], beta_ref[...])


def _mix_out(a, b, x, wa, wb, g, beta, *, tm=512):
    n, d = x.shape
    row = lambda w: pl.BlockSpec((tm, w), lambda i: (i, 0))
    consts = [wa, wb, g, beta]
    return pl.pallas_call(
        _mix_out_kernel,
        out_shape=jax.ShapeDtypeStruct((n, d), F32),
        grid=(n // tm,),
        in_specs=[row(a.shape[1]), row(b.shape[1]), row(d)] + [_const_spec(c.shape) for c in consts],
        out_specs=row(d),
        compiler_params=_params("parallel"),
        name="mix_out",
    )(a, b, x, *consts)


def _pad_heads(w, heads, width):
    lead = w.shape[:-1]
    w = w.reshape(lead + (heads, width))
    w = jnp.pad(w, [(0, 0)] * len(lead) + [(0, 0), (0, LANES - width)])
    return w.reshape(lead + (heads * LANES,))


def _even_weights(w_in, w_gate, b_gate, gla_norm, conv_w, conv_b, lru_wa, lru_ba, lru_wx, lru_bx, lru_lambda, w_out):
    gk = GLA_HEADS * GLA_DK
    gv = GLA_HEADS * GLA_DV
    o = 0
    cols = {}
    for name, width in (("q", gk), ("k", gk), ("v", gv), ("lr", GLA_RANK), ("r", gv), ("xb", LRU_WIDTH), ("gb", LRU_WIDTH)):
        cols[name] = w_in[:, o:o + width]
        o += width
    wlr = jnp.pad(cols["lr"], ((0, 0), (0, LANES - GLA_RANK)))
    wg = jnp.pad(_pad_heads(w_gate, GLA_HEADS, GLA_DK), ((0, LANES - GLA_RANK), (0, 0)))
    bg = _pad_heads(b_gate[None], GLA_HEADS, GLA_DK)
    in_consts = [_pad_heads(cols["q"], GLA_HEADS, GLA_DK).astype(BF16), _pad_heads(cols["k"], GLA_HEADS, GLA_DK).astype(BF16),
                 cols["v"].astype(BF16), cols["r"].astype(BF16), cols["xb"].astype(BF16), cols["gb"].astype(BF16),
                 wlr.astype(BF16), wg.astype(BF16), bg]
    eye = jnp.eye(LRU_BLOCKS, dtype=F32)
    bd = lambda w: (eye[:, None, :, None] * w[:, :, None, :]).reshape(LRU_WIDTH, LRU_WIDTH).astype(BF16)
    lru_consts = [conv_w, conv_b[None], bd(lru_wa), lru_ba[None], bd(lru_wx), lru_bx[None], lru_lambda[None]]
    return in_consts, gla_norm[None], lru_consts, w_out[:gv].astype(BF16), w_out[gv:].astype(BF16)


def _seq_rows(a, off, nb, t, t_pad):
    a = a[off:off + nb * t].reshape(nb, t, a.shape[1])
    return jnp.pad(a, ((0, 0), (0, t_pad - t), (0, 0))).reshape(nb * t_pad, a.shape[2])


def _unpad_rows(a, nb, t, t_pad):
    return a.reshape(nb, t_pad, a.shape[1])[:, :t].reshape(nb * t, a.shape[1])


def _even_layer(x, groups, weights, g, beta, *, tm):
    in_consts, gn, lru_consts, wo_a, wo_b = weights
    qkl, v, rs, xb, gg = _even_in(x, in_consts, tm=tm)
    ogs, yls, states = [], [], []
    for off, nb, t, tb, s_gla, h_lru, cbuf in groups:
        s0t = jnp.pad(jnp.swapaxes(s_gla, 2, 3), ((0, 0), (0, 0), (0, 0), (0, LANES - GLA_DK)))
        buf0 = jnp.pad(cbuf, ((0, 0), (SUBLANES - (CONV_W - 1), 0), (0, 0)))
        if tb is None:
            tg = GLA_CHUNK
            tl = SUBLANES
            og, st = _gla(_seq_rows(qkl, off, nb, t, tg), _seq_rows(v, off, nb, t, tg), _seq_rows(rs, off, nb, t, tg),
                          s0t, gn, nb=nb, t=tg, tb=tg)
            yl, hl, cb = _lru(_seq_rows(xb, off, nb, t, tl), _seq_rows(gg, off, nb, t, tl), buf0, h_lru[:, None],
                              lru_consts, nb=nb, t=tl, tb=tl, t_last=t - 1)
            og = _unpad_rows(og, nb, t, tg)
            yl = _unpad_rows(yl, nb, t, tl)
        else:
            og, st = _gla(qkl, v, rs, s0t, gn, nb=nb, t=t, tb=GLA_TIME_BLOCK)
            yl, hl, cb = _lru(xb, gg, buf0, h_lru[:, None], lru_consts, nb=nb, t=t, tb=tb, t_last=tb - 1)
        ogs.append(og)
        yls.append(yl)
        states.append((jnp.swapaxes(st[..., :GLA_DK], 2, 3), hl[:, 0], cb))
    n_used = sum(o.shape[0] for o in ogs)
    tail = jnp.zeros((x.shape[0] - n_used, LRU_WIDTH), F32)
    og = jnp.concatenate(ogs + [tail], axis=0)
    yl = jnp.concatenate(yls + [tail], axis=0)
    return _mix_out(og, yl, x, wo_a, wo_b, g, beta, tm=tm), states


RWKV_STREAMS = 5


def _rwkv_in_kernel(*refs, has_vres, tm, long_seqs, short_seqs):
    if has_vres:
        (x_ref, halo_ref, sl_ref, ov_ref, mu_ref, wr_ref, wk_ref, wv_ref, w1_ref, w2_ref, a1_ref, a2_ref, g1_ref, g2_ref,
         w0_ref, a0_ref, vf_ref, v1_ref, v2_ref, v0_ref, p_ref, g_ref, xs_ref) = refs
    else:
        (x_ref, halo_ref, sl_ref, ov_ref, mu_ref, wr_ref, wk_ref, wv_ref, w1_ref, w2_ref, a1_ref, a2_ref, g1_ref, g2_ref,
         w0_ref, a0_ref, p_ref, g_ref, xs_ref) = refs
    d = D_MODEL
    x = x_ref[...]
    r0 = pl.program_id(0) * tm
    xs_ref[0:SUBLANES] = halo_ref[...]
    xs_ref[SUBLANES:SUBLANES + tm] = x
    prev = xs_ref[pl.ds(SUBLANES - 1, tm), :]
    rows = r0 + lax.broadcasted_iota(jnp.int32, (tm, 1), 0)
    off, cnt, length = long_seqs
    b = (jnp.maximum(r0 - off, 0) + (length - 1)) // length
    hit = jnp.logical_and(rows == off + b * length, b < cnt)
    prev = jnp.where(hit, sl_ref[pl.ds(jnp.minimum(b, cnt - 1), 1), :], prev)
    off, cnt, length = short_seqs
    rel = rows - off
    first = jnp.logical_and(jnp.logical_and(rel >= 0, rel < cnt * length), jnp.bitwise_and(rel, length - 1) == 0)
    prev = jnp.where(first, ov_ref[...], prev)
    xx = prev - x
    mix = lambda i: (x + xx * mu_ref[i:i + 1, :]).astype(BF16)
    xr, xw, xk, xv, xa, xg = (mix(i) for i in range(6))
    p_ref[:, 0:d] = _dot(xr, wr_ref[...])
    p_ref[:, d:2 * d] = _dot(xk, wk_ref[...])
    v = _dot(xv, wv_ref[...])
    if has_vres:
        v = v + (vf_ref[...] - v) * _sigmoid(v0_ref[...] + _dot(_dot(xv, v1_ref[...]), v2_ref[...]))
    p_ref[:, 2 * d:3 * d] = v
    p_ref[:, 3 * d:4 * d] = w0_ref[...] + _dot(jnp.tanh(_dot(xw, w1_ref[...])), w2_ref[...])
    p_ref[:, 4 * d:5 * d] = a0_ref[...] + _dot(_dot(xa, a1_ref[...]), a2_ref[...])
    g_ref[...] = _dot(_sigmoid(_dot(xg, g1_ref[...])), g2_ref[...])


def _rwkv_in(x, shift_long, shift_short, consts, vres, *, tm, long_seqs, short_seqs):
    n, d = x.shape
    off, cnt, length = short_seqs
    assert length & (length - 1) == 0 and long_seqs[2] >= tm and tm % SUBLANES == 0
    tile0 = off // tm
    tiles = -(-(off + cnt * length) // tm) - tile0
    ov = jnp.pad(shift_short[:, None], ((0, 0), (0, length - 1), (0, 0))).reshape(cnt * length, d)
    ov = jnp.pad(ov, ((off - tile0 * tm, tiles * tm - (off - tile0 * tm) - cnt * length), (0, 0)))
    sl = jnp.pad(shift_long, ((0, -shift_long.shape[0] % SUBLANES), (0, 0)))
    row = lambda w: pl.BlockSpec((tm, w), lambda i: (i, 0))
    halo = pl.BlockSpec((SUBLANES, d), lambda i: (jnp.maximum(i * (tm // SUBLANES) - 1, 0), 0))
    ov_spec = pl.BlockSpec((tm, d), lambda i: (jnp.clip(i - tile0, 0, tiles - 1), 0))
    args = [x, x, sl, ov] + list(consts)
    specs = [row(d), halo, _const_spec(sl.shape), ov_spec] + [_const_spec(c.shape) for c in consts]
    if vres is not None:
        vf, vconsts = vres
        args += [vf] + list(vconsts)
        specs += [row(d)] + [_const_spec(c.shape) for c in vconsts]
    return pl.pallas_call(
        functools.partial(_rwkv_in_kernel, has_vres=vres is not None, tm=tm, long_seqs=long_seqs, short_seqs=short_seqs),
        out_shape=[jax.ShapeDtypeStruct((n, RWKV_STREAMS * d), F32), jax.ShapeDtypeStruct((n, d), F32)],
        grid=(n // tm,),
        in_specs=specs,
        out_specs=[row(RWKV_STREAMS * d), row(d)],
        scratch_shapes=[pltpu.VMEM((tm + SUBLANES, d), F32)],
        compiler_params=_params("parallel"),
        name="rwkv_in",
    )(*args)


WKV_SEQS = LANES // RWKV_HEADS


def _wkv_kernel(*refs, tt):
    tok_refs = refs[:WKV_SEQS]
    (s0_ref, kk_ref, ka_ref, rk_ref, lg_ref, lb_ref, y_ref, sout_ref,
     s_ref, p_ref, dec_ref, a_ref, b_ref, km_ref, bon_ref) = refs[WKV_SEQS:]
    t = pl.program_id(1)
    hk = RWKV_HEAD

    @pl.when(t == 0)
    def _init():
        s_ref[...] = s0_ref[...]

    def relayout(i, carry):
        for s in range(RWKV_STREAMS):
            rows = jnp.concatenate([tok_refs[q][i, s] for q in range(WKV_SEQS)], axis=0)
            p_ref[i, s] = rows.T
        return carry

    lax.fori_loop(0, tt, relayout, 0)

    r = p_ref[:, 0]
    k = p_ref[:, 1]
    v = p_ref[:, 2]
    dec_ref[...] = jnp.exp(-jnp.exp(-_softplus(-p_ref[:, 3]) - 0.5))
    ag = _sigmoid(p_ref[:, 4])
    kk = k * kk_ref[...][None]
    kk = kk / jnp.maximum(jnp.sqrt(jnp.sum(kk * kk, axis=1, keepdims=True)), 1e-12)
    km = k * (1.0 + (ag - 1.0) * ka_ref[...][None])
    km_ref[...] = km
    a_ref[...] = -kk
    b_ref[...] = kk * ag
    bon_ref[...] = jnp.sum(r * km * rk_ref[...][None], axis=1, keepdims=True) * v
    lg = lg_ref[...]
    lb = lb_ref[...]

    def step(i, carry):
        lanes = 4
        parts = [jnp.zeros((hk, LANES), F32) for _ in range(lanes)]
        for q in range(hk):
            parts[q % lanes] = parts[q % lanes] + s_ref[q] * a_ref[i, q:q + 1, :]
        sa = (parts[0] + parts[1]) + (parts[2] + parts[3])
        vv = p_ref[i, 2]
        parts = [jnp.zeros((hk, LANES), F32) for _ in range(lanes)]
        for q in range(hk):
            sn = s_ref[q] * dec_ref[i, q:q + 1, :] + (sa * b_ref[i, q:q + 1, :] + vv * km_ref[i, q:q + 1, :])
            s_ref[q] = sn
            parts[q % lanes] = parts[q % lanes] + sn * p_ref[i, 0, q:q + 1, :]
        y = (parts[0] + parts[1]) + (parts[2] + parts[3])
        mu = jnp.mean(y, axis=0, keepdims=True)
        dlt = y - mu
        var = jnp.mean(dlt * dlt, axis=0, keepdims=True)
        out = dlt * lax.rsqrt(var + RWKV_GN_EPS) * lg + lb + bon_ref[i]
        y_ref[i] = out.T.reshape(WKV_SEQS, RWKV_HEADS, hk)
        return carry

    lax.fori_loop(0, tt, step, 0)

    @pl.when(t == pl.num_programs(1) - 1)
    def _fin():
        sout_ref[...] = s_ref[...]


def _wkv(p, s0, lane_consts, *, off, nb, t, tt):
    hk = RWKV_HEAD
    l = nb * RWKV_HEADS
    assert off % tt == 0 and t % tt == 0 and nb % WKV_SEQS == 0
    p4 = p.reshape(p.shape[0], RWKV_STREAMS, RWKV_HEADS, hk)
    tok = lambda q: pl.BlockSpec((tt, RWKV_STREAMS, RWKV_HEADS, hk),
                                 lambda g, i: ((off + (g * WKV_SEQS + q) * t) // tt + i, 0, 0, 0))
    lane2 = pl.BlockSpec((hk, LANES), lambda g, i: (0, g))
    st = pl.BlockSpec((hk, hk, LANES), lambda g, i: (0, 0, g))
    seq = lambda: pltpu.VMEM((tt, hk, LANES), F32)
    return pl.pallas_call(
        functools.partial(_wkv_kernel, tt=tt),
        out_shape=[jax.ShapeDtypeStruct((t, nb, RWKV_HEADS, hk), F32), jax.ShapeDtypeStruct((hk, hk, l), F32)],
        grid=(l // LANES, t // tt),
        in_specs=[tok(q) for q in range(WKV_SEQS)] + [st] + [lane2] * 5,
        out_specs=[pl.BlockSpec((tt, WKV_SEQS, RWKV_HEADS, hk), lambda g, i: (i, g, 0, 0)), st],
        scratch_shapes=[pltpu.VMEM((hk, hk, LANES), F32), pltpu.VMEM((tt, RWKV_STREAMS, hk, LANES), F32),
                        seq(), seq(), seq(), seq(), seq()],
        compiler_params=_params("parallel", "arbitrary"),
        name="wkv",
    )(*([p4] * WKV_SEQS), s0, *lane_consts)


def _gate_out_kernel(y_ref, g_ref, x_ref, w_ref, lg_ref, lb_ref, o_ref):
    m = _dot(y_ref[...] * g_ref[...], w_ref[...])
    o_ref[...] = _layer_norm_rows(DN_ALPHA * x_ref[...] + m, lg_ref[...], lb_ref[...])


def _gate_out(y, g, x, w, lg, lb, *, tm):
    n, d = x.shape
    row = pl.BlockSpec((tm, d), lambda i: (i, 0))
    consts = [w, lg, lb]
    return pl.pallas_call(
        _gate_out_kernel,
        out_shape=jax.ShapeDtypeStruct((n, d), F32),
        grid=(n // tm,),
        in_specs=[row, row, row] + [_const_spec(c.shape) for c in consts],
        out_specs=row,
        compiler_params=_params("parallel"),
        name="gate_out",
    )(y, g, x, *consts)


def _pad_cols(w):
    return jnp.pad(w, ((0, 0), (0, LANES - w.shape[1]))).astype(BF16)


def _pad_rows(w):
    return jnp.pad(w, ((0, LANES - w.shape[0]), (0, 0))).astype(BF16)


def _rwkv_weights(mu, w_r, w_k, w_v, w_o, w0, w1, w2, a0, a1, a2, g1, g2, k_k, k_a, r_k, lnx_g, lnx_b, vres):
    consts = [jnp.pad(mu, ((0, SUBLANES - mu.shape[0]), (0, 0))), w_r.astype(BF16), w_k.astype(BF16), w_v.astype(BF16),
              _pad_cols(w1), _pad_rows(w2), _pad_cols(a1), _pad_rows(a2), _pad_cols(g1), _pad_rows(g2), w0[None], a0[None]]
    vconsts = None if vres is None else [_pad_cols(vres[1]), _pad_rows(vres[2]), vres[0][None]]
    per_head = [k_k.reshape(RWKV_HEADS, RWKV_HEAD).T, k_a.reshape(RWKV_HEADS, RWKV_HEAD).T, r_k.T,
                lnx_g.reshape(RWKV_HEADS, RWKV_HEAD).T, lnx_b.reshape(RWKV_HEADS, RWKV_HEAD).T]
    return consts, vconsts, per_head, w_o.astype(BF16)


def _rwkv_layer(x, groups, weights, v_first, g, beta, *, tm):
    consts, vconsts, per_head, w_o = weights
    n, d = x.shape
    (off_l, nb_l, t_l, _, _, shift_l), (off_s, nb_s, t_s, _, _, shift_s) = groups
    n_used = nb_l * t_l + nb_s * t_s
    vres = None if vconsts is None else (v_first, vconsts)
    p, gate = _rwkv_in(x, shift_l, shift_s, consts, vres, tm=tm,
                       long_seqs=(off_l, nb_l, t_l), short_seqs=(off_s, nb_s, t_s))
    ys, states = [], []
    for off, nb, t, tt, s, shift in groups:
        lanes = nb * RWKV_HEADS
        hk2 = RWKV_HEAD * RWKV_HEAD
        s0 = jnp.swapaxes(s, 2, 3).reshape(lanes, hk2).T.reshape(RWKV_HEAD, RWKV_HEAD, lanes)
        lane_consts = [jnp.tile(c, (1, nb)) for c in per_head]
        y, s_out = _wkv(p, s0, lane_consts, off=off, nb=nb, t=t, tt=tt)
        ys.append(jnp.swapaxes(y.reshape(t, nb, d), 0, 1).reshape(nb * t, d))
        s_new = jnp.swapaxes(s_out.reshape(hk2, lanes).T.reshape(nb, RWKV_HEADS, RWKV_HEAD, RWKV_HEAD), 2, 3)
        states.append((s_new, x[off:off + nb * t].reshape(nb, t, d)[:, -1]))
    y = jnp.concatenate(ys + [jnp.zeros((n - n_used, d), F32)], axis=0)
    return _gate_out(y, gate, x, w_o, g, beta, tm=tm), states, p[:, 2 * d:3 * d]


TOKEN_TILE = 512
PEER_TOKEN_TILE = 1024
PROMPT_TIME_BLOCK = 688
WKV_TIME_BLOCK = 16


def kernel(x_prompt, x_sample, state_gla, state_lru_h, state_lru_conv, state_rwkv, state_rwkv_shift, meta_tokens, ln_g, ln_b, ev_w_in, ev_gla_w_gate, ev_gla_b_gate, ev_gla_norm, ev_conv_w, ev_conv_b, ev_lru_wa, ev_lru_ba, ev_lru_wx, ev_lru_bx, ev_lru_lambda, ev_w_out, od_mu, od_w_r, od_w_k, od_w_v, od_w_o, od_w0, od_w1, od_w2, od_a0, od_a1, od_a2, od_v0, od_v1, od_v2, od_g1, od_g2, od_k_k, od_k_a, od_r_k, od_lnx_g, od_lnx_b, peer_w_q, peer_keys, peer_u, peer_v):
    bp, sp, d = x_prompt.shape
    bs, ss, _ = x_sample.shape
    tp = sp + N_META
    n_p, n_s = bp * tp, bs * ss
    n = -(-(n_p + n_s) // PEER_TOKEN_TILE) * PEER_TOKEN_TILE
    assert tp % PROMPT_TIME_BLOCK == 0 and tp % GLA_TIME_BLOCK == 0 and tp % WKV_TIME_BLOCK == 0 and n % TOKEN_TILE == 0

    xp = jnp.concatenate([jnp.broadcast_to(meta_tokens[None], (bp, N_META, d)), x_prompt], axis=1)
    x = jnp.concatenate([xp.reshape(n_p, d), x_sample.reshape(n_s, d), jnp.zeros((n - n_p - n_s, d), F32)], axis=0)

    n_pairs = DEPTH // 2
    zeros = lambda *s: jnp.zeros(s, F32)
    outs = {k: [] for k in ("p_gla", "p_h", "p_conv", "p_rwkv", "p_shift", "s_gla", "s_h", "s_conv", "s_rwkv", "s_shift")}
    v_first = None
    for layer in range(DEPTH):
        j = layer // 2
        g, beta = ln_g[layer, 0][None], ln_b[layer, 0][None]
        if layer % 2 == 0:
            weights = _even_weights(ev_w_in[j], ev_gla_w_gate[j], ev_gla_b_gate[j], ev_gla_norm[j], ev_conv_w[j], ev_conv_b[j],
                                    ev_lru_wa[j], ev_lru_ba[j], ev_lru_wx[j], ev_lru_bx[j], ev_lru_lambda[j], ev_w_out[j])
            groups = [(0, bp, tp, PROMPT_TIME_BLOCK, zeros(bp, GLA_HEADS, GLA_DK, GLA_DV), zeros(bp, LRU_WIDTH),
                       zeros(bp, CONV_W - 1, LRU_WIDTH)),
                      (n_p, bs, ss, None, state_gla[j], state_lru_h[j], state_lru_conv[j])]
            x, states = _even_layer(x, groups, weights, g, beta, tm=TOKEN_TILE)
            for pre, st in zip("ps", states):
                outs[pre + "_gla"].append(st[0])
                outs[pre + "_h"].append(st[1])
                outs[pre + "_conv"].append(st[2])
        else:
            vres = None if j == 0 else (od_v0[j - 1], od_v1[j - 1], od_v2[j - 1])
            weights = _rwkv_weights(od_mu[j], od_w_r[j], od_w_k[j], od_w_v[j], od_w_o[j], od_w0[j], od_w1[j], od_w2[j],
                                    od_a0[j], od_a1[j], od_a2[j], od_g1[j], od_g2[j], od_k_k[j], od_k_a[j], od_r_k[j],
                                    od_lnx_g[j], od_lnx_b[j], vres)
            groups = [(0, bp, tp, WKV_TIME_BLOCK, zeros(bp, RWKV_HEADS, RWKV_HEAD, RWKV_HEAD), zeros(bp, d)),
                      (n_p, bs, ss, ss, state_rwkv[j], state_rwkv_shift[j])]
            x, states, v = _rwkv_layer(x, groups, weights, v_first, g, beta, tm=TOKEN_TILE)
            if v_first is None:
                v_first = v
            for pre, st in zip("ps", states):
                outs[pre + "_rwkv"].append(st[0])
                outs[pre + "_shift"].append(st[1])
        x = _peer_layer(x, peer_w_q[layer].T.astype(BF16), peer_keys[layer, :, 0].astype(BF16),
                        peer_keys[layer, :, 1].astype(BF16), peer_u[layer].astype(BF16), peer_v[layer].astype(BF16),
                        ln_g[layer, 1][None], ln_b[layer, 1][None], tn=PEER_TOKEN_TILE)

    y_prompt = x[:n_p].reshape(bp, tp, d)[:, N_META:]
    y_sample = x[n_p:n_p + n_s].reshape(bs, ss, d)
    st = {k: jnp.stack(v) for k, v in outs.items()}
    return (y_prompt, y_sample, st["p_gla"], st["p_h"], st["p_conv"], st["p_rwkv"], st["p_shift"],
            st["s_gla"], st["s_h"], st["s_conv"], st["s_rwkv"], st["s_shift"])
```

```python
import functools

import jax
import jax.numpy as jnp
from jax import lax
from jax.experimental import pallas as pl
from jax.experimental.pallas import tpu as pltpu

F32 = jnp.float32
BF16 = jnp.bfloat16

D_MODEL = 1024
DEPTH = 4
N_META = 16
GLA_HEADS = 4
GLA_DK = 64
GLA_DV = 128
GLA_RANK = 16
GLA_GATE_NORM = 16.0
GLA_CHUNK = 16
LRU_WIDTH = 512
LRU_BLOCKS = 8
CONV_W = 4
LRU_C = 8.0
RWKV_HEAD = 64
RWKV_HEADS = D_MODEL // RWKV_HEAD
RWKV_GN_EPS = 64e-5
PEER_HEADS = 8
PEER_NKEYS = 128
PEER_DKEY = 256
PEER_HALF = PEER_DKEY // 2
PEER_TOPK = 16
DN_ALPHA = float((2 * DEPTH) ** 0.25)
LN_EPS = 1e-5

LANES = 128
SUBLANES = 8
VMEM_LIMIT = 56 * 1024 * 1024


def _dot(a, b):
    return jnp.dot(a.astype(BF16), b.astype(BF16), preferred_element_type=F32)


def _dot_nt(a, b):
    return lax.dot_general(a.astype(BF16), b.astype(BF16), (((1,), (1,)), ((), ())), preferred_element_type=F32)


def _dot_tn(a, b):
    return lax.dot_general(a.astype(BF16), b.astype(BF16), (((0,), (0,)), ((), ())), preferred_element_type=F32)


def _layer_norm_rows(z, g, b):
    mu = jnp.mean(z, axis=-1, keepdims=True)
    d = z - mu
    var = jnp.mean(d * d, axis=-1, keepdims=True)
    return d * lax.rsqrt(var + LN_EPS) * g + b


def _sigmoid(x):
    return 1.0 / (1.0 + jnp.exp(-x))


def _softplus(x):
    return jnp.maximum(x, 0.0) + jnp.log1p(jnp.exp(-jnp.abs(x)))


def _gelu_tanh(x):
    c0 = 0.7978845608028654
    c1 = c0 * 0.044715
    one = jnp.asarray(1.0, x.dtype)
    return (0.5 * x) * (one + jnp.tanh(x * (c0 + c1 * (x * x))))


def _silu(x):
    return x * _sigmoid(x)


def _params(*sem):
    return pltpu.CompilerParams(dimension_semantics=sem, vmem_limit_bytes=VMEM_LIMIT)


def _const_spec(shape):
    nd = len(shape)
    return pl.BlockSpec(shape, lambda *_: (0,) * nd)


PEER_CAND_ROWS = 80
PEER_PACK = 2 * SUBLANES
PEER_SELECT_LANES = 4 * LANES


def _peer_cand_index():
    rows = [r1 * PEER_TOPK for r1 in range(16)]
    for j in range(1, 8):
        rows += [r1 * PEER_TOPK + j for r1 in range(8)]
    rows += list(range(8, 16))
    return jnp.broadcast_to(jnp.asarray(rows, F32)[:, None], (PEER_CAND_ROWS, PEER_SELECT_LANES))


PEER_CODE_STEP = 2.0 ** 116
PEER_CODED_BELOW = -(2.0 ** 119)


def _rank_code(r):
    return -PEER_CODE_STEP * (PEER_TOPK + r)


def _decode_rank(s):
    return jnp.where(s < PEER_CODED_BELOW, s * (-1.0 / PEER_CODE_STEP) - PEER_TOPK, float(PEER_TOPK))


def _count_coded(s):
    return jnp.sum(jnp.where(s < PEER_CODED_BELOW, 1.0, 0.0), axis=0, keepdims=True)


def _top16(scores, key_iota):
    row16 = lax.broadcasted_iota(jnp.int32, (PEER_TOPK, scores[0].shape[1]), 0)
    vals = [jnp.zeros((PEER_TOPK, s.shape[1]), F32) for s in scores]
    scores = list(scores)
    for r in range(PEER_TOPK):
        tops = [jnp.max(s, axis=0, keepdims=True) for s in scores]
        hits = [s == m for s, m in zip(scores, tops)]
        if key_iota is not None:
            firsts = [jnp.min(jnp.where(hit, key_iota, float(PEER_NKEYS)), axis=0, keepdims=True) for hit in hits]
            hits = [key_iota == first for first in firsts]
        scores = [jnp.where(hit, _rank_code(r), s) for hit, s in zip(hits, scores)]
        vals = [jnp.where(row16 == r, m, v) for m, v in zip(tops, vals)]
    return list(zip(vals, scores))


def _peer_candidates(a, b):
    blocks = [a + b[0:1]]
    for j in range(1, 8):
        blocks.append(a[0:8] + b[j:j + 1])
    blocks.append(a[0:1] + b[8:16])
    return jnp.concatenate(blocks, axis=0)


def _select16(c, cidx):
    for _ in range(PEER_TOPK):
        m = jnp.max(c, axis=0, keepdims=True)
        hit = c == m
        if cidx is not None:
            first = jnp.min(jnp.where(hit, cidx, 1e9), axis=0, keepdims=True)
            hit = cidx == first
        c = jnp.where(hit, _rank_code(0), c)
    return c


def _peer_gate_tiles(s1, s2, a, b, s1c, s2c, c, cc):
    sel = jnp.where(cc < PEER_CODED_BELOW, 1.0, 0.0)
    z = jnp.sum(sel * jnp.exp(c - c[0:1]), axis=0, keepdims=True)
    j_lo = sel[0:8]
    for j in range(1, 8):
        j_lo = j_lo + sel[8 + 8 * j:16 + 8 * j]
    extra = jnp.sum(sel[72:80], axis=0, keepdims=True)
    row8 = lax.broadcasted_iota(jnp.int32, j_lo.shape, 0)
    j_lo = j_lo + jnp.where(row8 == 0, extra, 0.0)
    jt = jnp.concatenate([j_lo, sel[8:16]], axis=0)
    r1 = _decode_rank(s1c)
    h1 = jnp.zeros_like(r1)
    for r in range(PEER_TOPK):
        h1 = jnp.where(r1 == float(r), jt[r:r + 1], h1)
    return h1, jnp.exp(s1 - a[0:1]) / z, _decode_rank(s2c), jnp.exp(s2 - b[0:1])


def _peer_kernel(x_ref, wqt_ref, k1_ref, k2_ref, cidx_ref, u_ref, vt_ref, g_ref, b_ref, o_ref,
                 xt_ref, acc_ref, h1_ref, c1_ref, r2_ref, e2_ref, s_ref, hd0_ref, hd1_ref, coef_ref, *, tn, te):
    j = pl.program_id(1)
    nj = pl.num_programs(1)

    @pl.when(j == 0)
    def _select():
        xt_ref[...] = x_ref[...].T.astype(BF16)
        acc_ref[...] = jnp.zeros_like(acc_ref)
        hd1_ref[...] = jnp.zeros_like(hd1_ref)
        coef_ref[0:te] = jnp.zeros((te, tn), BF16)
        key_iota = lax.broadcasted_iota(jnp.int32, (PEER_NKEYS, PEER_SELECT_LANES), 0).astype(F32)

        def head(h, carry):
            q = _dot(wqt_ref[pl.ds(pl.multiple_of(h * PEER_DKEY, PEER_DKEY), PEER_DKEY), :], xt_ref[...])
            mu = jnp.mean(q, axis=0, keepdims=True)
            d = q - mu
            qn = d * lax.rsqrt(jnp.mean(d * d, axis=0, keepdims=True) + LN_EPS)
            s_ref[0] = _dot(k1_ref[h], qn[0:PEER_HALF])
            s_ref[1] = _dot(k2_ref[h], qn[PEER_HALF:PEER_DKEY])

            def chunk(ci, carry2):
                ls = pl.ds(pl.multiple_of(ci * PEER_SELECT_LANES, PEER_SELECT_LANES), PEER_SELECT_LANES)
                s1 = s_ref[0, :, ls]
                s2 = s_ref[1, :, ls]

                def put(tiles):
                    h1_ref[h, :, ls] = tiles[0]
                    c1_ref[h, :, ls] = tiles[1]
                    r2_ref[h, :, ls] = tiles[2].astype(BF16)
                    e2_ref[h, :, ls] = tiles[3].astype(BF16)

                (a, s1c), (b, s2c) = _top16([s1, s2], None)
                c = _peer_candidates(a, b)
                cc = _select16(c, None)
                put(_peer_gate_tiles(s1, s2, a, b, s1c, s2c, c, cc))
                miscount = (jnp.abs(_count_coded(s1c) - PEER_TOPK) + jnp.abs(_count_coded(s2c) - PEER_TOPK)
                            + jnp.abs(_count_coded(cc) - PEER_TOPK))

                @pl.when(jnp.max(miscount) > 0.0)
                def _with_ties():
                    (a, s1c), (b, s2c) = _top16([s1, s2], key_iota)
                    c = _peer_candidates(a, b)
                    cc = _select16(c, cidx_ref[...])
                    put(_peer_gate_tiles(s1, s2, a, b, s1c, s2c, c, cc))

                return carry2

            lax.fori_loop(0, tn // PEER_SELECT_LANES, chunk, 0)
            return carry

        lax.fori_loop(0, PEER_HEADS, head, 0)

    groups = PEER_NKEYS // PEER_PACK
    last_key = PEER_NKEYS - 1
    zero_pack = jnp.zeros((PEER_PACK, LANES), F32)

    def gate_rows(tile):
        rows = []
        for ii in range(te // PEER_NKEYS):
            i1 = jnp.clip(tile * (te // PEER_NKEYS) + ii, 0, last_key)
            rows.append(([h1_ref[h, pl.ds(i1, 1), :] for h in range(PEER_HEADS)],
                         [c1_ref[h, pl.ds(i1, 1), :] for h in range(PEER_HEADS)]))
        return rows

    def gate_stage(rows, hd_ref, coef_rows, lane_groups):
        nk = len(rows)
        for c in lane_groups:
            ls = slice(c * LANES, (c + 1) * LANES)
            gates = [[jnp.zeros((PEER_PACK, LANES), BF16) for _ in range(groups)] for _ in range(nk)]
            for h in range(PEER_HEADS):
                bounds = [(rows[ii][0][h][:, ls] + zero_pack).astype(BF16) for ii in range(nk)]
                scales = [(rows[ii][1][h][:, ls] + zero_pack).astype(BF16) for ii in range(nk)]
                for gi in range(groups):
                    krows = slice(gi * PEER_PACK, (gi + 1) * PEER_PACK)
                    rank = r2_ref[h, krows, ls]
                    fac = e2_ref[h, krows, ls]
                    for ii in range(nk):
                        gates[ii][gi] = gates[ii][gi] + jnp.where(rank < bounds[ii], fac * scales[ii], jnp.zeros((), BF16))
            for ii in range(nk):
                for gi in range(groups):
                    erows = slice(ii * PEER_NKEYS + gi * PEER_PACK, ii * PEER_NKEYS + (gi + 1) * PEER_PACK)
                    crows = slice(coef_rows + erows.start, coef_rows + erows.stop)
                    coef_ref[crows, ls] = gates[ii][gi] * _gelu_tanh(hd_ref[erows, ls].astype(BF16))

    piece = 2 * LANES
    per = piece // LANES
    rows_odd = gate_rows(2 * j - 1)
    rows_even = gate_rows(2 * j)
    pieces = tn // piece
    lanes_of = lambda c: slice(c * piece, (c + 1) * piece)

    def down_project(c):
        acc_ref[:, lanes_of(c)] += jnp.dot(vt_ref[0], coef_ref[:, lanes_of(c)], preferred_element_type=F32)

    for c in range(pieces):
        gate_stage(rows_odd, hd1_ref, te, range(c * per, (c + 1) * per))
        hd0_ref[:, lanes_of(c)] = _dot(u_ref[0:te, :], xt_ref[:, lanes_of(c)])
        hd1_ref[:, lanes_of(c)] = _dot(u_ref[te:2 * te, :], xt_ref[:, lanes_of(c)])
        if c > 0:
            down_project(c - 1)
    down_project(pieces - 1)
    for c in range(pieces):
        gate_stage(rows_even, hd0_ref, 0, range(c * per, (c + 1) * per))

    @pl.when(j == nj - 1)
    def _finish():
        y = acc_ref[...].T
        o_ref[...] = _layer_norm_rows(DN_ALPHA * x_ref[...] + y, g_ref[...], b_ref[...])


def _peer_layer(x, wqt, k1, k2, u, v, g, b, *, tn=1024, te=256):
    n, d = x.shape
    ne = u.shape[0]
    assert ne % (2 * te) == 0 and n % tn == 0
    pairs = ne // (2 * te)
    steps = pairs + 1
    vt = jnp.swapaxes(v.reshape(pairs, 2 * te, d), 1, 2)
    kern = functools.partial(_peer_kernel, tn=tn, te=te)
    one = pl.Buffered(1)
    return pl.pallas_call(
        kern,
        out_shape=jax.ShapeDtypeStruct((n, d), F32),
        grid=(n // tn, steps),
        in_specs=[
            pl.BlockSpec((tn, d), lambda i, j: (i, 0), pipeline_mode=one),
            pl.BlockSpec(wqt.shape, lambda i, j: (0, 0), pipeline_mode=one),
            pl.BlockSpec(k1.shape, lambda i, j: (0, 0, 0), pipeline_mode=one),
            pl.BlockSpec(k2.shape, lambda i, j: (0, 0, 0), pipeline_mode=one),
            pl.BlockSpec((PEER_CAND_ROWS, PEER_SELECT_LANES), lambda i, j: (0, 0), pipeline_mode=one),
            pl.BlockSpec((2 * te, d), lambda i, j: (jnp.minimum(j, pairs - 1), 0)),
            pl.BlockSpec((1, d, 2 * te), lambda i, j: (jnp.maximum(j - 1, 0), 0, 0)),
            pl.BlockSpec((1, d), lambda i, j: (0, 0), pipeline_mode=one),
            pl.BlockSpec((1, d), lambda i, j: (0, 0), pipeline_mode=one),
        ],
        out_specs=pl.BlockSpec((tn, d), lambda i, j: (i, 0), pipeline_mode=one),
        scratch_shapes=[
            pltpu.VMEM((d, tn), BF16),
            pltpu.VMEM((d, tn), F32),
            pltpu.VMEM((PEER_HEADS, PEER_NKEYS, tn), F32),
            pltpu.VMEM((PEER_HEADS, PEER_NKEYS, tn), F32),
            pltpu.VMEM((PEER_HEADS, PEER_NKEYS, tn), BF16),
            pltpu.VMEM((PEER_HEADS, PEER_NKEYS, tn), BF16),
            pltpu.VMEM((2, PEER_NKEYS, tn), F32),
            pltpu.VMEM((te, tn), F32),
            pltpu.VMEM((te, tn), F32),
            pltpu.VMEM((2 * te, tn), BF16),
        ],
        compiler_params=_params("parallel", "arbitrary"),
        name="peer",
    )(x, wqt, k1, k2, _peer_cand_index(), u, vt, g, b)


GLA_PAD = GLA_HEADS * LANES


def _even_in_kernel(x_ref, wq_ref, wk_ref, wv_ref, wr_ref, wxb_ref, wgb_ref, wlr_ref, wg_ref, bg_ref,
                    qkl_ref, v_ref, rs_ref, xb_ref, gg_ref):
    xb16 = x_ref[...].astype(BF16)
    qkl_ref[:, 0:GLA_PAD] = _dot(xb16, wq_ref[...]) * (GLA_DK ** -0.5)
    qkl_ref[:, GLA_PAD:2 * GLA_PAD] = _dot(xb16, wk_ref[...])
    glr = _dot(xb16, wlr_ref[...])
    z = _dot(glr, wg_ref[...]) + bg_ref[...]
    qkl_ref[:, 2 * GLA_PAD:3 * GLA_PAD] = -_softplus(-z) * (1.0 / GLA_GATE_NORM)
    v_ref[...] = _dot(xb16, wv_ref[...])
    rs_ref[...] = _silu(_dot(xb16, wr_ref[...]))
    xb_ref[...] = _dot(xb16, wxb_ref[...])
    gg_ref[...] = _gelu_tanh(_dot(xb16, wgb_ref[...]))


def _even_in(x, consts, *, tm=512):
    n, d = x.shape
    row = lambda w: pl.BlockSpec((tm, w), lambda i: (i, 0))
    return pl.pallas_call(
        _even_in_kernel,
        out_shape=[jax.ShapeDtypeStruct((n, 3 * GLA_PAD), F32)] + [jax.ShapeDtypeStruct((n, LRU_WIDTH), F32)] * 4,
        grid=(n // tm,),
        in_specs=[row(d)] + [_const_spec(c.shape) for c in consts],
        out_specs=[row(3 * GLA_PAD)] + [row(LRU_WIDTH)] * 4,
        compiler_params=_params("parallel"),
        name="even_in",
    )(x, *consts)


GLA_SEQS_PER_STEP = 8
GLA_TIME_BLOCK = 3 * GLA_CHUNK


def _gla_kernel(*refs, tb, ns):
    qkl_refs, v_refs, rs_refs, s0_refs = (refs[i * ns:(i + 1) * ns] for i in range(4))
    tri_ref, gn_ref = refs[4 * ns:4 * ns + 2]
    o_refs = refs[4 * ns + 2:5 * ns + 2]
    sout_refs = refs[5 * ns + 2:6 * ns + 2]
    st_ref = refs[6 * ns + 2]
    t = pl.program_id(1)

    @pl.when(t == 0)
    def _init():
        for p in range(ns):
            st_ref[p] = s0_refs[p][0]

    tri = tri_ref[...]
    causal = tri > 0
    gn = gn_ref[...]

    def chunk(c, carry):
        rows = pl.ds(pl.multiple_of(c * GLA_CHUNK, GLA_CHUNK), GLA_CHUNK)
        seqs = range(ns)
        pairs = [(p, h) for p in seqs for h in range(GLA_HEADS)]
        sl = lambda h: slice(h * LANES, (h + 1) * LANES)
        las = [qkl_refs[p][rows, 2 * GLA_PAD:3 * GLA_PAD] for p in seqs]
        his = [la.astype(BF16) for la in las]
        r1s = [la - hi.astype(F32) for la, hi in zip(las, his)]
        mids = [r1.astype(BF16) for r1 in r1s]
        los = [(r1 - mid.astype(F32)).astype(BF16) for r1, mid in zip(r1s, mids)]
        parts = [[jnp.dot(tri, part, preferred_element_type=F32) for part in (hi, mid, lo)]
                 for hi, mid, lo in zip(his, mids, los)]
        cums = [a + b + c3 for a, b, c3 in parts]
        lasts = [cum[GLA_CHUNK - 1:GLA_CHUNK] for cum in cums]
        qes = [qkl_refs[p][rows, 0:GLA_PAD] * jnp.exp(cums[p]) for p in seqs]
        ks = [qkl_refs[p][rows, GLA_PAD:2 * GLA_PAD] for p in seqs]
        kes = [ks[p] * jnp.exp(-cums[p]) for p in seqs]
        kls = [ks[p] * jnp.exp(lasts[p] - cums[p]) for p in seqs]
        decs = [jnp.exp(last) for last in lasts]
        vs = [v_refs[p][rows, :] for p in seqs]
        sts = [st_ref[p, h] for p, h in pairs]
        atts = [_dot_nt(qes[p][:, sl(h)], kes[p][:, sl(h)]) for p, h in pairs]
        inters = [_dot_nt(qes[p][:, sl(h)], st) for (p, h), st in zip(pairs, sts)]
        upds = [_dot_tn(vs[p][:, sl(h)], kls[p][:, sl(h)]) for p, h in pairs]
        intras = [_dot(jnp.where(causal, att, 0.0), vs[p][:, sl(h)]) for (p, h), att in zip(pairs, atts)]
        for i, (p, h) in enumerate(pairs):
            st_ref[p, h] = sts[i] * decs[p][:, sl(h)] + upds[i]
            o = intras[i] + inters[i]
            on = o * lax.rsqrt(jnp.mean(o * o, axis=-1, keepdims=True) + LN_EPS) * gn
            o_refs[p][rows, sl(h)] = on * rs_refs[p][rows, sl(h)]
        return carry

    lax.fori_loop(0, tb // GLA_CHUNK, chunk, 0)

    @pl.when(t == pl.num_programs(1) - 1)
    def _fin():
        for p in range(ns):
            sout_refs[p][0] = st_ref[p]


def _gla(qkl, v, rs, s0t, gn, *, nb, t, tb):
    nt = t // tb
    ns = GLA_SEQS_PER_STEP
    assert nb % ns == 0
    per = nb // ns
    w = GLA_HEADS * GLA_DV
    tri = jnp.tril(jnp.ones((GLA_CHUNK, GLA_CHUNK), BF16))
    row_in = lambda width, p: pl.BlockSpec((tb, width), lambda g, i: ((p * per + g) * nt + i, 0))
    st_in = lambda p: pl.BlockSpec((1, GLA_HEADS, GLA_DV, LANES), lambda g, i: (p * per + g, 0, 0, 0))
    row_out = pl.BlockSpec((tb, w), lambda g, i: (g * nt + i, 0))
    st_out = pl.BlockSpec((1, GLA_HEADS, GLA_DV, LANES), lambda g, i: (g, 0, 0, 0))
    slots = range(ns)
    outs = pl.pallas_call(
        functools.partial(_gla_kernel, tb=tb, ns=ns),
        out_shape=[jax.ShapeDtypeStruct((per * t, w), F32)] * ns
        + [jax.ShapeDtypeStruct((per,) + s0t.shape[1:], F32)] * ns,
        grid=(per, nt),
        in_specs=[row_in(3 * GLA_PAD, p) for p in slots] + [row_in(w, p) for p in slots] + [row_in(w, p) for p in slots]
        + [st_in(p) for p in slots] + [_const_spec(tri.shape), _const_spec(gn.shape)],
        out_specs=[row_out] * ns + [st_out] * ns,
        scratch_shapes=[pltpu.VMEM((ns, GLA_HEADS, GLA_DV, LANES), F32)],
        compiler_params=_params("parallel", "arbitrary"),
        name="gla",
    )(*([qkl] * ns + [v] * ns + [rs] * ns + [s0t] * ns + [tri, gn]))
    return jnp.concatenate(outs[:ns], axis=0), jnp.concatenate(outs[ns:], axis=0)


def _lru_kernel(xb_ref, gg_ref, buf0_ref, h0_ref, cw_ref, cb_ref, wa_ref, ba_ref, wx_ref, bx_ref, lam_ref,
                y_ref, hlast_ref, cbuf_ref, xs_ref, a_ref, b_ref, hs_ref, hcar_ref, *, tb, t_last):
    t = pl.program_id(1)
    halo = SUBLANES

    @pl.when(t == 0)
    def _init():
        xs_ref[0:halo] = buf0_ref[0]
        hcar_ref[...] = h0_ref[0]

    xs_ref[halo:halo + tb] = xb_ref[...]
    xc = cb_ref[...] + sum(xs_ref[pl.ds(halo - (CONV_W - 1) + i, tb), :] * cw_ref[i:i + 1, :] for i in range(CONV_W))
    ga = _sigmoid(_dot(xc, wa_ref[...]) + ba_ref[...])
    gx = _sigmoid(_dot(xc, wx_ref[...]) + bx_ref[...])
    log_at = ga * (-LRU_C * _softplus(-lam_ref[...]))
    a = jnp.exp(log_at)
    a_ref[...] = a
    b_ref[...] = jnp.sqrt(-jnp.tanh(log_at) * (a * a + 1.0)) * (gx * xc)

    def step(i, h):
        h = a_ref[pl.ds(i, 1), :] * h + b_ref[pl.ds(i, 1), :]
        hs_ref[pl.ds(i, 1), :] = h
        return h

    hcar_ref[...] = lax.fori_loop(0, tb, step, hcar_ref[...], unroll=8)
    y_ref[...] = hs_ref[...] * gg_ref[...]

    @pl.when(t == pl.num_programs(1) - 1)
    def _fin():
        hlast_ref[0] = hs_ref[t_last:t_last + 1, :]
        cbuf_ref[0] = xs_ref[halo + t_last - (CONV_W - 2):halo + t_last + 1, :]

    xs_ref[0:halo] = xs_ref[tb:tb + halo]


def _lru(xb, gg, buf0, h0, consts, *, nb, t, tb, t_last):
    nt = t // tb
    w = LRU_WIDTH
    row = pl.BlockSpec((tb, w), lambda b, i: (b * nt + i, 0))
    per_b = lambda r: pl.BlockSpec((1, r, w), lambda b, i: (b, 0, 0))
    return pl.pallas_call(
        functools.partial(_lru_kernel, tb=tb, t_last=t_last),
        out_shape=[jax.ShapeDtypeStruct((nb * t, w), F32), jax.ShapeDtypeStruct((nb, 1, w), F32),
                   jax.ShapeDtypeStruct((nb, CONV_W - 1, w), F32)],
        grid=(nb, nt),
        in_specs=[row, row, per_b(SUBLANES), per_b(1)] + [_const_spec(c.shape) for c in consts],
        out_specs=[row, per_b(1), per_b(CONV_W - 1)],
        scratch_shapes=[pltpu.VMEM((tb + 2 * SUBLANES, w), F32), pltpu.VMEM((tb, w), F32), pltpu.VMEM((tb, w), F32),
                        pltpu.VMEM((tb, w), F32), pltpu.VMEM((1, w), F32)],
        compiler_params=_params("parallel", "arbitrary"),
        name="lru",
    )(xb, gg, buf0, h0, *consts)


def _mix_out_kernel(a_ref, b_ref, x_ref, wa_ref, wb_ref, g_ref, beta_ref, o_ref):
    m = _dot(a_ref[...], wa_ref[...]) + _dot(b_ref[...], wb_ref[...])
    o_ref[...] = _layer_norm_rows(DN_ALPHA * x_ref[...] + m, g_ref[...], beta_ref[...])


def _mix_out(a, b, x, wa, wb, g, beta, *, tm=512):
    n, d = x.shape
    row = lambda w: pl.BlockSpec((tm, w), lambda i: (i, 0))
    consts = [wa, wb, g, beta]
    return pl.pallas_call(
        _mix_out_kernel,
        out_shape=jax.ShapeDtypeStruct((n, d), F32),
        grid=(n // tm,),
        in_specs=[row(a.shape[1]), row(b.shape[1]), row(d)] + [_const_spec(c.shape) for c in consts],
        out_specs=row(d),
        compiler_params=_params("parallel"),
        name="mix_out",
    )(a, b, x, *consts)


def _pad_heads(w, heads, width):
    lead = w.shape[:-1]
    w = w.reshape(lead + (heads, width))
    w = jnp.pad(w, [(0, 0)] * len(lead) + [(0, 0), (0, LANES - width)])
    return w.reshape(lead + (heads * LANES,))


def _even_weights(w_in, w_gate, b_gate, gla_norm, conv_w, conv_b, lru_wa, lru_ba, lru_wx, lru_bx, lru_lambda, w_out):
    gk = GLA_HEADS * GLA_DK
    gv = GLA_HEADS * GLA_DV
    o = 0
    cols = {}
    for name, width in (("q", gk), ("k", gk), ("v", gv), ("lr", GLA_RANK), ("r", gv), ("xb", LRU_WIDTH), ("gb", LRU_WIDTH)):
        cols[name] = w_in[:, o:o + width]
        o += width
    wlr = jnp.pad(cols["lr"], ((0, 0), (0, LANES - GLA_RANK)))
    wg = jnp.pad(_pad_heads(w_gate, GLA_HEADS, GLA_DK), ((0, LANES - GLA_RANK), (0, 0)))
    bg = _pad_heads(b_gate[None], GLA_HEADS, GLA_DK)
    in_consts = [_pad_heads(cols["q"], GLA_HEADS, GLA_DK).astype(BF16), _pad_heads(cols["k"], GLA_HEADS, GLA_DK).astype(BF16),
                 cols["v"].astype(BF16), cols["r"].astype(BF16), cols["xb"].astype(BF16), cols["gb"].astype(BF16),
                 wlr.astype(BF16), wg.astype(BF16), bg]
    eye = jnp.eye(LRU_BLOCKS, dtype=F32)
    bd = lambda w: (eye[:, None, :, None] * w[:, :, None, :]).reshape(LRU_WIDTH, LRU_WIDTH).astype(BF16)
    lru_consts = [conv_w, conv_b[None], bd(lru_wa), lru_ba[None], bd(lru_wx), lru_bx[None], lru_lambda[None]]
    return in_consts, gla_norm[None], lru_consts, w_out[:gv].astype(BF16), w_out[gv:].astype(BF16)


def _seq_rows(a, off, nb, t, t_pad):
    a = a[off:off + nb * t].reshape(nb, t, a.shape[1])
    return jnp.pad(a, ((0, 0), (0, t_pad - t), (0, 0))).reshape(nb * t_pad, a.shape[2])


def _unpad_rows(a, nb, t, t_pad):
    return a.reshape(nb, t_pad, a.shape[1])[:, :t].reshape(nb * t, a.shape[1])


def _even_layer(x, groups, weights, g, beta, *, tm):
    in_consts, gn, lru_consts, wo_a, wo_b = weights
    qkl, v, rs, xb, gg = _even_in(x, in_consts, tm=tm)
    ogs, yls, states = [], [], []
    for off, nb, t, tb, s_gla, h_lru, cbuf in groups:
        s0t = jnp.pad(jnp.swapaxes(s_gla, 2, 3), ((0, 0), (0, 0), (0, 0), (0, LANES - GLA_DK)))
        buf0 = jnp.pad(cbuf, ((0, 0), (SUBLANES - (CONV_W - 1), 0), (0, 0)))
        if tb is None:
            tg = GLA_CHUNK
            tl = SUBLANES
            og, st = _gla(_seq_rows(qkl, off, nb, t, tg), _seq_rows(v, off, nb, t, tg), _seq_rows(rs, off, nb, t, tg),
                          s0t, gn, nb=nb, t=tg, tb=tg)
            yl, hl, cb = _lru(_seq_rows(xb, off, nb, t, tl), _seq_rows(gg, off, nb, t, tl), buf0, h_lru[:, None],
                              lru_consts, nb=nb, t=tl, tb=tl, t_last=t - 1)
            og = _unpad_rows(og, nb, t, tg)
            yl = _unpad_rows(yl, nb, t, tl)
        else:
            og, st = _gla(qkl, v, rs, s0t, gn, nb=nb, t=t, tb=GLA_TIME_BLOCK)
            yl, hl, cb = _lru(xb, gg, buf0, h_lru[:, None], lru_consts, nb=nb, t=t, tb=tb, t_last=tb - 1)
        ogs.append(og)
        yls.append(yl)
        states.append((jnp.swapaxes(st[..., :GLA_DK], 2, 3), hl[:, 0], cb))
    n_used = sum(o.shape[0] for o in ogs)
    tail = jnp.zeros((x.shape[0] - n_used, LRU_WIDTH), F32)
    og = jnp.concatenate(ogs + [tail], axis=0)
    yl = jnp.concatenate(yls + [tail], axis=0)
    return _mix_out(og, yl, x, wo_a, wo_b, g, beta, tm=tm), states


RWKV_STREAMS = 5


def _rwkv_in_kernel(*refs, has_vres, tm, long_seqs, short_seqs):
    if has_vres:
        (x_ref, halo_ref, sl_ref, ov_ref, mu_ref, wr_ref, wk_ref, wv_ref, w1_ref, w2_ref, a1_ref, a2_ref, g1_ref, g2_ref,
         w0_ref, a0_ref, vf_ref, v1_ref, v2_ref, v0_ref, p_ref, g_ref, xs_ref) = refs
    else:
        (x_ref, halo_ref, sl_ref, ov_ref, mu_ref, wr_ref, wk_ref, wv_ref, w1_ref, w2_ref, a1_ref, a2_ref, g1_ref, g2_ref,
         w0_ref, a0_ref, p_ref, g_ref, xs_ref) = refs
    d = D_MODEL
    x = x_ref[...]
    r0 = pl.program_id(0) * tm
    xs_ref[0:SUBLANES] = halo_ref[...]
    xs_ref[SUBLANES:SUBLANES + tm] = x
    prev = xs_ref[pl.ds(SUBLANES - 1, tm), :]
    rows = r0 + lax.broadcasted_iota(jnp.int32, (tm, 1), 0)
    off, cnt, length = long_seqs
    b = (jnp.maximum(r0 - off, 0) + (length - 1)) // length
    hit = jnp.logical_and(rows == off + b * length, b < cnt)
    prev = jnp.where(hit, sl_ref[pl.ds(jnp.minimum(b, cnt - 1), 1), :], prev)
    off, cnt, length = short_seqs
    rel = rows - off
    first = jnp.logical_and(jnp.logical_and(rel >= 0, rel < cnt * length), jnp.bitwise_and(rel, length - 1) == 0)
    prev = jnp.where(first, ov_ref[...], prev)
    xx = prev - x
    mix = lambda i: (x + xx * mu_ref[i:i + 1, :]).astype(BF16)
    xr, xw, xk, xv, xa, xg = (mix(i) for i in range(6))
    p_ref[:, 0:d] = _dot(xr, wr_ref[...])
    p_ref[:, d:2 * d] = _dot(xk, wk_ref[...])
    v = _dot(xv, wv_ref[...])
    if has_vres:
        v = v + (vf_ref[...] - v) * _sigmoid(v0_ref[...] + _dot(_dot(xv, v1_ref[...]), v2_ref[...]))
    p_ref[:, 2 * d:3 * d] = v
    p_ref[:, 3 * d:4 * d] = w0_ref[...] + _dot(jnp.tanh(_dot(xw, w1_ref[...])), w2_ref[...])
    p_ref[:, 4 * d:5 * d] = a0_ref[...] + _dot(_dot(xa, a1_ref[...]), a2_ref[...])
    g_ref[...] = _dot(_sigmoid(_dot(xg, g1_ref[...])), g2_ref[...])


def _rwkv_in(x, shift_long, shift_short, consts, vres, *, tm, long_seqs, short_seqs):
    n, d = x.shape
    off, cnt, length = short_seqs
    assert length & (length - 1) == 0 and long_seqs[2] >= tm and tm % SUBLANES == 0
    tile0 = off // tm
    tiles = -(-(off + cnt * length) // tm) - tile0
    ov = jnp.pad(shift_short[:, None], ((0, 0), (0, length - 1), (0, 0))).reshape(cnt * length, d)
    ov = jnp.pad(ov, ((off - tile0 * tm, tiles * tm - (off - tile0 * tm) - cnt * length), (0, 0)))
    sl = jnp.pad(shift_long, ((0, -shift_long.shape[0] % SUBLANES), (0, 0)))
    row = lambda w: pl.BlockSpec((tm, w), lambda i: (i, 0))
    halo = pl.BlockSpec((SUBLANES, d), lambda i: (jnp.maximum(i * (tm // SUBLANES) - 1, 0), 0))
    ov_spec = pl.BlockSpec((tm, d), lambda i: (jnp.clip(i - tile0, 0, tiles - 1), 0))
    args = [x, x, sl, ov] + list(consts)
    specs = [row(d), halo, _const_spec(sl.shape), ov_spec] + [_const_spec(c.shape) for c in consts]
    if vres is not None:
        vf, vconsts = vres
        args += [vf] + list(vconsts)
        specs += [row(d)] + [_const_spec(c.shape) for c in vconsts]
    return pl.pallas_call(
        functools.partial(_rwkv_in_kernel, has_vres=vres is not None, tm=tm, long_seqs=long_seqs, short_seqs=short_seqs),
        out_shape=[jax.ShapeDtypeStruct((n, RWKV_STREAMS * d), F32), jax.ShapeDtypeStruct((n, d), F32)],
        grid=(n // tm,),
        in_specs=specs,
        out_specs=[row(RWKV_STREAMS * d), row(d)],
        scratch_shapes=[pltpu.VMEM((tm + SUBLANES, d), F32)],
        compiler_params=_params("parallel"),
        name="rwkv_in",
    )(*args)


WKV_SEQS = LANES // RWKV_HEADS


def _wkv_kernel(*refs, tt):
    tok_refs = refs[:WKV_SEQS]
    (s0_ref, kk_ref, ka_ref, rk_ref, lg_ref, lb_ref, y_ref, sout_ref,
     s_ref, p_ref, dec_ref, a_ref, b_ref, km_ref, bon_ref) = refs[WKV_SEQS:]
    t = pl.program_id(1)
    hk = RWKV_HEAD

    @pl.when(t == 0)
    def _init():
        s_ref[...] = s0_ref[...]

    def relayout(i, carry):
        for s in range(RWKV_STREAMS):
            rows = jnp.concatenate([tok_refs[q][i, s] for q in range(WKV_SEQS)], axis=0)
            p_ref[i, s] = rows.T
        return carry

    lax.fori_loop(0, tt, relayout, 0)

    r = p_ref[:, 0]
    k = p_ref[:, 1]
    v = p_ref[:, 2]
    dec_ref[...] = jnp.exp(-jnp.exp(-_softplus(-p_ref[:, 3]) - 0.5))
    ag = _sigmoid(p_ref[:, 4])
    kk = k * kk_ref[...][None]
    kk = kk / jnp.maximum(jnp.sqrt(jnp.sum(kk * kk, axis=1, keepdims=True)), 1e-12)
    km = k * (1.0 + (ag - 1.0) * ka_ref[...][None])
    km_ref[...] = km
    a_ref[...] = -kk
    b_ref[...] = kk * ag
    bon_ref[...] = jnp.sum(r * km * rk_ref[...][None], axis=1, keepdims=True) * v
    lg = lg_ref[...]
    lb = lb_ref[...]

    def step(i, carry):
        lanes = 4
        half = hk // 2
        ys = []
        for v0 in (0, half):
            vs = slice(v0, v0 + half)
            parts = [jnp.zeros((half, LANES), F32) for _ in range(lanes)]
            for q in range(hk):
                parts[q % lanes] = parts[q % lanes] + s_ref[q, vs, :] * a_ref[i, q:q + 1, :]
            sa = (parts[0] + parts[1]) + (parts[2] + parts[3])
            vv = p_ref[i, 2, vs, :]
            parts = [jnp.zeros((half, LANES), F32) for _ in range(lanes)]
            for q in range(hk):
                sn = s_ref[q, vs, :] * dec_ref[i, q:q + 1, :] + (sa * b_ref[i, q:q + 1, :] + vv * km_ref[i, q:q + 1, :])
                s_ref[q, vs, :] = sn
                parts[q % lanes] = parts[q % lanes] + sn * p_ref[i, 0, q:q + 1, :]
            ys.append((parts[0] + parts[1]) + (parts[2] + parts[3]))
        y = jnp.concatenate(ys, axis=0)
        mu = jnp.mean(y, axis=0, keepdims=True)
        dlt = y - mu
        var = jnp.mean(dlt * dlt, axis=0, keepdims=True)
        out = dlt * lax.rsqrt(var + RWKV_GN_EPS) * lg + lb + bon_ref[i]
        y_ref[i] = out.T.reshape(WKV_SEQS, RWKV_HEADS, hk)
        return carry

    lax.fori_loop(0, tt, step, 0)

    @pl.when(t == pl.num_programs(1) - 1)
    def _fin():
        sout_ref[...] = s_ref[...]


def _wkv(p, s0, lane_consts, *, off, nb, t, tt):
    hk = RWKV_HEAD
    l = nb * RWKV_HEADS
    assert off % tt == 0 and t % tt == 0 and nb % WKV_SEQS == 0
    p4 = p.reshape(p.shape[0], RWKV_STREAMS, RWKV_HEADS, hk)
    tok = lambda q: pl.BlockSpec((tt, RWKV_STREAMS, RWKV_HEADS, hk),
                                 lambda g, i: ((off + (g * WKV_SEQS + q) * t) // tt + i, 0, 0, 0))
    lane2 = pl.BlockSpec((hk, LANES), lambda g, i: (0, g))
    st = pl.BlockSpec((hk, hk, LANES), lambda g, i: (0, 0, g))
    seq = lambda: pltpu.VMEM((tt, hk, LANES), F32)
    return pl.pallas_call(
        functools.partial(_wkv_kernel, tt=tt),
        out_shape=[jax.ShapeDtypeStruct((t, nb, RWKV_HEADS, hk), F32), jax.ShapeDtypeStruct((hk, hk, l), F32)],
        grid=(l // LANES, t // tt),
        in_specs=[tok(q) for q in range(WKV_SEQS)] + [st] + [lane2] * 5,
        out_specs=[pl.BlockSpec((tt, WKV_SEQS, RWKV_HEADS, hk), lambda g, i: (i, g, 0, 0)), st],
        scratch_shapes=[pltpu.VMEM((hk, hk, LANES), F32), pltpu.VMEM((tt, RWKV_STREAMS, hk, LANES), F32),
                        seq(), seq(), seq(), seq(), seq()],
        compiler_params=_params("parallel", "arbitrary"),
        name="wkv",
    )(*([p4] * WKV_SEQS), s0, *lane_consts)


def _gate_out_kernel(y_ref, g_ref, x_ref, w_ref, lg_ref, lb_ref, o_ref):
    m = _dot(y_ref[...] * g_ref[...], w_ref[...])
    o_ref[...] = _layer_norm_rows(DN_ALPHA * x_ref[...] + m, lg_ref[...], lb_ref[...])


def _gate_out(y, g, x, w, lg, lb, *, tm):
    n, d = x.shape
    row = pl.BlockSpec((tm, d), lambda i: (i, 0))
    consts = [w, lg, lb]
    return pl.pallas_call(
        _gate_out_kernel,
        out_shape=jax.ShapeDtypeStruct((n, d), F32),
        grid=(n // tm,),
        in_specs=[row, row, row] + [_const_spec(c.shape) for c in consts],
        out_specs=row,
        compiler_params=_params("parallel"),
        name="gate_out",
    )(y, g, x, *consts)


def _pad_cols(w):
    return jnp.pad(w, ((0, 0), (0, LANES - w.shape[1]))).astype(BF16)


def _pad_rows(w):
    return jnp.pad(w, ((0, LANES - w.shape[0]), (0, 0))).astype(BF16)


def _rwkv_weights(mu, w_r, w_k, w_v, w_o, w0, w1, w2, a0, a1, a2, g1, g2, k_k, k_a, r_k, lnx_g, lnx_b, vres):
    consts = [jnp.pad(mu, ((0, SUBLANES - mu.shape[0]), (0, 0))), w_r.astype(BF16), w_k.astype(BF16), w_v.astype(BF16),
              _pad_cols(w1), _pad_rows(w2), _pad_cols(a1), _pad_rows(a2), _pad_cols(g1), _pad_rows(g2), w0[None], a0[None]]
    vconsts = None if vres is None else [_pad_cols(vres[1]), _pad_rows(vres[2]), vres[0][None]]
    per_head = [k_k.reshape(RWKV_HEADS, RWKV_HEAD).T, k_a.reshape(RWKV_HEADS, RWKV_HEAD).T, r_k.T,
                lnx_g.reshape(RWKV_HEADS, RWKV_HEAD).T, lnx_b.reshape(RWKV_HEADS, RWKV_HEAD).T]
    return consts, vconsts, per_head, w_o.astype(BF16)


def _rwkv_layer(x, groups, weights, v_first, g, beta, *, tm):
    consts, vconsts, per_head, w_o = weights
    n, d = x.shape
    (off_l, nb_l, t_l, _, _, shift_l), (off_s, nb_s, t_s, _, _, shift_s) = groups
    n_used = nb_l * t_l + nb_s * t_s
    vres = None if vconsts is None else (v_first, vconsts)
    p, gate = _rwkv_in(x, shift_l, shift_s, consts, vres, tm=tm,
                       long_seqs=(off_l, nb_l, t_l), short_seqs=(off_s, nb_s, t_s))
    ys, states = [], []
    for off, nb, t, tt, s, shift in groups:
        lanes = nb * RWKV_HEADS
        hk2 = RWKV_HEAD * RWKV_HEAD
        s0 = jnp.swapaxes(s, 2, 3).reshape(lanes, hk2).T.reshape(RWKV_HEAD, RWKV_HEAD, lanes)
        lane_consts = [jnp.tile(c, (1, nb)) for c in per_head]
        y, s_out = _wkv(p, s0, lane_consts, off=off, nb=nb, t=t, tt=tt)
        ys.append(jnp.swapaxes(y.reshape(t, nb, d), 0, 1).reshape(nb * t, d))
        s_new = jnp.swapaxes(s_out.reshape(hk2, lanes).T.reshape(nb, RWKV_HEADS, RWKV_HEAD, RWKV_HEAD), 2, 3)
        states.append((s_new, x[off:off + nb * t].reshape(nb, t, d)[:, -1]))
    y = jnp.concatenate(ys + [jnp.zeros((n - n_used, d), F32)], axis=0)
    return _gate_out(y, gate, x, w_o, g, beta, tm=tm), states, p[:, 2 * d:3 * d]


TOKEN_TILE = 512
PEER_TOKEN_TILE = 1024
PROMPT_TIME_BLOCK = 688
WKV_TIME_BLOCK = 16


def kernel(x_prompt, x_sample, state_gla, state_lru_h, state_lru_conv, state_rwkv, state_rwkv_shift, meta_tokens, ln_g, ln_b, ev_w_in, ev_gla_w_gate, ev_gla_b_gate, ev_gla_norm, ev_conv_w, ev_conv_b, ev_lru_wa, ev_lru_ba, ev_lru_wx, ev_lru_bx, ev_lru_lambda, ev_w_out, od_mu, od_w_r, od_w_k, od_w_v, od_w_o, od_w0, od_w1, od_w2, od_a0, od_a1, od_a2, od_v0, od_v1, od_v2, od_g1, od_g2, od_k_k, od_k_a, od_r_k, od_lnx_g, od_lnx_b, peer_w_q, peer_keys, peer_u, peer_v):
    bp, sp, d = x_prompt.shape
    bs, ss, _ = x_sample.shape
    tp = sp + N_META
    n_p, n_s = bp * tp, bs * ss
    n = -(-(n_p + n_s) // PEER_TOKEN_TILE) * PEER_TOKEN_TILE
    assert tp % PROMPT_TIME_BLOCK == 0 and tp % GLA_TIME_BLOCK == 0 and tp % WKV_TIME_BLOCK == 0 and n % TOKEN_TILE == 0

    xp = jnp.concatenate([jnp.broadcast_to(meta_tokens[None], (bp, N_META, d)), x_prompt], axis=1)
    x = jnp.concatenate([xp.reshape(n_p, d), x_sample.reshape(n_s, d), jnp.zeros((n - n_p - n_s, d), F32)], axis=0)

    n_pairs = DEPTH // 2
    zeros = lambda *s: jnp.zeros(s, F32)
    outs = {k: [] for k in ("p_gla", "p_h", "p_conv", "p_rwkv", "p_shift", "s_gla", "s_h", "s_conv", "s_rwkv", "s_shift")}
    v_first = None
    for layer in range(DEPTH):
        j = layer // 2
        g, beta = ln_g[layer, 0][None], ln_b[layer, 0][None]
        if layer % 2 == 0:
            weights = _even_weights(ev_w_in[j], ev_gla_w_gate[j], ev_gla_b_gate[j], ev_gla_norm[j], ev_conv_w[j], ev_conv_b[j],
                                    ev_lru_wa[j], ev_lru_ba[j], ev_lru_wx[j], ev_lru_bx[j], ev_lru_lambda[j], ev_w_out[j])
            groups = [(0, bp, tp, PROMPT_TIME_BLOCK, zeros(bp, GLA_HEADS, GLA_DK, GLA_DV), zeros(bp, LRU_WIDTH),
                       zeros(bp, CONV_W - 1, LRU_WIDTH)),
                      (n_p, bs, ss, None, state_gla[j], state_lru_h[j], state_lru_conv[j])]
            x, states = _even_layer(x, groups, weights, g, beta, tm=TOKEN_TILE)
            for pre, st in zip("ps", states):
                outs[pre + "_gla"].append(st[0])
                outs[pre + "_h"].append(st[1])
                outs[pre + "_conv"].append(st[2])
        else:
            vres = None if j == 0 else (od_v0[j - 1], od_v1[j - 1], od_v2[j - 1])
            weights = _rwkv_weights(od_mu[j], od_w_r[j], od_w_k[j], od_w_v[j], od_w_o[j], od_w0[j], od_w1[j], od_w2[j],
                                    od_a0[j], od_a1[j], od_a2[j], od_g1[j], od_g2[j], od_k_k[j], od_k_a[j], od_r_k[j],
                                    od_lnx_g[j], od_lnx_b[j], vres)
            groups = [(0, bp, tp, WKV_TIME_BLOCK, zeros(bp, RWKV_HEADS, RWKV_HEAD, RWKV_HEAD), zeros(bp, d)),
                      (n_p, bs, ss, ss, state_rwkv[j], state_rwkv_shift[j])]
            x, states, v = _rwkv_layer(x, groups, weights, v_first, g, beta, tm=TOKEN_TILE)
            if v_first is None:
                v_first = v
            for pre, st in zip("ps", states):
                outs[pre + "_rwkv"].append(st[0])
                outs[pre + "_shift"].append(st[1])
        x = _peer_layer(x, peer_w_q[layer].T.astype(BF16), peer_keys[layer, :, 0].astype(BF16),
                        peer_keys[layer, :, 1].astype(BF16), peer_u[layer].astype(BF16), peer_v[layer].astype(BF16),
                        ln_g[layer, 1][None], ln_b[layer, 1][None], tn=PEER_TOKEN_TILE)

    y_prompt = x[:n_p].reshape(bp, tp, d)[:, N_META:]
    y_sample = x[n_p:n_p + n_s].reshape(bs, ss, d)
    st = {k: jnp.stack(v) for k, v in outs.items()}
    return (y_prompt, y_sample, st["p_gla"], st["p_h"], st["p_conv"], st["p_rwkv"], st["p_shift"],
            st["s_gla"], st["s_h"], st["s_conv"], st["s_rwkv"], st["s_shift"])
```
